```python
import math
import jax
import jax.numpy as jnp
from jax import lax
import numpy as np

D_MODEL = 1024
BATCH = 16
SEQ = 256
DEPTH = 4
DEC_BATCH = 4
DEC_SEQ = 4096
PAST_LEN = 256

GRID_W = 64
N_EVEN = (DEPTH + 1) // 2
N_ODD = DEPTH // 2
A_HEADS = 4
A_DK = 128
A_DV = 128
A_WIDTH = A_HEADS * A_DK
HGRN_CHUNK = 64
B_HEADS = 8
B_KV_HEADS = 2
HEAD_DIM = 64
B_GROUP = B_HEADS // B_KV_HEADS
B_WIDTH = B_HEADS * HEAD_DIM
KV_WIDTH = B_KV_HEADS * HEAD_DIM
WINDOW = 128
ATT_BLOCK = 128
ROPE_THETA = 10000.0
IN_WIDTH = 5 * A_WIDTH + B_WIDTH + 2 * KV_WIDTH
MIX_WIDTH = A_WIDTH + B_WIDTH
S5_GROUP = 16
S5_GROUPS = D_MODEL // S5_GROUP
S5_STATE = 64
N_EXPERTS = 16
EC_FACTOR = 2
D_FF_EXPERT = D_MODEL
EPS = 1e-6
NEG_INF = -1e30

kernel_name = 'hybrid_diffusion_prefix_denoise_step'


def rmsnorm(x, w):
    xf = x.astype(jnp.float32)
    y = xf * lax.rsqrt(jnp.mean(xf * xf, axis=-1, keepdims=True) + EPS)
    return (y * w.astype(jnp.float32)).astype(x.dtype)


def hgrn_lower_bounds(logits):
    pr = jax.nn.softmax(logits.astype(jnp.float32), axis=0)
    cs = jnp.cumsum(pr, axis=0)
    return cs - cs[:1]


def hgrn2_chunk_scan(q, k, v, logf, s0):
    bsz, nh, L, dk = q.shape
    n_chunks = L // HGRN_CHUNK

    def to_chunks(t):
        return jnp.moveaxis(t.reshape(bsz, nh, n_chunks, HGRN_CHUNK, t.shape[-1]), 2, 0)

    tri = jnp.tril(jnp.ones((HGRN_CHUNK, HGRN_CHUNK), dtype=bool))

    def step(S, blk):
        qb, kb, vb, gb = blk
        b = jnp.cumsum(gb, axis=2)
        diff = b[:, :, :, None, :] - b[:, :, None, :, :]
        decay = jnp.exp(jnp.where(tri[:, :, None], diff, NEG_INF))
        att = jnp.einsum('bhtd,bhsd,bhtsd->bhts', qb, kb, decay)
        o = jnp.einsum('bhts,bhse->bhte', att, vb) + jnp.einsum('bhtd,bhde->bhte', qb * jnp.exp(b), S)
        b_last = b[:, :, -1:, :]
        S = jnp.exp(b_last)[:, :, 0, :, None] * S + jnp.einsum('bhsd,bhse->bhde', kb * jnp.exp(b_last - b), vb)
        return S, o

    S, o = lax.scan(step, s0, (to_chunks(q), to_chunks(k), to_chunks(v), to_chunks(logf)))
    return jnp.moveaxis(o, 0, 2).reshape(bsz, nh, L, v.shape[-1]), S


def s5_discretise(a_re, a_im, log_dt, b_re, b_im):
    a_re = a_re.astype(jnp.float32)
    a_im = a_im.astype(jnp.float32)
    b_re = b_re.astype(jnp.float32)
    b_im = b_im.astype(jnp.float32)
    dt = jnp.exp(log_dt.astype(jnp.float32))[:, None]
    mag = jnp.exp(dt * a_re)
    ab_re = mag * jnp.cos(dt * a_im)
    ab_im = mag * jnp.sin(dt * a_im)
    den = a_re * a_re + a_im * a_im
    f_re = ((ab_re - 1.0) * a_re + ab_im * a_im) / den
    f_im = (ab_im * a_re - (ab_re - 1.0) * a_im) / den
    bb_re = f_re[..., None] * b_re - f_im[..., None] * b_im
    bb_im = f_re[..., None] * b_im + f_im[..., None] * b_re
    return ab_re, ab_im, bb_re, bb_im


def _complex_affine_combine(e1, e2):
    a1r, a1i, b1r, b1i = e1
    a2r, a2i, b2r, b2i = e2
    return (a1r * a2r - a1i * a2i, a1r * a2i + a1i * a2r,
            a2r * b1r - a2i * b1i + b2r, a2r * b1i + a2i * b1r + b2i)


def s5_scan(u, ab_re, ab_im, bb_re, bb_im, s0_re, s0_im):
    L = u.shape[1]
    bu_re = jnp.einsum('blgc,gpc->lbgp', u, bb_re)
    bu_im = jnp.einsum('blgc,gpc->lbgp', u, bb_im)
    bu_re = bu_re.at[0].add(ab_re * s0_re - ab_im * s0_im)
    bu_im = bu_im.at[0].add(ab_re * s0_im + ab_im * s0_re)
    a_re = jnp.broadcast_to(ab_re, (L, 1) + ab_re.shape)
    a_im = jnp.broadcast_to(ab_im, (L, 1) + ab_im.shape)
    _, _, s_re, s_im = lax.associative_scan(_complex_affine_combine, (a_re, a_im, bu_re, bu_im))
    return s_re, s_im


def axial_rope(L):
    rows = L // GRID_W
    row = jnp.repeat(jnp.arange(rows, dtype=jnp.float32), GRID_W)
    col = jnp.tile(jnp.arange(GRID_W, dtype=jnp.float32), rows)
    n_pair = HEAD_DIM // 4
    freqs = ROPE_THETA ** (-jnp.arange(n_pair, dtype=jnp.float32) / n_pair)
    ang = jnp.concatenate([row[:, None] * freqs, col[:, None] * freqs], axis=-1)
    return jnp.cos(ang), jnp.sin(ang)


def apply_rope(x, cos, sin):
    xf = x.astype(jnp.float32).reshape(x.shape[:-1] + (HEAD_DIM // 2, 2))
    c = cos[None, :, None, :]
    s = sin[None, :, None, :]
    x1 = xf[..., 0]
    x2 = xf[..., 1]
    out = jnp.stack([x1 * c - x2 * s, x1 * s + x2 * c], axis=-1)
    return out.reshape(x.shape).astype(x.dtype)


def context_attention(q, k, v, sink):
    bsz, L = q.shape[:2]
    qg = q.reshape(bsz, L, B_KV_HEADS, B_GROUP, HEAD_DIM)
    s = jnp.einsum('blkgd,bkmd->bkglm', qg, k).astype(jnp.float32) * HEAD_DIM ** -0.5
    s_sink = jnp.broadcast_to(sink.astype(jnp.float32).reshape(1, B_KV_HEADS, B_GROUP, 1, 1), s.shape[:-1] + (1,))
    p = jax.nn.softmax(jnp.concatenate([s, s_sink], axis=-1), axis=-1).astype(v.dtype)
    o = jnp.einsum('bkglm,bkmd->blkgd', p[..., :-1], v)
    return o.reshape(bsz, L, B_WIDTH)


def latent_attention(q, k, v, k_ctx, v_ctx, sink):
    bsz, L = q.shape[:2]
    nb = L // ATT_BLOCK
    span = ATT_BLOCK + 2 * WINDOW
    scale = HEAD_DIM ** -0.5
    qb = q.reshape(bsz, nb, ATT_BLOCK, B_KV_HEADS, B_GROUP, HEAD_DIM)
    pad = ((0, 0), (WINDOW, WINDOW), (0, 0), (0, 0))
    kpos = jnp.arange(nb)[:, None] * ATT_BLOCK + jnp.arange(span)[None, :]
    kb = jnp.pad(k, pad)[:, kpos]
    vb = jnp.pad(v, pad)[:, kpos]
    rel = jnp.arange(span)[None, :] - jnp.arange(ATT_BLOCK)[:, None]
    valid = ((rel >= 0) & (rel <= 2 * WINDOW))[None] & ((kpos >= WINDOW) & (kpos < L + WINDOW))[:, None, :]
    s_loc = jnp.einsum('bnqkgd,bnskd->bnkgqs', qb, kb).astype(jnp.float32) * scale
    s_loc = jnp.where(valid[None, :, None, None], s_loc, NEG_INF)
    s_ctx = jnp.einsum('bnqkgd,bkmd->bnkgqm', qb, k_ctx).astype(jnp.float32) * scale
    s_sink = jnp.broadcast_to(sink.astype(jnp.float32).reshape(1, 1, B_KV_HEADS, B_GROUP, 1, 1), s_loc.shape[:-1] + (1,))
    p = jax.nn.softmax(jnp.concatenate([s_loc, s_ctx, s_sink], axis=-1), axis=-1).astype(v.dtype)
    n_ctx = k_ctx.shape[2]
    o = (jnp.einsum('bnkgqs,bnskd->bnqkgd', p[..., :span], vb)
         + jnp.einsum('bnkgqm,bkmd->bnqkgd', p[..., span:span + n_ctx], v_ctx))
    return o.reshape(bsz, L, B_WIDTH)


def expert_choice_ffn(x, w_router, w_gate, w_up, w_down):
    n_tok = x.shape[0]
    cap = EC_FACTOR * n_tok // N_EXPERTS
    aff = jax.nn.softmax((x @ w_router).astype(jnp.float32), axis=-1)
    g, idx = lax.top_k(aff.T, cap)
    xs = x[idx]
    h = jax.nn.silu(jnp.einsum('ecd,edf->ecf', xs, w_gate)) * jnp.einsum('ecd,edf->ecf', xs, w_up)
    out = jnp.einsum('ecf,efd->ecd', h, w_down) * g[..., None].astype(x.dtype)
    return jnp.zeros_like(x).at[idx.reshape(-1)].add(out.reshape(-1, D_MODEL))


def even_mixer(h, p, e, lb, rope, cache):
    bsz, L, _ = h.shape
    proj = h @ p['w_in_ab'][e]
    splits = [A_WIDTH, 2 * A_WIDTH, 3 * A_WIDTH, 4 * A_WIDTH, 5 * A_WIDTH,
              5 * A_WIDTH + B_WIDTH, 5 * A_WIDTH + B_WIDTH + KV_WIDTH]
    q_a, f_fw, f_bw, i_a, g_a, q_b, k_b, v_b = jnp.split(proj, splits, axis=-1)

    def heads(t):
        return t.reshape(bsz, L, A_HEADS, A_DK).transpose(0, 2, 1, 3).astype(jnp.float32)

    q = jax.nn.silu(heads(q_a))
    v = heads(i_a)
    lbh = lb.reshape(2, 1, A_HEADS, 1, A_DK)
    forget = lbh + (1.0 - lbh) * jax.nn.sigmoid(jnp.stack([heads(f_fw), heads(f_bw)]))
    k = 1.0 - forget
    logf = jnp.log(forget)
    if cache is None:
        s0 = jnp.zeros((2, bsz, A_HEADS, A_DK, A_DV), jnp.float32)
    else:
        s0 = jnp.moveaxis(cache[2], 1, 0).astype(jnp.float32)
    o_f, s_f = hgrn2_chunk_scan(q, k[0], v, logf[0], s0[0])
    o_b, s_b = hgrn2_chunk_scan(jnp.flip(q, 2), jnp.flip(k[1], 2), jnp.flip(v, 2), jnp.flip(logf[1], 2), s0[1])
    o = (o_f + jnp.flip(o_b, 2)).transpose(0, 2, 1, 3)
    gate = jax.nn.silu(g_a.reshape(bsz, L, A_HEADS, A_DV).astype(jnp.float32))
    o_a = (rmsnorm(o, p['hgrn_norm_w'][e]) * gate).reshape(bsz, L, A_WIDTH).astype(h.dtype)

    qh = rmsnorm(q_b.reshape(bsz, L, B_HEADS, HEAD_DIM), p['q_norm_w'][e])
    kh = rmsnorm(k_b.reshape(bsz, L, B_KV_HEADS, HEAD_DIM), p['k_norm_w'][e])
    vh = v_b.reshape(bsz, L, B_KV_HEADS, HEAD_DIM)
    sink = p['attn_sink'][e]
    if cache is None:
        k_ctx = kh.transpose(0, 2, 1, 3)
        v_ctx = vh.transpose(0, 2, 1, 3)
        o_att = context_attention(qh, k_ctx, v_ctx, sink)
        state = (k_ctx, v_ctx, jnp.stack([s_f, s_b], axis=1))
    else:
        cos, sin = rope
        o_att = latent_attention(apply_rope(qh, cos, sin), apply_rope(kh, cos, sin), vh, cache[0], cache[1], sink)
        state = None
    y = jnp.concatenate([o_a, o_att.astype(h.dtype)], axis=-1) @ p['w_out_ab'][e]
    return y, state


def odd_mixer(h, p, o, cache):
    bsz, L, _ = h.shape
    u = h.astype(jnp.float32).reshape(bsz, L, S5_GROUPS, S5_GROUP)
    ys = []
    finals = []
    for d in range(2):
        ab_re, ab_im, bb_re, bb_im = s5_discretise(p['s5_a_re'][o, d], p['s5_a_im'][o, d], p['s5_log_dt'][o, d],
                                                   p['s5_b_re'][o, d], p['s5_b_im'][o, d])
        if cache is None:
            s0_re = jnp.zeros((bsz, S5_GROUPS, S5_STATE), jnp.float32)
            s0_im = s0_re
        else:
            s0_re = cache[:, d, 0].astype(jnp.float32)
            s0_im = cache[:, d, 1].astype(jnp.float32)
        ud = u if d == 0 else jnp.flip(u, 1)
        s_re, s_im = s5_scan(ud, ab_re, ab_im, bb_re, bb_im, s0_re, s0_im)
        if cache is None:
            finals.append(jnp.stack([s_re[-1], s_im[-1]], axis=1))
        c_re = p['s5_c_re'][o, d].astype(jnp.float32)
        c_im = p['s5_c_im'][o, d].astype(jnp.float32)
        y = jnp.einsum('lbgp,gcp->blgc', s_re, c_re) - jnp.einsum('lbgp,gcp->blgc', s_im, c_im)
        ys.append(y if d == 0 else jnp.flip(y, 1))
    y = (ys[0] + ys[1]).reshape(bsz, L, D_MODEL) + p['s5_d'][o].astype(jnp.float32) * u.reshape(bsz, L, D_MODEL)
    y = jax.nn.gelu(y).astype(h.dtype)
    out = (y @ p['glu_w_a'][o]) * jax.nn.sigmoid(y @ p['glu_w_b'][o])
    state = jnp.stack(finals, axis=1) if cache is None else None
    return out, state


def trunk(x, cond, p, ctx_cache):
    is_ctx = ctx_cache is None
    bsz, L, _ = x.shape
    rope = None if is_ctx else axial_rope(L)
    lower_bounds = hgrn_lower_bounds(p['hgrn_lb_logits'])
    ks, vs, hs, ss = [], [], [], []
    for l in range(DEPTH):
        mod = jax.nn.silu(cond) @ p['ada_w'][l] + p['ada_b'][l]
        sh1, sc1, g1, sh2, sc2, g2 = jnp.split(mod[:, None, :], 6, axis=-1)
        h = rmsnorm(x, p['norm_w'][l, 0]) * (1.0 + sc1) + sh1
        if l % 2 == 0:
            e = l // 2
            cache = None if is_ctx else (ctx_cache[0][:, e], ctx_cache[1][:, e], ctx_cache[2][:, e])
            y, st = even_mixer(h, p, e, lower_bounds[e], rope, cache)
            if is_ctx:
                ks.append(st[0])
                vs.append(st[1])
                hs.append(st[2])
        else:
            o = l // 2
            cache = None if is_ctx else ctx_cache[3][:, o]
            y, st = odd_mixer(h, p, o, cache)
            if is_ctx:
                ss.append(st)
        x = x + g1 * y
        h = rmsnorm(x, p['norm_w'][l, 1]) * (1.0 + sc2) + sh2
        y = expert_choice_ffn(h.reshape(bsz * L, D_MODEL), p['router_w'][l], p['exp_w_gate'][l],
                              p['exp_w_up'][l], p['exp_w_down'][l])
        x = x + g2 * y.reshape(bsz, L, D_MODEL)
    return x, ks, vs, hs, ss


def setup_inputs(seed: int = 0) -> dict:
    key = jax.random.key(seed)
    ks = jax.random.split(key, 40)

    def nrm(k, shape, s=1.0):
        return s * jax.random.normal(k, shape, jnp.float32)

    G, P = S5_GROUPS, S5_STATE
    n_idx = jnp.arange(P, dtype=jnp.float32)
    return {
        'x_prompt': nrm(ks[0], (BATCH, SEQ, D_MODEL)),
        'x_sample': nrm(ks[1], (DEC_BATCH, DEC_SEQ, D_MODEL)),
        'cache_k': nrm(ks[2], (DEC_BATCH, N_EVEN, B_KV_HEADS, PAST_LEN, HEAD_DIM)),
        'cache_v': nrm(ks[3], (DEC_BATCH, N_EVEN, B_KV_HEADS, PAST_LEN, HEAD_DIM)),
        'state_hgrn': nrm(ks[4], (DEC_BATCH, N_EVEN, 2, A_HEADS, A_DK, A_DV), 0.5),
        'state_s5': nrm(ks[5], (DEC_BATCH, N_ODD, 2, 2, G, P), 0.2),
        'c': nrm(ks[6], (DEC_BATCH, D_MODEL)),
        'c_ctx': nrm(ks[7], (D_MODEL,)),
        'norm_w': 1.0 + nrm(ks[8], (DEPTH, 2, D_MODEL), 0.02),
        'ada_w': nrm(ks[9], (DEPTH, D_MODEL, 6 * D_MODEL), 0.5 * D_MODEL ** -0.5),
        'ada_b': nrm(ks[10], (DEPTH, 6 * D_MODEL), 0.02),
        'w_in_ab': nrm(ks[11], (N_EVEN, D_MODEL, IN_WIDTH), D_MODEL ** -0.5),
        'w_out_ab': nrm(ks[12], (N_EVEN, MIX_WIDTH, D_MODEL), MIX_WIDTH ** -0.5),
        'hgrn_lb_logits': nrm(ks[13], (N_EVEN, 2, A_HEADS * A_DK)),
        'hgrn_norm_w': 1.0 + nrm(ks[14], (N_EVEN, A_DV), 0.02),
        'q_norm_w': 1.0 + nrm(ks[15], (N_EVEN, HEAD_DIM), 0.02),
        'k_norm_w': 1.0 + nrm(ks[16], (N_EVEN, HEAD_DIM), 0.02),
        'attn_sink': nrm(ks[17], (N_EVEN, B_HEADS), 0.5),
        's5_a_re': -0.5 + nrm(ks[18], (N_ODD, 2, G, P), 0.01),
        's5_a_im': math.pi * n_idx + nrm(ks[19], (N_ODD, 2, G, P), 0.01),
        's5_log_dt': jax.random.uniform(ks[20], (N_ODD, 2, G), jnp.float32, math.log(1e-3), math.log(1e-1)),
        's5_b_re': nrm(ks[21], (N_ODD, 2, G, P, S5_GROUP), (2 * S5_GROUP) ** -0.5),
        's5_b_im': nrm(ks[22], (N_ODD, 2, G, P, S5_GROUP), (2 * S5_GROUP) ** -0.5),
        's5_c_re': nrm(ks[23], (N_ODD, 2, G, S5_GROUP, P), P ** -0.5),
        's5_c_im': nrm(ks[24], (N_ODD, 2, G, S5_GROUP, P), P ** -0.5),
        's5_d': nrm(ks[25], (N_ODD, D_MODEL)),
        'glu_w_a': nrm(ks[26], (N_ODD, D_MODEL, D_MODEL), D_MODEL ** -0.5),
        'glu_w_b': nrm(ks[27], (N_ODD, D_MODEL, D_MODEL), D_MODEL ** -0.5),
        'router_w': nrm(ks[28], (DEPTH, D_MODEL, N_EXPERTS), D_MODEL ** -0.5),
        'exp_w_gate': nrm(ks[29], (DEPTH, N_EXPERTS, D_MODEL, D_FF_EXPERT), D_MODEL ** -0.5),
        'exp_w_up': nrm(ks[30], (DEPTH, N_EXPERTS, D_MODEL, D_FF_EXPERT), D_MODEL ** -0.5),
        'exp_w_down': nrm(ks[31], (DEPTH, N_EXPERTS, D_FF_EXPERT, D_MODEL), D_FF_EXPERT ** -0.5),
    }


def reference(x_prompt, x_sample, cache_k, cache_v, state_hgrn, state_s5, c, c_ctx,
              norm_w, ada_w, ada_b, w_in_ab, w_out_ab, hgrn_lb_logits, hgrn_norm_w,
              q_norm_w, k_norm_w, attn_sink, s5_a_re, s5_a_im, s5_log_dt, s5_b_re, s5_b_im,
              s5_c_re, s5_c_im, s5_d, glu_w_a, glu_w_b, router_w, exp_w_gate, exp_w_up, exp_w_down):
    p = dict(norm_w=norm_w, ada_w=ada_w, ada_b=ada_b, w_in_ab=w_in_ab, w_out_ab=w_out_ab,
             hgrn_lb_logits=hgrn_lb_logits, hgrn_norm_w=hgrn_norm_w, q_norm_w=q_norm_w,
             k_norm_w=k_norm_w, attn_sink=attn_sink, s5_a_re=s5_a_re, s5_a_im=s5_a_im,
             s5_log_dt=s5_log_dt, s5_b_re=s5_b_re, s5_b_im=s5_b_im, s5_c_re=s5_c_re,
             s5_c_im=s5_c_im, s5_d=s5_d, glu_w_a=glu_w_a, glu_w_b=glu_w_b, router_w=router_w,
             exp_w_gate=exp_w_gate, exp_w_up=exp_w_up, exp_w_down=exp_w_down)
    cond_ctx = jnp.broadcast_to(c_ctx[None, :], (x_prompt.shape[0], D_MODEL))
    y_prompt, ks, vs, hs, ss = trunk(x_prompt, cond_ctx, p, None)
    new_cache_k = jnp.stack(ks, axis=1)
    new_cache_v = jnp.stack(vs, axis=1)
    new_state_hgrn = jnp.stack(hs, axis=1)
    new_state_s5 = jnp.stack(ss, axis=1)
    y_sample, _, _, _, _ = trunk(x_sample, c, p, (cache_k, cache_v, state_hgrn, state_s5))
    return (y_prompt, y_sample, new_cache_k, new_cache_v, new_state_hgrn, new_state_s5)
```

```python
import functools
import math

import jax
import jax.numpy as jnp
from jax import lax
from jax.experimental import pallas as pl
from jax.experimental.pallas import tpu as pltpu

F32 = jnp.float32
BF16 = jnp.bfloat16

A_HEADS = 4
A_DK = 128
A_WIDTH = A_HEADS * A_DK
B_HEADS = 8
B_KV_HEADS = 2
HEAD_DIM = 64
B_GROUP = B_HEADS // B_KV_HEADS
B_WIDTH = B_HEADS * HEAD_DIM
KV_WIDTH = B_KV_HEADS * HEAD_DIM
WINDOW = 128
GRID_W = 64
ROPE_THETA = 10000.0
S5_GROUP = 16
S5_STATE = 64
S5_CHUNK = 16
N_EXPERTS = 16
EC_FACTOR = 2
EPS = 1e-6
NEG_INF = -1e30

HGRN_C = 64
HGRN_SB = 16
TOK_TILE = 256
LANES = 128
VMEM_LIMIT = 56 * 1024 * 1024


def _cparams(sem):
    return pltpu.CompilerParams(dimension_semantics=sem, vmem_limit_bytes=VMEM_LIMIT)


def _sigmoid(x):
    return 1.0 / (1.0 + jnp.exp(-x))


def _silu(x):
    return x * _sigmoid(x)


def _norm_mod(x, nw, sc, sh):
    ms = jnp.mean(x * x, axis=-1, keepdims=True)
    return (x * lax.rsqrt(ms + EPS) * nw) * (1.0 + sc) + sh


def _dot(a, b):
    return jnp.dot(a, b, preferred_element_type=F32)


def _dot_nt(a, b):
    return lax.dot_general(a, b, (((1,), (1,)), ((), ())), preferred_element_type=F32)


def _split3(x):
    hi = x.astype(BF16)
    r1 = x - hi.astype(F32)
    mid = r1.astype(BF16)
    lo = (r1 - mid.astype(F32)).astype(BF16)
    return hi, mid, lo


def _ada_kernel(c_ref, w_ref, b_ref, o_ref):
    s = _silu(c_ref[...])
    o_ref[0] = _dot(s.astype(BF16), w_ref[0].astype(BF16)) + b_ref[0]


def ada_modulation(cond, ada_w, ada_b):
    depth, d, n = ada_w.shape
    rows = cond.shape[0]
    tn = 1536
    return pl.pallas_call(
        _ada_kernel,
        grid=(depth, n // tn),
        in_specs=[pl.BlockSpec((rows, d), lambda l, j: (0, 0)),
                  pl.BlockSpec((1, d, tn), lambda l, j: (l, 0, j)),
                  pl.BlockSpec((1, 1, tn), lambda l, j: (l, 0, j))],
        out_specs=pl.BlockSpec((1, rows, tn), lambda l, j: (l, 0, j)),
        out_shape=jax.ShapeDtypeStruct((depth, rows, n), F32),
        compiler_params=_cparams(("arbitrary", "arbitrary")),
        name="ada_modulation",
    )(cond, ada_w, ada_b.reshape(depth, 1, n))


def _mod_spec(rowmap, k, d):
    return pl.BlockSpec((1, 1, d), lambda i: (rowmap(i), 0, k))


def _inproj_kernel(x_ref, nw_ref, sc_ref, sh_ref, w_ref, o_ref):
    h = _norm_mod(x_ref[...], nw_ref[...], sc_ref[0], sh_ref[0])
    o_ref[...] = _dot(h.astype(BF16), w_ref[...])


def norm_mod_matmul(x, nw, mod, rowmap, k_sc, k_sh, w_bf16):
    n_tok, d = x.shape
    n = w_bf16.shape[1]
    tm = TOK_TILE
    return pl.pallas_call(
        _inproj_kernel,
        grid=(n_tok // tm,),
        in_specs=[pl.BlockSpec((tm, d), lambda i: (i, 0)),
                  pl.BlockSpec((1, d), lambda i: (0, 0)),
                  _mod_spec(rowmap, k_sc, d), _mod_spec(rowmap, k_sh, d),
                  pl.BlockSpec((d, n), lambda i: (0, 0))],
        out_specs=pl.BlockSpec((tm, n), lambda i: (i, 0)),
        out_shape=jax.ShapeDtypeStruct((n_tok, n), F32),
        compiler_params=_cparams(("arbitrary",)),
        name="norm_mod_matmul",
    )(x, nw, mod, mod, w_bf16)


def _norm_mod_kernel(x_ref, nw_ref, sc_ref, sh_ref, o_ref):
    o_ref[...] = _norm_mod(x_ref[...], nw_ref[...], sc_ref[0], sh_ref[0])


def norm_mod(x, nw, mod, rowmap, k_sc, k_sh):
    n_tok, d = x.shape
    tm = TOK_TILE
    return pl.pallas_call(
        _norm_mod_kernel,
        grid=(n_tok // tm,),
        in_specs=[pl.BlockSpec((tm, d), lambda i: (i, 0)),
                  pl.BlockSpec((1, d), lambda i: (0, 0)),
                  _mod_spec(rowmap, k_sc, d), _mod_spec(rowmap, k_sh, d)],
        out_specs=pl.BlockSpec((tm, d), lambda i: (i, 0)),
        out_shape=jax.ShapeDtypeStruct((n_tok, d), F32),
        compiler_params=_cparams(("arbitrary",)),
        name="norm_mod",
    )(x, nw, mod, mod)


def _outproj_kernel(of_ref, ob_ref, ga_ref, oatt_ref, x_ref, g1_ref, hw_ref, w_ref, o_ref):
    o = of_ref[...] + ob_ref[...]
    gate = _silu(ga_ref[...])
    hw = hw_ref[...]
    parts = []
    for h in range(A_HEADS):
        sl = slice(h * A_DK, (h + 1) * A_DK)
        oh = o[:, sl]
        ms = jnp.mean(oh * oh, axis=-1, keepdims=True)
        parts.append(((oh * lax.rsqrt(ms + EPS) * hw) * gate[:, sl]).astype(BF16))
    parts.append(oatt_ref[...].astype(BF16))
    lhs = jnp.concatenate(parts, axis=1)
    y = _dot(lhs, w_ref[...])
    o_ref[...] = x_ref[...] + g1_ref[0] * y


def even_out_proj(o_f, o_b, proj, o_att, x, mod, rowmap, hw, w_bf16):
    n_tok, d = x.shape
    tm = TOK_TILE
    aw = A_WIDTH
    return pl.pallas_call(
        _outproj_kernel,
        grid=(n_tok // tm,),
        in_specs=[pl.BlockSpec((tm, aw), lambda i: (i, 0)),
                  pl.BlockSpec((tm, aw), lambda i: (i, 0)),
                  pl.BlockSpec((tm, aw), lambda i: (i, 4)),
                  pl.BlockSpec((tm, B_WIDTH), lambda i: (i, 0)),
                  pl.BlockSpec((tm, d), lambda i: (i, 0)),
                  _mod_spec(rowmap, 2, d),
                  pl.BlockSpec((1, A_DK), lambda i: (0, 0)),
                  pl.BlockSpec((aw + B_WIDTH, d), lambda i: (0, 0))],
        out_specs=pl.BlockSpec((tm, d), lambda i: (i, 0)),
        out_shape=jax.ShapeDtypeStruct((n_tok, d), F32),
        compiler_params=_cparams(("arbitrary",)),
        name="even_out_proj",
    )(o_f, o_b, proj, o_att, x, mod, hw, w_bf16)


def _glu_kernel(y_ref, x_ref, g1_ref, wa_ref, wb_ref, o_ref):
    y = jax.nn.gelu(y_ref[...], approximate=True).astype(BF16)
    a = _dot(y, wa_ref[...])
    b = _dot(y, wb_ref[...])
    o_ref[...] = x_ref[...] + g1_ref[0] * (a * _sigmoid(b))


def glu_residual(y, x, mod, rowmap, wa_bf16, wb_bf16):
    n_tok, d = x.shape
    tm = TOK_TILE
    return pl.pallas_call(
        _glu_kernel,
        grid=(n_tok // tm,),
        in_specs=[pl.BlockSpec((tm, d), lambda i: (i, 0)),
                  pl.BlockSpec((tm, d), lambda i: (i, 0)),
                  _mod_spec(rowmap, 2, d),
                  pl.BlockSpec((d, d), lambda i: (0, 0)),
                  pl.BlockSpec((d, d), lambda i: (0, 0))],
        out_specs=pl.BlockSpec((tm, d), lambda i: (i, 0)),
        out_shape=jax.ShapeDtypeStruct((n_tok, d), F32),
        compiler_params=_cparams(("arbitrary",)),
        name="glu_residual",
    )(y, x, mod, wa_bf16, wb_bf16)


def _hgrn_chunk(q, k, v, g, st, msel, rev):
    c = q.shape[0]
    nb = c // HGRN_SB
    row = lax.broadcasted_iota(jnp.int32, (c, c), 0)
    col = lax.broadcasted_iota(jnp.int32, (c, c), 1)
    tri = jnp.where((col >= row) if rev else (col <= row), 1.0, 0.0).astype(BF16)
    gh, gm, gl = _split3(g)
    b = _dot(tri, gh) + _dot(tri, gm) + _dot(tri, gl)
    b_edge = b[0:1] if rev else b[c - 1:c]
    qs = q * jnp.exp(b)
    kdec = k * jnp.exp(b_edge - b)

    slabs = []
    for i in range(nb):
        sl = slice(i * HGRN_SB, (i + 1) * HGRN_SB)
        bi, qi, ki = b[sl], q[sl], k[sl]
        pieces = []
        for s in range(HGRN_SB):
            e = jnp.exp(jnp.minimum(bi - bi[s:s + 1], 0.0))
            pieces.append(((qi * e) * ki[s:s + 1]).astype(BF16))
        slabs.append(jnp.concatenate(pieces, axis=1))
    a_loc = _dot(jnp.concatenate(slabs, axis=0), msel)

    lane = lax.broadcasted_iota(jnp.int32, (HGRN_SB, LANES), 1)
    rloc = lax.broadcasted_iota(jnp.int32, (HGRN_SB, LANES), 0)
    dmask = ((lane >= rloc) & (lane < HGRN_SB)) if rev else (lane <= rloc)
    krow = lax.broadcasted_iota(jnp.int32, (c, A_DK), 0)
    att_rows = []
    for i in range(nb):
        sl = slice(i * HGRN_SB, (i + 1) * HGRN_SB)
        a_d = jnp.where(dmask, a_loc[sl], 0.0)
        if i > 0:
            a_d = pltpu.roll(a_d, i * HGRN_SB, axis=1)
        a_i = a_d[:, :c]
        has_off = (i < nb - 1) if rev else (i > 0)
        if has_off:
            edge = (i + 1) * HGRN_SB if rev else i * HGRN_SB
            r = b[edge:edge + 1] if rev else b[edge - 1:edge]
            qp = q[sl] * jnp.exp(b[sl] - r)
            live = (krow >= edge) if rev else (krow < edge)
            kp = jnp.where(live, k * jnp.exp(jnp.minimum(r - b, 0.0)), 0.0)
            a_i = a_i + _dot_nt(qp.astype(BF16), kp.astype(BF16))
        att_rows.append(a_i)
    att = jnp.concatenate(att_rows, axis=0)

    vb = v.astype(BF16)
    o = _dot(att.astype(BF16), vb) + _dot_nt(qs.astype(BF16), st.astype(BF16))
    st_new = st * jnp.exp(b_edge) + _dot(v.T.astype(BF16), kdec.astype(BF16))
    return o, st_new


def _hgrn_kernel(qf_ref, vf_ref, ff_ref, qb_ref, vb_ref, fb_ref, lbl_ref, msel_ref, s0_ref,
                 of_ref, ob_ref, sout_ref, st_ref, *, layer):
    c_idx = pl.program_id(1)

    @pl.when(c_idx == 0)
    def _():
        st_ref[...] = s0_ref[0]

    lg = lbl_ref[...]
    ex = jnp.exp(lg - jnp.max(lg, axis=0, keepdims=True))
    pr = ex / jnp.sum(ex, axis=0, keepdims=True)
    lb = jnp.zeros_like(pr[0])
    for e in range(1, layer + 1):
        lb = lb + pr[e]
    msel = msel_ref[...]

    for d, (q_ref, v_ref, f_ref, o_ref) in enumerate(((qf_ref, vf_ref, ff_ref, of_ref),
                                                       (qb_ref, vb_ref, fb_ref, ob_ref))):
        qa = q_ref[0]
        q_all = _silu(qa)
        v_all = v_ref[0]
        lbd = lb[d:d + 1]
        forget = lbd + (1.0 - lbd) * _sigmoid(f_ref[0])
        k_all = 1.0 - forget
        g_all = jnp.log(forget)
        for h in range(A_HEADS):
            sl = slice(h * A_DK, (h + 1) * A_DK)
            o, st_new = _hgrn_chunk(q_all[:, sl], k_all[:, sl], v_all[:, sl], g_all[:, sl],
                                    st_ref[d, h], msel, rev=(d == 1))
            o_ref[0, :, sl] = o
            st_ref[d, h] = st_new

    @pl.when(c_idx == pl.num_programs(1) - 1)
    def _():
        sout_ref[0] = st_ref[...]


def hgrn2_mixer(proj, lb_logits, s0t, layer):
    bsz, seq, _ = proj.shape
    c = HGRN_C
    nc = seq // c
    aw = A_WIDTH
    msel = jnp.repeat(jnp.eye(HGRN_SB, LANES, dtype=BF16), A_DK, axis=0)

    def fwd(col):
        return pl.BlockSpec((1, c, aw), lambda b, i: (b, i, col))

    def bwd(col):
        return pl.BlockSpec((1, c, aw), lambda b, i: (b, nc - 1 - i, col))

    st_spec = pl.BlockSpec((1, 2, A_HEADS, A_DK, A_DK), lambda b, i: (b, 0, 0, 0, 0))
    return pl.pallas_call(
        functools.partial(_hgrn_kernel, layer=layer),
        grid=(bsz, nc),
        in_specs=[fwd(0), fwd(3), fwd(1), bwd(0), bwd(3), bwd(2),
                  pl.BlockSpec(lb_logits.shape, lambda b, i: (0, 0, 0)),
                  pl.BlockSpec(msel.shape, lambda b, i: (0, 0)),
                  st_spec],
        out_specs=[pl.BlockSpec((1, c, aw), lambda b, i: (b, i, 0)),
                   pl.BlockSpec((1, c, aw), lambda b, i: (b, nc - 1 - i, 0)),
                   st_spec],
        out_shape=[jax.ShapeDtypeStruct((bsz, seq, aw), F32),
                   jax.ShapeDtypeStruct((bsz, seq, aw), F32),
                   jax.ShapeDtypeStruct((bsz, 2, A_HEADS, A_DK, A_DK), F32)],
        scratch_shapes=[pltpu.VMEM((2, A_HEADS, A_DK, A_DK), F32)],
        compiler_params=_cparams(("arbitrary", "arbitrary")),
        name="hgrn2_mixer",
    )(proj, proj, proj, proj, proj, proj, lb_logits, msel, s0t)


def _head_norm(x, w, gmat):
    hi, mid, lo = _split3(x * x)
    ms = _dot(hi, gmat) + _dot(mid, gmat) + _dot(lo, gmat)
    return x * lax.rsqrt(ms + EPS) * w


def _rope(x, cos, sin_signed):
    width = x.shape[1]
    lane = lax.broadcasted_iota(jnp.int32, x.shape, 1)
    nxt = pltpu.roll(x, width - 1, axis=1)
    prv = pltpu.roll(x, 1, axis=1)
    partner = jnp.where(lane % 2 == 0, nxt, prv)
    return x * cos + partner * sin_signed


def _qkprep_kernel(*refs, rope):
    if rope:
        q_ref, k_ref, qw_ref, kw_ref, gm_ref, cos_ref, sin_ref, qo_ref, ko_ref = refs
    else:
        q_ref, k_ref, qw_ref, kw_ref, gm_ref, qo_ref, ko_ref = refs
    gm = gm_ref[...]
    qn = _head_norm(q_ref[0], qw_ref[...], gm)
    kn = _head_norm(k_ref[0], kw_ref[...], gm[:KV_WIDTH, :KV_WIDTH])
    if rope:
        cos = cos_ref[...]
        sin = sin_ref[...]
        qn = _rope(qn, cos, sin)
        kn = _rope(kn, cos[:, :KV_WIDTH], sin[:, :KV_WIDTH])
    qo_ref[0] = (qn * (HEAD_DIM ** -0.5)).astype(BF16)
    ko_ref[0] = kn


def qk_prepare(proj, qw, kw, rope_tabs):
    bsz, seq, _ = proj.shape
    tm = 256
    qcol = (5 * A_WIDTH) // B_WIDTH
    kcol = (5 * A_WIDTH + B_WIDTH) // KV_WIDTH
    gidx = jnp.arange(B_WIDTH) // HEAD_DIM
    gmat = jnp.where(gidx[:, None] == gidx[None, :], 1.0 / HEAD_DIM, 0.0).astype(BF16)
    qw_t = jnp.tile(qw, B_HEADS).reshape(1, B_WIDTH)
    kw_t = jnp.tile(kw, B_KV_HEADS).reshape(1, KV_WIDTH)
    in_specs = [pl.BlockSpec((1, tm, B_WIDTH), lambda b, i: (b, i, qcol)),
                pl.BlockSpec((1, tm, KV_WIDTH), lambda b, i: (b, i, kcol)),
                pl.BlockSpec((1, B_WIDTH), lambda b, i: (0, 0)),
                pl.BlockSpec((1, KV_WIDTH), lambda b, i: (0, 0)),
                pl.BlockSpec((B_WIDTH, B_WIDTH), lambda b, i: (0, 0))]
    args = [proj, proj, qw_t, kw_t, gmat]
    if rope_tabs is not None:
        in_specs += [pl.BlockSpec((tm, B_WIDTH), lambda b, i: (i, 0)),
                     pl.BlockSpec((tm, B_WIDTH), lambda b, i: (i, 0))]
        args += list(rope_tabs)
    return pl.pallas_call(
        functools.partial(_qkprep_kernel, rope=rope_tabs is not None),
        grid=(bsz, seq // tm),
        in_specs=in_specs,
        out_specs=[pl.BlockSpec((1, tm, B_WIDTH), lambda b, i: (b, i, 0)),
                   pl.BlockSpec((1, tm, KV_WIDTH), lambda b, i: (b, i, 0))],
        out_shape=[jax.ShapeDtypeStruct((bsz, seq, B_WIDTH), BF16),
                   jax.ShapeDtypeStruct((bsz, seq, KV_WIDTH), F32)],
        compiler_params=_cparams(("arbitrary", "arbitrary")),
        name="qk_prepare",
    )(*args)


def rope_tables(seq):
    pos = jnp.arange(seq)
    row = (pos // GRID_W).astype(F32)
    col = (pos % GRID_W).astype(F32)
    n_pair = HEAD_DIM // 4
    freqs = ROPE_THETA ** (-jnp.arange(n_pair, dtype=F32) / n_pair)
    ang = jnp.concatenate([row[:, None] * freqs, col[:, None] * freqs], axis=-1)
    cos = jnp.repeat(jnp.cos(ang), 2, axis=-1)
    sin = jnp.repeat(jnp.sin(ang), 2, axis=-1) * jnp.tile(jnp.array([-1.0, 1.0], F32), HEAD_DIM // 2)
    return jnp.tile(cos, (1, B_HEADS)), jnp.tile(sin, (1, B_HEADS))


def _softmax_av(s, sink, v):
    m = jnp.maximum(jnp.max(s, axis=1, keepdims=True), sink)
    p = jnp.exp(s - m)
    den = jnp.sum(p, axis=1, keepdims=True) + jnp.exp(sink - m)
    return _dot(p.astype(BF16), v) / den


def _ctx_attn_kernel(q_ref, k_ref, v_ref, sink_ref, o_ref):
    q = q_ref[0]
    k = k_ref[0].astype(BF16)
    v = v_ref[0].astype(BF16)
    for kvh in range(B_KV_HEADS):
        ks = slice(kvh * HEAD_DIM, (kvh + 1) * HEAD_DIM)
        for gq in range(B_GROUP):
            h = kvh * B_GROUP + gq
            hs = slice(h * HEAD_DIM, (h + 1) * HEAD_DIM)
            s = _dot_nt(q[:, hs], k[:, ks])
            o_ref[0, :, hs] = _softmax_av(s, sink_ref[h:h + 1, 0:1], v[:, ks])


def context_attention(qn, kn, proj, sink):
    bsz, seq, _ = qn.shape
    vcol = (5 * A_WIDTH + B_WIDTH + KV_WIDTH) // KV_WIDTH
    sink_t = jnp.broadcast_to(sink.reshape(B_HEADS, 1), (B_HEADS, LANES))
    return pl.pallas_call(
        _ctx_attn_kernel,
        grid=(bsz,),
        in_specs=[pl.BlockSpec((1, seq, B_WIDTH), lambda b: (b, 0, 0)),
                  pl.BlockSpec((1, seq, KV_WIDTH), lambda b: (b, 0, 0)),
                  pl.BlockSpec((1, seq, KV_WIDTH), lambda b: (b, 0, vcol)),
                  pl.BlockSpec((B_HEADS, LANES), lambda b: (0, 0))],
        out_specs=pl.BlockSpec((1, seq, B_WIDTH), lambda b: (b, 0, 0)),
        out_shape=jax.ShapeDtypeStruct((bsz, seq, B_WIDTH), F32),
        compiler_params=_cparams(("arbitrary",)),
        name="context_attention",
    )(qn, kn, proj, sink_t)


def _lat_attn_kernel(q_ref, kp_ref, kc_ref, kn_ref, vp_ref, vc_ref, vn_ref, kx_ref, vx_ref, sink_ref, o_ref):
    blk = pl.program_id(1)
    nblk = pl.num_programs(1)
    tq = q_ref.shape[1]
    q = q_ref[0]
    kl = jnp.concatenate([kp_ref[0], kc_ref[0], kn_ref[0]], axis=0).astype(BF16)
    vl = jnp.concatenate([vp_ref[0], vc_ref[0], vn_ref[0]], axis=0).astype(BF16)
    n_ctx = kx_ref.shape[2]
    span = 3 * tq
    i = lax.broadcasted_iota(jnp.int32, (tq, span + n_ctx), 0)
    j = lax.broadcasted_iota(jnp.int32, (tq, span + n_ctx), 1)
    dist = j - tq - i
    valid = (dist >= -WINDOW) & (dist <= WINDOW)
    valid = valid & ((j >= tq) | (blk > 0)) & ((j < 2 * tq) | (blk < nblk - 1))
    valid = valid | (j >= span)
    for kvh in range(B_KV_HEADS):
        ks = slice(kvh * HEAD_DIM, (kvh + 1) * HEAD_DIM)
        kk = jnp.concatenate([kl[:, ks], kx_ref[0, kvh].astype(BF16)], axis=0)
        vv = jnp.concatenate([vl[:, ks], vx_ref[0, kvh].astype(BF16)], axis=0)
        for gq in range(B_GROUP):
            h = kvh * B_GROUP + gq
            hs = slice(h * HEAD_DIM, (h + 1) * HEAD_DIM)
            s = jnp.where(valid, _dot_nt(q[:, hs], kk), NEG_INF)
            o_ref[0, :, hs] = _softmax_av(s, sink_ref[h:h + 1, 0:1], vv)


def latent_attention(qr, kr, proj, k_ctx, v_ctx, sink):
    bsz, seq, _ = qr.shape
    tq = WINDOW
    nblk = seq // tq
    vcol = (5 * A_WIDTH + B_WIDTH + KV_WIDTH) // KV_WIDTH
    n_ctx = k_ctx.shape[2]
    sink_t = jnp.broadcast_to(sink.reshape(B_HEADS, 1), (B_HEADS, LANES))

    def kv_specs(col):
        return [pl.BlockSpec((1, tq, KV_WIDTH), lambda b, i: (b, jnp.maximum(i - 1, 0), col)),
                pl.BlockSpec((1, tq, KV_WIDTH), lambda b, i: (b, i, col)),
                pl.BlockSpec((1, tq, KV_WIDTH), lambda b, i: (b, jnp.minimum(i + 1, nblk - 1), col))]

    ctx_spec = pl.BlockSpec((1, B_KV_HEADS, n_ctx, HEAD_DIM), lambda b, i: (b, 0, 0, 0))
    return pl.pallas_call(
        _lat_attn_kernel,
        grid=(bsz, nblk),
        in_specs=[pl.BlockSpec((1, tq, B_WIDTH), lambda b, i: (b, i, 0))] + kv_specs(0) + kv_specs(vcol)
                 + [ctx_spec, ctx_spec, pl.BlockSpec((B_HEADS, LANES), lambda b, i: (0, 0))],
        out_specs=pl.BlockSpec((1, tq, B_WIDTH), lambda b, i: (b, i, 0)),
        out_shape=jax.ShapeDtypeStruct((bsz, seq, B_WIDTH), F32),
        compiler_params=_cparams(("arbitrary", "arbitrary")),
        name="latent_attention",
    )(qr, kr, kr, kr, proj, proj, proj, k_ctx, v_ctx, sink_t)


def s5_operators(a_re, a_im, log_dt, b_re, b_im, c_re, c_im):
    t = S5_CHUNK
    hi = lax.Precision.HIGHEST
    ks, ws, vs, a1s, a2s = [], [], [], [], []
    for d in range(2):
        are, aim = a_re[d].astype(F32), a_im[d].astype(F32)
        dt = jnp.exp(log_dt[d].astype(F32))[:, None]
        den = are * are + aim * aim
        steps = jnp.arange(t + 1, dtype=F32)[:, None, None]
        mag = jnp.exp(steps * (dt * are))
        pw_re = mag * jnp.cos(steps * (dt * aim))
        pw_im = mag * jnp.sin(steps * (dt * aim))
        ab_re, ab_im = pw_re[1], pw_im[1]
        f_re = ((ab_re - 1.0) * are + ab_im * aim) / den
        f_im = (ab_im * are - (ab_re - 1.0) * aim) / den
        bre, bim = b_re[d].astype(F32), b_im[d].astype(F32)
        bb_re = f_re[..., None] * bre - f_im[..., None] * bim
        bb_im = f_re[..., None] * bim + f_im[..., None] * bre
        cre, cim = c_re[d].astype(F32), c_im[d].astype(F32)
        cp_re = cre[:, None] * pw_re.transpose(1, 0, 2)[:, :, None, :] - cim[:, None] * pw_im.transpose(1, 0, 2)[:, :, None, :]
        cp_im = cre[:, None] * pw_im.transpose(1, 0, 2)[:, :, None, :] + cim[:, None] * pw_re.transpose(1, 0, 2)[:, :, None, :]
        m = (jnp.einsum('gkcp,gpd->gkcd', cp_re[:, :t], bb_re, precision=hi)
             - jnp.einsum('gkcp,gpd->gkcd', cp_im[:, :t], bb_im, precision=hi))
        s_i = jnp.arange(t)[:, None]
        t_i = jnp.arange(t)[None, :]
        lag = (t_i - s_i) if d == 0 else (s_i - t_i)
        blk = jnp.where((lag >= 0)[None, :, :, None, None], m[:, jnp.clip(lag, 0, t - 1)], 0.0)
        ks.append(blk.transpose(0, 1, 4, 2, 3).reshape(-1, t * S5_GROUP, t * S5_GROUP))
        pidx = (t - 1 - jnp.arange(t)) if d == 0 else jnp.arange(t)
        pr = pw_re[pidx].transpose(1, 0, 2)[:, :, None, :]
        pi = pw_im[pidx].transpose(1, 0, 2)[:, :, None, :]
        bbr = bb_re.transpose(0, 2, 1)[:, None]
        bbi = bb_im.transpose(0, 2, 1)[:, None]
        w_re = pr * bbr - pi * bbi
        w_im = pr * bbi + pi * bbr
        ws.append(jnp.concatenate([w_re, w_im], axis=-1).reshape(-1, t * S5_GROUP, 2 * S5_STATE))
        kidx = (jnp.arange(t) + 1) if d == 0 else (t - jnp.arange(t))
        e_re = cp_re[:, kidx]
        e_im = cp_im[:, kidx]
        v = jnp.concatenate([e_re, -e_im], axis=-1)
        vs.append(v.transpose(0, 3, 1, 2).reshape(-1, 2 * S5_STATE, t * S5_GROUP))
        a1s.append(jnp.concatenate([pw_re[t], pw_re[t]], axis=-1))
        a2s.append(jnp.concatenate([-pw_im[t], pw_im[t]], axis=-1))
    return (ks[0] + ks[1], jnp.concatenate(ws, axis=-1), jnp.concatenate(vs, axis=1),
            jnp.concatenate(a1s, axis=-1), jnp.concatenate(a2s, axis=-1))


def _s5_state_kernel(u_ref, w_ref, o_ref):
    gb = u_ref.shape[0]
    n = w_ref.shape[2]
    for g in range(gb):
        o_ref[:, g * n:(g + 1) * n] = _dot(u_ref[g].astype(BF16), w_ref[g])


def s5_chunk_states(u, w_bf16):
    g, r, tc = u.shape
    n = w_bf16.shape[2]
    gb = 8
    tr = min(r, 512)
    return pl.pallas_call(
        _s5_state_kernel,
        grid=(g // gb, r // tr),
        in_specs=[pl.BlockSpec((gb, tr, tc), lambda i, j: (i, j, 0)),
                  pl.BlockSpec((gb, tc, n), lambda i, j: (i, 0, 0))],
        out_specs=pl.BlockSpec((tr, gb * n), lambda i, j: (j, i)),
        out_shape=jax.ShapeDtypeStruct((r, g * n), F32),
        compiler_params=_cparams(("arbitrary", "arbitrary")),
        name="s5_chunk_states",
    )(u, w_bf16)


def _s5_scan_kernel(d_ref, a1_ref, a2_ref, s0_ref, st_ref, fin_ref):
    nj = d_ref.shape[0]
    half = d_ref.shape[2] // 2
    a1 = a1_ref[...]
    a2 = a2_ref[...]
    a1f, a1b = a1[:, :half], a1[:, half:]
    a2f, a2b = a2[:, :half], a2[:, half:]
    s0 = s0_ref[...]

    def step(i, carry):
        sf, sb = carry
        jb = nj - 1 - i
        st_ref[i, :, :half] = sf
        st_ref[jb, :, half:] = sb
        sf = a1f * sf + a2f * pltpu.roll(sf, half // 2, axis=1) + d_ref[i, :, :half]
        sb = a1b * sb + a2b * pltpu.roll(sb, half // 2, axis=1) + d_ref[jb, :, half:]
        return sf, sb

    sf, sb = lax.fori_loop(0, nj, step, (s0[:, :half], s0[:, half:]))
    fin_ref[:, :half] = sf
    fin_ref[:, half:] = sb


def s5_chunk_scan(dv, a1, a2, s0):
    nj, rows, n = dv.shape
    tr = 32
    return pl.pallas_call(
        _s5_scan_kernel,
        grid=(rows // tr,),
        in_specs=[pl.BlockSpec((nj, tr, n), lambda i: (0, i, 0)),
                  pl.BlockSpec((tr, n), lambda i: (i, 0)),
                  pl.BlockSpec((tr, n), lambda i: (i, 0)),
                  pl.BlockSpec((tr, n), lambda i: (i, 0))],
        out_specs=[pl.BlockSpec((nj, tr, n), lambda i: (0, i, 0)),
                   pl.BlockSpec((tr, n), lambda i: (i, 0))],
        out_shape=[jax.ShapeDtypeStruct((nj, rows, n), F32),
                   jax.ShapeDtypeStruct((rows, n), F32)],
        compiler_params=_cparams(("arbitrary",)),
        name="s5_chunk_scan",
    )(dv, a1, a2, s0)


def _s5_out_kernel(u_ref, s_ref, k_ref, v_ref, dsk_ref, o_ref):
    gb = u_ref.shape[0]
    n = v_ref.shape[1]
    for g in range(gb):
        u = u_ref[g]
        y = _dot(u.astype(BF16), k_ref[g]) + _dot(s_ref[:, g * n:(g + 1) * n].astype(BF16), v_ref[g])
        o_ref[g] = y + dsk_ref[g] * u


def s5_outputs(u, states, k_bf16, v_bf16, dskip):
    g, r, tc = u.shape
    n = v_bf16.shape[1]
    gb = 8
    tr = min(r, 512)
    return pl.pallas_call(
        _s5_out_kernel,
        grid=(g // gb, r // tr),
        in_specs=[pl.BlockSpec((gb, tr, tc), lambda i, j: (i, j, 0)),
                  pl.BlockSpec((tr, gb * n), lambda i, j: (j, i)),
                  pl.BlockSpec((gb, tc, tc), lambda i, j: (i, 0, 0)),
                  pl.BlockSpec((gb, n, tc), lambda i, j: (i, 0, 0)),
                  pl.BlockSpec((gb, 1, tc), lambda i, j: (i, 0, 0))],
        out_specs=pl.BlockSpec((gb, tr, tc), lambda i, j: (i, j, 0)),
        out_shape=jax.ShapeDtypeStruct((g, r, tc), F32),
        compiler_params=_cparams(("arbitrary", "arbitrary")),
        name="s5_outputs",
    )(u, states, k_bf16, v_bf16, dskip)


def s5_mixer(h, ops, dskip, s0):
    bsz, seq, d = h.shape
    g = d // S5_GROUP
    t = S5_CHUNK
    nj = seq // t
    k_tot, w_tot, v_tot, a1, a2 = ops
    u = h.reshape(bsz, nj, t, g, S5_GROUP).transpose(3, 1, 0, 2, 4).reshape(g, nj * bsz, t * S5_GROUP)
    dv = s5_chunk_states(u, w_tot.astype(BF16))
    n = 4 * S5_STATE
    s0r = s0.transpose(0, 3, 1, 2, 4).reshape(bsz * g, n)
    states, final = s5_chunk_scan(dv.reshape(nj, bsz * g, n), jnp.tile(a1, (bsz, 1)), jnp.tile(a2, (bsz, 1)), s0r)
    dsk = jnp.tile(dskip.reshape(g, 1, S5_GROUP), (1, 1, t))
    y = s5_outputs(u, states.reshape(nj * bsz, g * n), k_tot.astype(BF16), v_tot.astype(BF16), dsk)
    y = y.reshape(g, nj, bsz, t, S5_GROUP).transpose(2, 1, 3, 0, 4).reshape(bsz, seq, d)
    final = final.reshape(bsz, g, 2, 2, S5_STATE).transpose(0, 2, 3, 1, 4)
    return y, final


def _router_kernel(x_ref, nw_ref, sc_ref, sh_ref, wr_ref, h_ref, aff_ref):
    h = _norm_mod(x_ref[...], nw_ref[...], sc_ref[0], sh_ref[0])
    h_ref[...] = h
    logits = _dot_nt(wr_ref[...], h.astype(BF16))
    ex = jnp.exp(logits - jnp.max(logits, axis=0, keepdims=True))
    p = ex / jnp.sum(ex, axis=0, keepdims=True)
    for k in range(aff_ref.shape[0]):
        aff_ref[k] = p[:, k * LANES:(k + 1) * LANES]


def moe_router(x, nw, mod, rowmap, wr_t_bf16):
    n_tok, d = x.shape
    tm = TOK_TILE
    ne = wr_t_bf16.shape[0]
    return pl.pallas_call(
        _router_kernel,
        grid=(n_tok // tm,),
        in_specs=[pl.BlockSpec((tm, d), lambda i: (i, 0)),
                  pl.BlockSpec((1, d), lambda i: (0, 0)),
                  _mod_spec(rowmap, 4, d), _mod_spec(rowmap, 3, d),
                  pl.BlockSpec((ne, d), lambda i: (0, 0))],
        out_specs=[pl.BlockSpec((tm, d), lambda i: (i, 0)),
                   pl.BlockSpec((tm // LANES, ne, LANES), lambda i: (i, 0, 0))],
        out_shape=[jax.ShapeDtypeStruct((n_tok, d), F32),
                   jax.ShapeDtypeStruct((n_tok // LANES, ne, LANES), F32)],
        compiler_params=_cparams(("arbitrary",)),
        name="moe_router",
    )(x, nw, mod, mod, wr_t_bf16)


def _select_kernel(aff_ref, ut_ref, pos_ref, inc_ref, *, cap):
    nt, ne, _ = aff_ref.shape
    bits = lax.bitcast_convert_type(aff_ref[...], jnp.int32)

    def count(mask):
        c = jnp.sum(jnp.where(mask, 1.0, 0.0), axis=0)
        return jnp.sum(c, axis=1, keepdims=True)

    def radix(k, thr):
        cand = thr | (jnp.int32(1) << (30 - k))
        return jnp.where(count(bits >= cand[None]) >= cap, cand, thr)

    thr = lax.fori_loop(0, 31, radix, jnp.zeros((ne, 1), jnp.int32))
    gt = bits > thr[None]
    eq = bits == thr[None]
    need = cap - count(gt)
    ut = ut_ref[...]

    def excl_rank(mask):
        m = jnp.where(mask, 1.0, 0.0)
        inc_ref[...] = _dot(m.reshape(nt * ne, LANES).astype(BF16), ut).reshape(nt, ne, LANES)

        def body(t, carry):
            inc = inc_ref[t]
            inc_ref[t] = inc + carry
            return carry + inc[:, LANES - 1:LANES]

        lax.fori_loop(0, nt, body, jnp.zeros((ne, 1), F32))
        return inc_ref[...] - m

    sel = gt | (eq & (excl_rank(eq) < need[None]))
    rank = excl_rank(sel)
    pos_ref[...] = jnp.where(sel, rank, float(cap)).astype(jnp.int32)


def moe_select(aff_t, cap):
    nt, ne, _ = aff_t.shape
    ut = jnp.triu(jnp.ones((LANES, LANES), BF16))
    return pl.pallas_call(
        functools.partial(_select_kernel, cap=cap),
        grid=(1,),
        in_specs=[pl.BlockSpec((nt, ne, LANES), lambda i: (0, 0, 0)),
                  pl.BlockSpec((LANES, LANES), lambda i: (0, 0))],
        out_specs=pl.BlockSpec((nt, ne, LANES), lambda i: (0, 0, 0)),
        out_shape=jax.ShapeDtypeStruct((nt, ne, LANES), jnp.int32),
        scratch_shapes=[pltpu.VMEM((nt, ne, LANES), F32)],
        compiler_params=_cparams(("arbitrary",)),
        name="moe_select",
    )(aff_t, ut)


def _expert_kernel(idx_ref, h_hbm, wr_ref, wg_ref, wu_ref, wd_ref, y_ref, xbuf, wgb, wub, wdb, sem):
    e = pl.program_id(0)
    ch = pl.program_id(1)
    tr = xbuf.shape[0]

    @pl.when(ch == 0)
    def _():
        wgb[...] = wg_ref[0].astype(BF16)
        wub[...] = wu_ref[0].astype(BF16)
        wdb[...] = wd_ref[0].astype(BF16)

    def row_copy(r, tok):
        return pltpu.make_async_copy(h_hbm.at[pl.ds(tok, 1)], xbuf.at[pl.ds(r, 1)], sem)

    def issue(r, carry):
        row_copy(r, idx_ref[e, ch * tr + r]).start()
        return carry

    def drain(r, carry):
        row_copy(r, 0).wait()
        return carry

    lax.fori_loop(0, tr, issue, 0)
    lax.fori_loop(0, tr, drain, 0)

    x = xbuf[...].astype(BF16)
    logits = _dot(x, wr_ref[...])
    lane = lax.broadcasted_iota(jnp.int32, logits.shape, 1)
    ne = pl.num_programs(0)
    logits = jnp.where(lane < ne, logits, NEG_INF)
    ex = jnp.exp(logits - jnp.max(logits, axis=1, keepdims=True))
    gate = jnp.sum(jnp.where(lane == e, ex, 0.0), axis=1, keepdims=True) / jnp.sum(ex, axis=1, keepdims=True)
    hmid = (_silu(_dot(x, wgb[...])) * _dot(x, wub[...])).astype(BF16)
    y_ref[0] = _dot(hmid, wdb[...]) * gate


def moe_experts(idx, h, wr_pad_bf16, w_gate, w_up, w_down):
    ne, rows = idx.shape
    d = h.shape[1]
    f = w_gate.shape[2]
    tr = 512
    return pl.pallas_call(
        _expert_kernel,
        grid_spec=pltpu.PrefetchScalarGridSpec(
            num_scalar_prefetch=1,
            grid=(ne, rows // tr),
            in_specs=[pl.BlockSpec(memory_space=pl.ANY),
                      pl.BlockSpec((d, LANES), lambda e, c, idx: (0, 0)),
                      pl.BlockSpec((1, d, f), lambda e, c, idx: (e, 0, 0)),
                      pl.BlockSpec((1, d, f), lambda e, c, idx: (e, 0, 0)),
                      pl.BlockSpec((1, f, d), lambda e, c, idx: (e, 0, 0))],
            out_specs=pl.BlockSpec((1, tr, d), lambda e, c, idx: (e, c, 0)),
            scratch_shapes=[pltpu.VMEM((tr, d), F32),
                            pltpu.VMEM((d, f), BF16), pltpu.VMEM((d, f), BF16), pltpu.VMEM((f, d), BF16),
                            pltpu.SemaphoreType.DMA(())]),
        out_shape=jax.ShapeDtypeStruct((ne, rows, d), F32),
        compiler_params=_cparams(("arbitrary", "arbitrary")),
        name="moe_experts",
    )(idx, h, wr_pad_bf16, w_gate, w_up, w_down)


def _combine_kernel(starts_ref, idx_ref, y_hbm, x_ref, g2_ref, o_ref, acc, stage, sem):
    tb = pl.program_id(0)
    tm = acc.shape[0]
    ne = y_hbm.shape[0]
    acc[...] = jnp.zeros_like(acc)

    def chunk_copy(e, src_row, dst_row):
        return pltpu.make_async_copy(y_hbm.at[e, pl.ds(src_row, 8)], stage.at[pl.ds(dst_row, 8)], sem)

    spans = []
    off = jnp.int32(0)
    for e in range(ne):
        s0 = starts_ref[e, tb]
        s1 = starts_ref[e, tb + 1]
        a = (s0 // 8) * 8
        nchunk = jnp.where(s1 > s0, (s1 - a + 7) // 8, 0)

        def issue(c, carry, e=e, a=a, off=off):
            chunk_copy(e, pl.multiple_of(a + 8 * c, 8), pl.multiple_of(off + 8 * c, 8)).start()
            return carry

        lax.fori_loop(0, nchunk, issue, 0)
        spans.append((s0, s1, off - a))
        off = off + 8 * nchunk

    def drain(c, carry):
        chunk_copy(0, 0, 0).wait()
        return carry

    lax.fori_loop(0, off // 8, drain, 0)

    for e in range(ne):
        s0, s1, shift = spans[e]

        def add_row(r, carry, e=e, shift=shift):
            n = idx_ref[e, r] - tb * tm
            acc[pl.ds(n, 1), :] = acc[pl.ds(n, 1), :] + stage[pl.ds(r + shift, 1), :]
            return carry

        lax.fori_loop(s0, s1, add_row, 0)

    o_ref[...] = x_ref[...] + g2_ref[0] * acc[...]


def moe_combine(starts, idx, y, x, mod, rowmap):
    n_tok, d = x.shape
    ne = y.shape[0]
    tm = TOK_TILE
    stage_rows = ne * tm + ne * 16
    return pl.pallas_call(
        _combine_kernel,
        grid_spec=pltpu.PrefetchScalarGridSpec(
            num_scalar_prefetch=2,
            grid=(n_tok // tm,),
            in_specs=[pl.BlockSpec(memory_space=pl.ANY),
                      pl.BlockSpec((tm, d), lambda i, s, ix: (i, 0)),
                      pl.BlockSpec((1, 1, d), lambda i, s, ix: (rowmap(i), 0, 5))],
            out_specs=pl.BlockSpec((tm, d), lambda i, s, ix: (i, 0)),
            scratch_shapes=[pltpu.VMEM((tm, d), F32), pltpu.VMEM((stage_rows, d), F32),
                            pltpu.SemaphoreType.DMA(())]),
        out_shape=jax.ShapeDtypeStruct((n_tok, d), F32),
        compiler_params=_cparams(("arbitrary",)),
        name="moe_combine",
    )(starts, idx, y, x, mod)


def moe_layer(x, nw, mod, rowmap, groups, wr, w_gate, w_up, w_down):
    n_tok, d = x.shape
    ne = wr.shape[1]
    h, aff_t = moe_router(x, nw, mod, rowmap, wr.T.astype(BF16))
    idx_parts = []
    tile_counts = []
    for first, count in groups:
        cap = EC_FACTOR * count // ne
        pos = moe_select(aff_t[first // LANES:(first + count) // LANES], cap)
        pos_e = pos.transpose(1, 0, 2).reshape(ne, count)
        tok = jnp.broadcast_to(jnp.arange(count, dtype=jnp.int32) + first, (ne, count))
        lists = jnp.zeros((ne, cap), jnp.int32).at[jnp.arange(ne)[:, None], pos_e].set(tok, mode='drop')
        idx_parts.append(lists)
        chosen = (pos_e < cap).astype(jnp.int32).reshape(ne, count // TOK_TILE, TOK_TILE)
        tile_counts.append(jnp.sum(chosen, axis=-1))
    idx = jnp.concatenate(idx_parts, axis=1)
    counts = jnp.concatenate(tile_counts, axis=1)
    starts = jnp.concatenate([jnp.zeros((ne, 1), jnp.int32), jnp.cumsum(counts, axis=1)], axis=1).astype(jnp.int32)
    wr_pad = jnp.zeros((d, LANES), BF16).at[:, :ne].set(wr.astype(BF16))
    y = moe_experts(idx, h, wr_pad, w_gate, w_up, w_down)
    return moe_combine(starts, idx, y, x, mod, rowmap)


def kernel(x_prompt, x_sample, cache_k, cache_v, state_hgrn, state_s5, c, c_ctx, norm_w, ada_w, ada_b, w_in_ab, w_out_ab, hgrn_lb_logits, hgrn_norm_w, q_norm_w, k_norm_w, attn_sink, s5_a_re, s5_a_im, s5_log_dt, s5_b_re, s5_b_im, s5_c_re, s5_c_im, s5_d, glu_w_a, glu_w_b, router_w, exp_w_gate, exp_w_up, exp_w_down):
    b_ctx, l_ctx, d = x_prompt.shape
    b_lat, l_lat, _ = x_sample.shape
    depth = norm_w.shape[0]
    n_ctx = b_ctx * l_ctx
    n_lat = b_lat * l_lat
    tiles_ctx_seq = l_ctx // TOK_TILE
    tiles_lat_seq = l_lat // TOK_TILE
    ctx_tiles = n_ctx // TOK_TILE

    def rowmap(i):
        return jnp.where(i < ctx_tiles, i // tiles_ctx_seq, b_ctx + (i - ctx_tiles) // tiles_lat_seq)

    cond = jnp.concatenate([c_ctx[None, :], c, jnp.zeros((8 - 1 - b_lat, d), F32)], axis=0)
    mod_small = ada_modulation(cond, ada_w, ada_b)
    seq_rows = jnp.concatenate([jnp.zeros((b_ctx,), jnp.int32), 1 + jnp.arange(b_lat, dtype=jnp.int32)])
    mods = mod_small[:, seq_rows][:, :, None, :]

    x = jnp.concatenate([x_prompt.reshape(n_ctx, d), x_sample.reshape(n_lat, d)], axis=0)
    groups = ((0, n_ctx), (n_ctx, n_lat))
    rope = rope_tables(l_lat)
    ks, vs, hs, ss = [], [], [], []
    for l in range(depth):
        mod = mods[l]
        nw1 = norm_w[l, 0].reshape(1, d)
        nw2 = norm_w[l, 1].reshape(1, d)
        if l % 2 == 0:
            e = l // 2
            proj = norm_mod_matmul(x, nw1, mod, rowmap, 1, 0, w_in_ab[e].astype(BF16))
            width = proj.shape[1]
            proj_c = proj[:n_ctx].reshape(b_ctx, l_ctx, width)
            proj_l = proj[n_ctx:].reshape(b_lat, l_lat, width)
            zero_state = jnp.zeros((b_ctx, 2, A_HEADS, A_DK, A_DK), F32)
            of_c, ob_c, st_c = hgrn2_mixer(proj_c, hgrn_lb_logits, zero_state, e)
            of_l, ob_l, _ = hgrn2_mixer(proj_l, hgrn_lb_logits, jnp.swapaxes(state_hgrn[:, e], -1, -2), e)
            hs.append(jnp.swapaxes(st_c, -1, -2))
            qn_c, kn_c = qk_prepare(proj_c, q_norm_w[e], k_norm_w[e], None)
            att_c = context_attention(qn_c, kn_c, proj_c, attn_sink[e])
            ks.append(kn_c.reshape(b_ctx, l_ctx, B_KV_HEADS, HEAD_DIM).transpose(0, 2, 1, 3))
            v_off = 5 * A_WIDTH + B_WIDTH + KV_WIDTH
            vs.append(proj_c[:, :, v_off:].reshape(b_ctx, l_ctx, B_KV_HEADS, HEAD_DIM).transpose(0, 2, 1, 3))
            qr_l, kr_l = qk_prepare(proj_l, q_norm_w[e], k_norm_w[e], rope)
            att_l = latent_attention(qr_l, kr_l, proj_l, cache_k[:, e], cache_v[:, e], attn_sink[e])
            o_f = jnp.concatenate([of_c.reshape(n_ctx, A_WIDTH), of_l.reshape(n_lat, A_WIDTH)], axis=0)
            o_b = jnp.concatenate([ob_c.reshape(n_ctx, A_WIDTH), ob_l.reshape(n_lat, A_WIDTH)], axis=0)
            o_att = jnp.concatenate([att_c.reshape(n_ctx, B_WIDTH), att_l.reshape(n_lat, B_WIDTH)], axis=0)
            x = even_out_proj(o_f, o_b, proj, o_att, x, mod, rowmap, hgrn_norm_w[e].reshape(1, A_DK),
                              w_out_ab[e].astype(BF16))
        else:
            o = l // 2
            h = norm_mod(x, nw1, mod, rowmap, 1, 0)
            ops = s5_operators(s5_a_re[o], s5_a_im[o], s5_log_dt[o], s5_b_re[o], s5_b_im[o], s5_c_re[o], s5_c_im[o])
            g = d // S5_GROUP
            zero_s5 = jnp.zeros((b_ctx, 2, 2, g, S5_STATE), F32)
            y_c, fin_c = s5_mixer(h[:n_ctx].reshape(b_ctx, l_ctx, d), ops, s5_d[o], zero_s5)
            y_l, _ = s5_mixer(h[n_ctx:].reshape(b_lat, l_lat, d), ops, s5_d[o], state_s5[:, o])
            ss.append(fin_c)
            y = jnp.concatenate([y_c.reshape(n_ctx, d), y_l.reshape(n_lat, d)], axis=0)
            x = glu_residual(y, x, mod, rowmap, glu_w_a[o].astype(BF16), glu_w_b[o].astype(BF16))
        x = moe_layer(x, nw2, mod, rowmap, groups, router_w[l], exp_w_gate[l], exp_w_up[l], exp_w_down[l])
    y_prompt = x[:n_ctx].reshape(b_ctx, l_ctx, d)
    y_sample = x[n_ctx:].reshape(b_lat, l_lat, d)
    return (y_prompt, y_sample, jnp.stack(ks, axis=1), jnp.stack(vs, axis=1),
            jnp.stack(hs, axis=1), jnp.stack(ss, axis=1))
```

```python
import functools
import math

import jax
import jax.numpy as jnp
from jax import lax
from jax.experimental import pallas as pl
from jax.experimental.pallas import tpu as pltpu

F32 = jnp.float32
BF16 = jnp.bfloat16

A_HEADS = 4
A_DK = 128
A_WIDTH = A_HEADS * A_DK
B_HEADS = 8
B_KV_HEADS = 2
HEAD_DIM = 64
B_GROUP = B_HEADS // B_KV_HEADS
B_WIDTH = B_HEADS * HEAD_DIM
KV_WIDTH = B_KV_HEADS * HEAD_DIM
WINDOW = 128
GRID_W = 64
ROPE_THETA = 10000.0
S5_GROUP = 16
S5_STATE = 64
S5_CHUNK = 16
N_EXPERTS = 16
EC_FACTOR = 2
EPS = 1e-6
NEG_INF = -1e30

HGRN_C = 64
HGRN_SB = 16
TOK_TILE = 256
LANES = 128
VMEM_LIMIT = 56 * 1024 * 1024


def _cparams(sem):
    return pltpu.CompilerParams(dimension_semantics=sem, vmem_limit_bytes=VMEM_LIMIT)


def _sigmoid(x):
    return 1.0 / (1.0 + jnp.exp(-x))


def _silu(x):
    return x * _sigmoid(x)


def _norm_mod(x, nw, sc, sh):
    ms = jnp.mean(x * x, axis=-1, keepdims=True)
    return (x * lax.rsqrt(ms + EPS) * nw) * (1.0 + sc) + sh


def _dot(a, b):
    return jnp.dot(a, b, preferred_element_type=F32)


def _dot_nt(a, b):
    return lax.dot_general(a, b, (((1,), (1,)), ((), ())), preferred_element_type=F32)


def _split3(x):
    hi = x.astype(BF16)
    r1 = x - hi.astype(F32)
    mid = r1.astype(BF16)
    lo = (r1 - mid.astype(F32)).astype(BF16)
    return hi, mid, lo


def _ada_kernel(c_ref, w_ref, b_ref, o_ref):
    s = _silu(c_ref[...])
    o_ref[0] = _dot(s.astype(BF16), w_ref[0].astype(BF16)) + b_ref[0]


def ada_modulation(cond, ada_w, ada_b):
    depth, d, n = ada_w.shape
    rows = cond.shape[0]
    tn = 1536
    return pl.pallas_call(
        _ada_kernel,
        grid=(depth, n // tn),
        in_specs=[pl.BlockSpec((rows, d), lambda l, j: (0, 0)),
                  pl.BlockSpec((1, d, tn), lambda l, j: (l, 0, j)),
                  pl.BlockSpec((1, 1, tn), lambda l, j: (l, 0, j))],
        out_specs=pl.BlockSpec((1, rows, tn), lambda l, j: (l, 0, j)),
        out_shape=jax.ShapeDtypeStruct((depth, rows, n), F32),
        compiler_params=_cparams(("arbitrary", "arbitrary")),
        name="ada_modulation",
    )(cond, ada_w, ada_b.reshape(depth, 1, n))


def _mod_spec(rowmap, k, d):
    return pl.BlockSpec((1, 1, d), lambda i: (rowmap(i), 0, k))


def _inproj_kernel(x_ref, nw_ref, sc_ref, sh_ref, w_ref, o_ref):
    h = _norm_mod(x_ref[...], nw_ref[...], sc_ref[0], sh_ref[0])
    o_ref[...] = _dot(h.astype(BF16), w_ref[...])


def norm_mod_matmul(x, nw, mod, rowmap, k_sc, k_sh, w_bf16):
    n_tok, d = x.shape
    n = w_bf16.shape[1]
    tm = TOK_TILE
    return pl.pallas_call(
        _inproj_kernel,
        grid=(n_tok // tm,),
        in_specs=[pl.BlockSpec((tm, d), lambda i: (i, 0)),
                  pl.BlockSpec((1, d), lambda i: (0, 0)),
                  _mod_spec(rowmap, k_sc, d), _mod_spec(rowmap, k_sh, d),
                  pl.BlockSpec((d, n), lambda i: (0, 0))],
        out_specs=pl.BlockSpec((tm, n), lambda i: (i, 0)),
        out_shape=jax.ShapeDtypeStruct((n_tok, n), F32),
        compiler_params=_cparams(("arbitrary",)),
        name="norm_mod_matmul",
    )(x, nw, mod, mod, w_bf16)


def _norm_mod_kernel(x_ref, nw_ref, sc_ref, sh_ref, o_ref):
    o_ref[...] = _norm_mod(x_ref[...], nw_ref[...], sc_ref[0], sh_ref[0])


def norm_mod(x, nw, mod, rowmap, k_sc, k_sh):
    n_tok, d = x.shape
    tm = TOK_TILE
    return pl.pallas_call(
        _norm_mod_kernel,
        grid=(n_tok // tm,),
        in_specs=[pl.BlockSpec((tm, d), lambda i: (i, 0)),
                  pl.BlockSpec((1, d), lambda i: (0, 0)),
                  _mod_spec(rowmap, k_sc, d), _mod_spec(rowmap, k_sh, d)],
        out_specs=pl.BlockSpec((tm, d), lambda i: (i, 0)),
        out_shape=jax.ShapeDtypeStruct((n_tok, d), F32),
        compiler_params=_cparams(("arbitrary",)),
        name="norm_mod",
    )(x, nw, mod, mod)


def _outproj_kernel(of_ref, ob_ref, ga_ref, oatt_ref, x_ref, g1_ref, hw_ref, w_ref, o_ref):
    o = of_ref[...] + ob_ref[...]
    gate = _silu(ga_ref[...])
    hw = hw_ref[...]
    parts = []
    for h in range(A_HEADS):
        sl = slice(h * A_DK, (h + 1) * A_DK)
        oh = o[:, sl]
        ms = jnp.mean(oh * oh, axis=-1, keepdims=True)
        parts.append(((oh * lax.rsqrt(ms + EPS) * hw) * gate[:, sl]).astype(BF16))
    parts.append(oatt_ref[...].astype(BF16))
    lhs = jnp.concatenate(parts, axis=1)
    y = _dot(lhs, w_ref[...])
    o_ref[...] = x_ref[...] + g1_ref[0] * y


def even_out_proj(o_f, o_b, proj, o_att, x, mod, rowmap, hw, w_bf16):
    n_tok, d = x.shape
    tm = TOK_TILE
    aw = A_WIDTH
    return pl.pallas_call(
        _outproj_kernel,
        grid=(n_tok // tm,),
        in_specs=[pl.BlockSpec((tm, aw), lambda i: (i, 0)),
                  pl.BlockSpec((tm, aw), lambda i: (i, 0)),
                  pl.BlockSpec((tm, aw), lambda i: (i, 4)),
                  pl.BlockSpec((tm, B_WIDTH), lambda i: (i, 0)),
                  pl.BlockSpec((tm, d), lambda i: (i, 0)),
                  _mod_spec(rowmap, 2, d),
                  pl.BlockSpec((1, A_DK), lambda i: (0, 0)),
                  pl.BlockSpec((aw + B_WIDTH, d), lambda i: (0, 0))],
        out_specs=pl.BlockSpec((tm, d), lambda i: (i, 0)),
        out_shape=jax.ShapeDtypeStruct((n_tok, d), F32),
        compiler_params=_cparams(("arbitrary",)),
        name="even_out_proj",
    )(o_f, o_b, proj, o_att, x, mod, hw, w_bf16)


def _glu_kernel(y_ref, x_ref, g1_ref, wa_ref, wb_ref, o_ref):
    y = jax.nn.gelu(y_ref[...], approximate=True).astype(BF16)
    a = _dot(y, wa_ref[...])
    b = _dot(y, wb_ref[...])
    o_ref[...] = x_ref[...] + g1_ref[0] * (a * _sigmoid(b))


def glu_residual(y, x, mod, rowmap, wa_bf16, wb_bf16):
    n_tok, d = x.shape
    tm = TOK_TILE
    return pl.pallas_call(
        _glu_kernel,
        grid=(n_tok // tm,),
        in_specs=[pl.BlockSpec((tm, d), lambda i: (i, 0)),
                  pl.BlockSpec((tm, d), lambda i: (i, 0)),
                  _mod_spec(rowmap, 2, d),
                  pl.BlockSpec((d, d), lambda i: (0, 0)),
                  pl.BlockSpec((d, d), lambda i: (0, 0))],
        out_specs=pl.BlockSpec((tm, d), lambda i: (i, 0)),
        out_shape=jax.ShapeDtypeStruct((n_tok, d), F32),
        compiler_params=_cparams(("arbitrary",)),
        name="glu_residual",
    )(y, x, mod, wa_bf16, wb_bf16)


def _hgrn_chunk(q, k, v, g, st, msel, rev):
    c = q.shape[0]
    nb = c // HGRN_SB
    row = lax.broadcasted_iota(jnp.int32, (c, c), 0)
    col = lax.broadcasted_iota(jnp.int32, (c, c), 1)
    tri = jnp.where((col >= row) if rev else (col <= row), 1.0, 0.0).astype(BF16)
    gh, gm, gl = _split3(g)
    b = _dot(tri, gh) + _dot(tri, gm) + _dot(tri, gl)
    b_edge = b[0:1] if rev else b[c - 1:c]
    qs = q * jnp.exp(b)
    kdec = k * jnp.exp(b_edge - b)

    slabs = []
    for i in range(nb):
        sl = slice(i * HGRN_SB, (i + 1) * HGRN_SB)
        bi, qi, ki = b[sl], q[sl], k[sl]
        pieces = []
        for s in range(HGRN_SB):
            e = jnp.exp(jnp.minimum(bi - bi[s:s + 1], 0.0))
            pieces.append(((qi * e) * ki[s:s + 1]).astype(BF16))
        slabs.append(jnp.concatenate(pieces, axis=1))
    a_loc = _dot(jnp.concatenate(slabs, axis=0), msel)

    lane = lax.broadcasted_iota(jnp.int32, (HGRN_SB, LANES), 1)
    rloc = lax.broadcasted_iota(jnp.int32, (HGRN_SB, LANES), 0)
    dmask = ((lane >= rloc) & (lane < HGRN_SB)) if rev else (lane <= rloc)
    krow = lax.broadcasted_iota(jnp.int32, (c, A_DK), 0)
    att_rows = []
    for i in range(nb):
        sl = slice(i * HGRN_SB, (i + 1) * HGRN_SB)
        a_d = jnp.where(dmask, a_loc[sl], 0.0)
        if i > 0:
            a_d = pltpu.roll(a_d, i * HGRN_SB, axis=1)
        a_i = a_d[:, :c]
        has_off = (i < nb - 1) if rev else (i > 0)
        if has_off:
            edge = (i + 1) * HGRN_SB if rev else i * HGRN_SB
            r = b[edge:edge + 1] if rev else b[edge - 1:edge]
            qp = q[sl] * jnp.exp(b[sl] - r)
            live = (krow >= edge) if rev else (krow < edge)
            kp = jnp.where(live, k * jnp.exp(jnp.minimum(r - b, 0.0)), 0.0)
            a_i = a_i + _dot_nt(qp.astype(BF16), kp.astype(BF16))
        att_rows.append(a_i)
    att = jnp.concatenate(att_rows, axis=0)

    vb = v.astype(BF16)
    o = _dot(att.astype(BF16), vb) + _dot_nt(qs.astype(BF16), st.astype(BF16))
    st_new = st * jnp.exp(b_edge) + _dot(v.T.astype(BF16), kdec.astype(BF16))
    return o, st_new


def _hgrn_kernel(qf_ref, vf_ref, ff_ref, qb_ref, vb_ref, fb_ref, lbl_ref, msel_ref, s0_ref,
                 of_ref, ob_ref, sout_ref, st_ref, *, layer):
    c_idx = pl.program_id(1)

    @pl.when(c_idx == 0)
    def _():
        st_ref[...] = s0_ref[0]

    lg = lbl_ref[...]
    ex = jnp.exp(lg - jnp.max(lg, axis=0, keepdims=True))
    pr = ex / jnp.sum(ex, axis=0, keepdims=True)
    lb = jnp.zeros_like(pr[0])
    for e in range(1, layer + 1):
        lb = lb + pr[e]
    msel = msel_ref[...]

    for d, (q_ref, v_ref, f_ref, o_ref) in enumerate(((qf_ref, vf_ref, ff_ref, of_ref),
                                                       (qb_ref, vb_ref, fb_ref, ob_ref))):
        qa = q_ref[0]
        q_all = _silu(qa)
        v_all = v_ref[0]
        lbd = lb[d:d + 1]
        forget = lbd + (1.0 - lbd) * _sigmoid(f_ref[0])
        k_all = 1.0 - forget
        g_all = jnp.log(forget)
        for h in range(A_HEADS):
            sl = slice(h * A_DK, (h + 1) * A_DK)
            o, st_new = _hgrn_chunk(q_all[:, sl], k_all[:, sl], v_all[:, sl], g_all[:, sl],
                                    st_ref[d, h], msel, rev=(d == 1))
            o_ref[0, :, sl] = o
            st_ref[d, h] = st_new

    @pl.when(c_idx == pl.num_programs(1) - 1)
    def _():
        sout_ref[0] = st_ref[...]


def hgrn2_mixer(proj, lb_logits, s0t, layer):
    bsz, seq, _ = proj.shape
    c = HGRN_C
    nc = seq // c
    aw = A_WIDTH
    msel = jnp.repeat(jnp.eye(HGRN_SB, LANES, dtype=BF16), A_DK, axis=0)

    def fwd(col):
        return pl.BlockSpec((1, c, aw), lambda b, i: (b, i, col))

    def bwd(col):
        return pl.BlockSpec((1, c, aw), lambda b, i: (b, nc - 1 - i, col))

    st_spec = pl.BlockSpec((1, 2, A_HEADS, A_DK, A_DK), lambda b, i: (b, 0, 0, 0, 0))
    return pl.pallas_call(
        functools.partial(_hgrn_kernel, layer=layer),
        grid=(bsz, nc),
        in_specs=[fwd(0), fwd(3), fwd(1), bwd(0), bwd(3), bwd(2),
                  pl.BlockSpec(lb_logits.shape, lambda b, i: (0, 0, 0)),
                  pl.BlockSpec(msel.shape, lambda b, i: (0, 0)),
                  st_spec],
        out_specs=[pl.BlockSpec((1, c, aw), lambda b, i: (b, i, 0)),
                   pl.BlockSpec((1, c, aw), lambda b, i: (b, nc - 1 - i, 0)),
                   st_spec],
        out_shape=[jax.ShapeDtypeStruct((bsz, seq, aw), F32),
                   jax.ShapeDtypeStruct((bsz, seq, aw), F32),
                   jax.ShapeDtypeStruct((bsz, 2, A_HEADS, A_DK, A_DK), F32)],
        scratch_shapes=[pltpu.VMEM((2, A_HEADS, A_DK, A_DK), F32)],
        compiler_params=_cparams(("arbitrary", "arbitrary")),
        name="hgrn2_mixer",
    )(proj, proj, proj, proj, proj, proj, lb_logits, msel, s0t)


def _head_norm(x, w, gmat):
    hi, mid, lo = _split3(x * x)
    ms = _dot(hi, gmat) + _dot(mid, gmat) + _dot(lo, gmat)
    return x * lax.rsqrt(ms + EPS) * w


def _rope(x, cos, sin_signed):
    width = x.shape[1]
    lane = lax.broadcasted_iota(jnp.int32, x.shape, 1)
    nxt = pltpu.roll(x, width - 1, axis=1)
    prv = pltpu.roll(x, 1, axis=1)
    partner = jnp.where(lane % 2 == 0, nxt, prv)
    return x * cos + partner * sin_signed


def _qkprep_kernel(*refs, rope):
    if rope:
        q_ref, k_ref, qw_ref, kw_ref, gm_ref, cos_ref, sin_ref, qo_ref, ko_ref = refs
    else:
        q_ref, k_ref, qw_ref, kw_ref, gm_ref, qo_ref, ko_ref = refs
    gm = gm_ref[...]
    qn = _head_norm(q_ref[0], qw_ref[...], gm)
    kn = _head_norm(k_ref[0], kw_ref[...], gm[:KV_WIDTH, :KV_WIDTH])
    if rope:
        cos = cos_ref[...]
        sin = sin_ref[...]
        qn = _rope(qn, cos, sin)
        kn = _rope(kn, cos[:, :KV_WIDTH], sin[:, :KV_WIDTH])
    qo_ref[0] = (qn * (HEAD_DIM ** -0.5)).astype(BF16)
    ko_ref[0] = kn


def qk_prepare(proj, qw, kw, rope_tabs):
    bsz, seq, _ = proj.shape
    tm = 256
    qcol = (5 * A_WIDTH) // B_WIDTH
    kcol = (5 * A_WIDTH + B_WIDTH) // KV_WIDTH
    gidx = jnp.arange(B_WIDTH) // HEAD_DIM
    gmat = jnp.where(gidx[:, None] == gidx[None, :], 1.0 / HEAD_DIM, 0.0).astype(BF16)
    qw_t = jnp.tile(qw, B_HEADS).reshape(1, B_WIDTH)
    kw_t = jnp.tile(kw, B_KV_HEADS).reshape(1, KV_WIDTH)
    in_specs = [pl.BlockSpec((1, tm, B_WIDTH), lambda b, i: (b, i, qcol)),
                pl.BlockSpec((1, tm, KV_WIDTH), lambda b, i: (b, i, kcol)),
                pl.BlockSpec((1, B_WIDTH), lambda b, i: (0, 0)),
                pl.BlockSpec((1, KV_WIDTH), lambda b, i: (0, 0)),
                pl.BlockSpec((B_WIDTH, B_WIDTH), lambda b, i: (0, 0))]
    args = [proj, proj, qw_t, kw_t, gmat]
    if rope_tabs is not None:
        in_specs += [pl.BlockSpec((tm, B_WIDTH), lambda b, i: (i, 0)),
                     pl.BlockSpec((tm, B_WIDTH), lambda b, i: (i, 0))]
        args += list(rope_tabs)
    return pl.pallas_call(
        functools.partial(_qkprep_kernel, rope=rope_tabs is not None),
        grid=(bsz, seq // tm),
        in_specs=in_specs,
        out_specs=[pl.BlockSpec((1, tm, B_WIDTH), lambda b, i: (b, i, 0)),
                   pl.BlockSpec((1, tm, KV_WIDTH), lambda b, i: (b, i, 0))],
        out_shape=[jax.ShapeDtypeStruct((bsz, seq, B_WIDTH), BF16),
                   jax.ShapeDtypeStruct((bsz, seq, KV_WIDTH), F32)],
        compiler_params=_cparams(("arbitrary", "arbitrary")),
        name="qk_prepare",
    )(*args)


def rope_tables(seq):
    pos = jnp.arange(seq)
    row = (pos // GRID_W).astype(F32)
    col = (pos % GRID_W).astype(F32)
    n_pair = HEAD_DIM // 4
    freqs = ROPE_THETA ** (-jnp.arange(n_pair, dtype=F32) / n_pair)
    ang = jnp.concatenate([row[:, None] * freqs, col[:, None] * freqs], axis=-1)
    cos = jnp.repeat(jnp.cos(ang), 2, axis=-1)
    sin = jnp.repeat(jnp.sin(ang), 2, axis=-1) * jnp.tile(jnp.array([-1.0, 1.0], F32), HEAD_DIM // 2)
    return jnp.tile(cos, (1, B_HEADS)), jnp.tile(sin, (1, B_HEADS))


def _softmax_av(s, sink, v):
    m = jnp.maximum(jnp.max(s, axis=1, keepdims=True), sink)
    p = jnp.exp(s - m)
    den = jnp.sum(p, axis=1, keepdims=True) + jnp.exp(sink - m)
    return _dot(p.astype(BF16), v) / den


def _ctx_attn_kernel(q_ref, k_ref, v_ref, sink_ref, o_ref):
    q = q_ref[0]
    k = k_ref[0].astype(BF16)
    v = v_ref[0].astype(BF16)
    for kvh in range(B_KV_HEADS):
        ks = slice(kvh * HEAD_DIM, (kvh + 1) * HEAD_DIM)
        for gq in range(B_GROUP):
            h = kvh * B_GROUP + gq
            hs = slice(h * HEAD_DIM, (h + 1) * HEAD_DIM)
            s = _dot_nt(q[:, hs], k[:, ks])
            o_ref[0, :, hs] = _softmax_av(s, sink_ref[h:h + 1, 0:1], v[:, ks])


def context_attention(qn, kn, proj, sink):
    bsz, seq, _ = qn.shape
    vcol = (5 * A_WIDTH + B_WIDTH + KV_WIDTH) // KV_WIDTH
    sink_t = jnp.broadcast_to(sink.reshape(B_HEADS, 1), (B_HEADS, LANES))
    return pl.pallas_call(
        _ctx_attn_kernel,
        grid=(bsz,),
        in_specs=[pl.BlockSpec((1, seq, B_WIDTH), lambda b: (b, 0, 0)),
                  pl.BlockSpec((1, seq, KV_WIDTH), lambda b: (b, 0, 0)),
                  pl.BlockSpec((1, seq, KV_WIDTH), lambda b: (b, 0, vcol)),
                  pl.BlockSpec((B_HEADS, LANES), lambda b: (0, 0))],
        out_specs=pl.BlockSpec((1, seq, B_WIDTH), lambda b: (b, 0, 0)),
        out_shape=jax.ShapeDtypeStruct((bsz, seq, B_WIDTH), F32),
        compiler_params=_cparams(("arbitrary",)),
        name="context_attention",
    )(qn, kn, proj, sink_t)


def _lat_attn_kernel(q_ref, kp_ref, kc_ref, kn_ref, vp_ref, vc_ref, vn_ref, kx_ref, vx_ref, sink_ref, o_ref):
    blk = pl.program_id(1)
    nblk = pl.num_programs(1)
    tq = q_ref.shape[1]
    q = q_ref[0]
    kl = jnp.concatenate([kp_ref[0], kc_ref[0], kn_ref[0]], axis=0).astype(BF16)
    vl = jnp.concatenate([vp_ref[0], vc_ref[0], vn_ref[0]], axis=0).astype(BF16)
    n_ctx = kx_ref.shape[2]
    span = 3 * tq
    i = lax.broadcasted_iota(jnp.int32, (tq, span + n_ctx), 0)
    j = lax.broadcasted_iota(jnp.int32, (tq, span + n_ctx), 1)
    dist = j - tq - i
    valid = (dist >= -WINDOW) & (dist <= WINDOW)
    valid = valid & ((j >= tq) | (blk > 0)) & ((j < 2 * tq) | (blk < nblk - 1))
    valid = valid | (j >= span)
    for kvh in range(B_KV_HEADS):
        ks = slice(kvh * HEAD_DIM, (kvh + 1) * HEAD_DIM)
        kk = jnp.concatenate([kl[:, ks], kx_ref[0, kvh].astype(BF16)], axis=0)
        vv = jnp.concatenate([vl[:, ks], vx_ref[0, kvh].astype(BF16)], axis=0)
        for gq in range(B_GROUP):
            h = kvh * B_GROUP + gq
            hs = slice(h * HEAD_DIM, (h + 1) * HEAD_DIM)
            s = jnp.where(valid, _dot_nt(q[:, hs], kk), NEG_INF)
            o_ref[0, :, hs] = _softmax_av(s, sink_ref[h:h + 1, 0:1], vv)


def latent_attention(qr, kr, proj, k_ctx, v_ctx, sink):
    bsz, seq, _ = qr.shape
    tq = WINDOW
    nblk = seq // tq
    vcol = (5 * A_WIDTH + B_WIDTH + KV_WIDTH) // KV_WIDTH
    n_ctx = k_ctx.shape[2]
    sink_t = jnp.broadcast_to(sink.reshape(B_HEADS, 1), (B_HEADS, LANES))

    def kv_specs(col):
        return [pl.BlockSpec((1, tq, KV_WIDTH), lambda b, i: (b, jnp.maximum(i - 1, 0), col)),
                pl.BlockSpec((1, tq, KV_WIDTH), lambda b, i: (b, i, col)),
                pl.BlockSpec((1, tq, KV_WIDTH), lambda b, i: (b, jnp.minimum(i + 1, nblk - 1), col))]

    ctx_spec = pl.BlockSpec((1, B_KV_HEADS, n_ctx, HEAD_DIM), lambda b, i: (b, 0, 0, 0))
    return pl.pallas_call(
        _lat_attn_kernel,
        grid=(bsz, nblk),
        in_specs=[pl.BlockSpec((1, tq, B_WIDTH), lambda b, i: (b, i, 0))] + kv_specs(0) + kv_specs(vcol)
                 + [ctx_spec, ctx_spec, pl.BlockSpec((B_HEADS, LANES), lambda b, i: (0, 0))],
        out_specs=pl.BlockSpec((1, tq, B_WIDTH), lambda b, i: (b, i, 0)),
        out_shape=jax.ShapeDtypeStruct((bsz, seq, B_WIDTH), F32),
        compiler_params=_cparams(("arbitrary", "arbitrary")),
        name="latent_attention",
    )(qr, kr, kr, kr, proj, proj, proj, k_ctx, v_ctx, sink_t)


def s5_operators(a_re, a_im, log_dt, b_re, b_im, c_re, c_im):
    t = S5_CHUNK
    hi = lax.Precision.HIGHEST
    ks, ws, vs, a1s, a2s = [], [], [], [], []
    for d in range(2):
        are, aim = a_re[d].astype(F32), a_im[d].astype(F32)
        dt = jnp.exp(log_dt[d].astype(F32))[:, None]
        den = are * are + aim * aim
        steps = jnp.arange(t + 1, dtype=F32)[:, None, None]
        mag = jnp.exp(steps * (dt * are))
        pw_re = mag * jnp.cos(steps * (dt * aim))
        pw_im = mag * jnp.sin(steps * (dt * aim))
        ab_re, ab_im = pw_re[1], pw_im[1]
        f_re = ((ab_re - 1.0) * are + ab_im * aim) / den
        f_im = (ab_im * are - (ab_re - 1.0) * aim) / den
        bre, bim = b_re[d].astype(F32), b_im[d].astype(F32)
        bb_re = f_re[..., None] * bre - f_im[..., None] * bim
        bb_im = f_re[..., None] * bim + f_im[..., None] * bre
        cre, cim = c_re[d].astype(F32), c_im[d].astype(F32)
        cp_re = cre[:, None] * pw_re.transpose(1, 0, 2)[:, :, None, :] - cim[:, None] * pw_im.transpose(1, 0, 2)[:, :, None, :]
        cp_im = cre[:, None] * pw_im.transpose(1, 0, 2)[:, :, None, :] + cim[:, None] * pw_re.transpose(1, 0, 2)[:, :, None, :]
        m = (jnp.einsum('gkcp,gpd->gkcd', cp_re[:, :t], bb_re, precision=hi)
             - jnp.einsum('gkcp,gpd->gkcd', cp_im[:, :t], bb_im, precision=hi))
        s_i = jnp.arange(t)[:, None]
        t_i = jnp.arange(t)[None, :]
        lag = (t_i - s_i) if d == 0 else (s_i - t_i)
        blk = jnp.where((lag >= 0)[None, :, :, None, None], m[:, jnp.clip(lag, 0, t - 1)], 0.0)
        ks.append(blk.transpose(0, 1, 4, 2, 3).reshape(-1, t * S5_GROUP, t * S5_GROUP))
        pidx = (t - 1 - jnp.arange(t)) if d == 0 else jnp.arange(t)
        pr = pw_re[pidx].transpose(1, 0, 2)[:, :, None, :]
        pi = pw_im[pidx].transpose(1, 0, 2)[:, :, None, :]
        bbr = bb_re.transpose(0, 2, 1)[:, None]
        bbi = bb_im.transpose(0, 2, 1)[:, None]
        w_re = pr * bbr - pi * bbi
        w_im = pr * bbi + pi * bbr
        ws.append(jnp.concatenate([w_re, w_im], axis=-1).reshape(-1, t * S5_GROUP, 2 * S5_STATE))
        kidx = (jnp.arange(t) + 1) if d == 0 else (t - jnp.arange(t))
        e_re = cp_re[:, kidx]
        e_im = cp_im[:, kidx]
        v = jnp.concatenate([e_re, -e_im], axis=-1)
        vs.append(v.transpose(0, 3, 1, 2).reshape(-1, 2 * S5_STATE, t * S5_GROUP))
        a1s.append(jnp.concatenate([pw_re[t], pw_re[t]], axis=-1))
        a2s.append(jnp.concatenate([-pw_im[t], pw_im[t]], axis=-1))
    return (ks[0] + ks[1], jnp.concatenate(ws, axis=-1), jnp.concatenate(vs, axis=1),
            jnp.concatenate(a1s, axis=-1), jnp.concatenate(a2s, axis=-1))


def _s5_state_kernel(u_ref, w_ref, o_ref):
    gb = u_ref.shape[0]
    n = w_ref.shape[2]
    for g in range(gb):
        o_ref[:, g * n:(g + 1) * n] = _dot(u_ref[g].astype(BF16), w_ref[g])


def s5_chunk_states(u, w_bf16):
    g, r, tc = u.shape
    n = w_bf16.shape[2]
    gb = 8
    tr = min(r, 512)
    return pl.pallas_call(
        _s5_state_kernel,
        grid=(g // gb, r // tr),
        in_specs=[pl.BlockSpec((gb, tr, tc), lambda i, j: (i, j, 0)),
                  pl.BlockSpec((gb, tc, n), lambda i, j: (i, 0, 0))],
        out_specs=pl.BlockSpec((tr, gb * n), lambda i, j: (j, i)),
        out_shape=jax.ShapeDtypeStruct((r, g * n), F32),
        compiler_params=_cparams(("arbitrary", "arbitrary")),
        name="s5_chunk_states",
    )(u, w_bf16)


def _s5_scan_kernel(d_ref, a1_ref, a2_ref, s0_ref, st_ref, fin_ref):
    nj = d_ref.shape[0]
    half = d_ref.shape[2] // 2
    a1 = a1_ref[...]
    a2 = a2_ref[...]
    a1f, a1b = a1[:, :half], a1[:, half:]
    a2f, a2b = a2[:, :half], a2[:, half:]
    s0 = s0_ref[...]

    def step(i, carry):
        sf, sb = carry
        jb = nj - 1 - i
        st_ref[i, :, :half] = sf
        st_ref[jb, :, half:] = sb
        sf = a1f * sf + a2f * pltpu.roll(sf, half // 2, axis=1) + d_ref[i, :, :half]
        sb = a1b * sb + a2b * pltpu.roll(sb, half // 2, axis=1) + d_ref[jb, :, half:]
        return sf, sb

    sf, sb = lax.fori_loop(0, nj, step, (s0[:, :half], s0[:, half:]))
    fin_ref[:, :half] = sf
    fin_ref[:, half:] = sb


def s5_chunk_scan(dv, a1, a2, s0):
    nj, rows, n = dv.shape
    tr = 32
    return pl.pallas_call(
        _s5_scan_kernel,
        grid=(rows // tr,),
        in_specs=[pl.BlockSpec((nj, tr, n), lambda i: (0, i, 0)),
                  pl.BlockSpec((tr, n), lambda i: (i, 0)),
                  pl.BlockSpec((tr, n), lambda i: (i, 0)),
                  pl.BlockSpec((tr, n), lambda i: (i, 0))],
        out_specs=[pl.BlockSpec((nj, tr, n), lambda i: (0, i, 0)),
                   pl.BlockSpec((tr, n), lambda i: (i, 0))],
        out_shape=[jax.ShapeDtypeStruct((nj, rows, n), F32),
                   jax.ShapeDtypeStruct((rows, n), F32)],
        compiler_params=_cparams(("arbitrary",)),
        name="s5_chunk_scan",
    )(dv, a1, a2, s0)


def _s5_out_kernel(u_ref, s_ref, k_ref, v_ref, dsk_ref, o_ref):
    gb = u_ref.shape[0]
    n = v_ref.shape[1]
    for g in range(gb):
        u = u_ref[g]
        y = _dot(u.astype(BF16), k_ref[g]) + _dot(s_ref[:, g * n:(g + 1) * n].astype(BF16), v_ref[g])
        o_ref[g] = y + dsk_ref[g] * u


def s5_outputs(u, states, k_bf16, v_bf16, dskip):
    g, r, tc = u.shape
    n = v_bf16.shape[1]
    gb = 8
    tr = min(r, 512)
    return pl.pallas_call(
        _s5_out_kernel,
        grid=(g // gb, r // tr),
        in_specs=[pl.BlockSpec((gb, tr, tc), lambda i, j: (i, j, 0)),
                  pl.BlockSpec((tr, gb * n), lambda i, j: (j, i)),
                  pl.BlockSpec((gb, tc, tc), lambda i, j: (i, 0, 0)),
                  pl.BlockSpec((gb, n, tc), lambda i, j: (i, 0, 0)),
                  pl.BlockSpec((gb, 1, tc), lambda i, j: (i, 0, 0))],
        out_specs=pl.BlockSpec((gb, tr, tc), lambda i, j: (i, j, 0)),
        out_shape=jax.ShapeDtypeStruct((g, r, tc), F32),
        compiler_params=_cparams(("arbitrary", "arbitrary")),
        name="s5_outputs",
    )(u, states, k_bf16, v_bf16, dskip)


def s5_mixer(h, ops, dskip, s0):
    bsz, seq, d = h.shape
    g = d // S5_GROUP
    t = S5_CHUNK
    nj = seq // t
    k_tot, w_tot, v_tot, a1, a2 = ops
    u = h.reshape(bsz, nj, t, g, S5_GROUP).transpose(3, 1, 0, 2, 4).reshape(g, nj * bsz, t * S5_GROUP)
    dv = s5_chunk_states(u, w_tot.astype(BF16))
    n = 4 * S5_STATE
    s0r = s0.transpose(0, 3, 1, 2, 4).reshape(bsz * g, n)
    states, final = s5_chunk_scan(dv.reshape(nj, bsz * g, n), jnp.tile(a1, (bsz, 1)), jnp.tile(a2, (bsz, 1)), s0r)
    dsk = jnp.tile(dskip.reshape(g, 1, S5_GROUP), (1, 1, t))
    y = s5_outputs(u, states.reshape(nj * bsz, g * n), k_tot.astype(BF16), v_tot.astype(BF16), dsk)
    y = y.reshape(g, nj, bsz, t, S5_GROUP).transpose(2, 1, 3, 0, 4).reshape(bsz, seq, d)
    final = final.reshape(bsz, g, 2, 2, S5_STATE).transpose(0, 2, 3, 1, 4)
    return y, final


def _router_kernel(x_ref, nw_ref, sc_ref, sh_ref, wr_ref, h_ref, aff_ref):
    h = _norm_mod(x_ref[...], nw_ref[...], sc_ref[0], sh_ref[0])
    h_ref[...] = h
    logits = _dot_nt(wr_ref[...], h.astype(BF16))
    ex = jnp.exp(logits - jnp.max(logits, axis=0, keepdims=True))
    p = ex / jnp.sum(ex, axis=0, keepdims=True)
    for k in range(aff_ref.shape[0]):
        aff_ref[k] = p[:, k * LANES:(k + 1) * LANES]


def moe_router(x, nw, mod, rowmap, wr_t_bf16):
    n_tok, d = x.shape
    tm = TOK_TILE
    ne = wr_t_bf16.shape[0]
    return pl.pallas_call(
        _router_kernel,
        grid=(n_tok // tm,),
        in_specs=[pl.BlockSpec((tm, d), lambda i: (i, 0)),
                  pl.BlockSpec((1, d), lambda i: (0, 0)),
                  _mod_spec(rowmap, 4, d), _mod_spec(rowmap, 3, d),
                  pl.BlockSpec((ne, d), lambda i: (0, 0))],
        out_specs=[pl.BlockSpec((tm, d), lambda i: (i, 0)),
                   pl.BlockSpec((tm // LANES, ne, LANES), lambda i: (i, 0, 0))],
        out_shape=[jax.ShapeDtypeStruct((n_tok, d), F32),
                   jax.ShapeDtypeStruct((n_tok // LANES, ne, LANES), F32)],
        compiler_params=_cparams(("arbitrary",)),
        name="moe_router",
    )(x, nw, mod, mod, wr_t_bf16)


def _select_kernel(aff_ref, ut_ref, v8_ref, cidx_ref, meta_ref, inc_ref, *, cap, first, slot0):
    nt, ne, _ = aff_ref.shape
    bits = lax.bitcast_convert_type(aff_ref[...], jnp.int32)

    def count(mask):
        c = jnp.sum(jnp.where(mask, 1.0, 0.0), axis=0)
        return jnp.sum(c, axis=1, keepdims=True)

    def radix(k, thr):
        cand = thr | (jnp.int32(1) << (30 - k))
        return jnp.where(count(bits >= cand[None]) >= cap, cand, thr)

    thr = lax.fori_loop(0, 31, radix, jnp.zeros((ne, 1), jnp.int32))
    gt = bits > thr[None]
    eq = bits == thr[None]
    need = cap - count(gt)
    ut = ut_ref[...]

    def excl_rank(mask):
        m = jnp.where(mask, 1.0, 0.0)
        inc_ref[...] = _dot(m.reshape(nt * ne, LANES).astype(BF16), ut).reshape(nt, ne, LANES)

        def body(t, carry):
            inc = inc_ref[t]
            inc_ref[t] = inc + carry
            return carry + inc[:, LANES - 1:LANES]

        lax.fori_loop(0, nt, body, jnp.zeros((ne, 1), F32))
        return inc_ref[...] - m

    sel = gt | (eq & (excl_rank(eq) < need[None]))
    rank = excl_rank(sel)
    start = rank[:, :, 0:1]
    count = rank[:, :, LANES - 1:LANES] + jnp.where(sel[:, :, LANES - 1:LANES], 1.0, 0.0) - start
    lane3 = lax.broadcasted_iota(jnp.int32, (nt, ne, LANES), 2)
    meta_ref[...] = jnp.where(lane3 == 0, start + float(slot0), jnp.where(lane3 == 1, count, 0.0)).astype(jnp.int32)

    inc_ref[...] = jnp.where(sel, rank - start, -1.0)
    sub = lax.broadcasted_iota(jnp.int32, (LANES, LANES), 0).astype(F32)
    v8 = v8_ref[...]

    def tile_body(t, carry):
        rho = inc_ref[t]
        base = lax.convert_element_type(first + t * LANES, F32)
        for e in range(ne):
            onehot = jnp.where(sub == rho[e:e + 1, :], 1.0, 0.0).astype(BF16)
            packed = _dot_nt(v8, onehot)
            cidx_ref[t, e:e + 1, :] = (packed[0:1] + base).astype(jnp.int32)
        return carry

    lax.fori_loop(0, nt, tile_body, 0)


def moe_select(aff_t, cap, first, slot0):
    nt, ne, _ = aff_t.shape
    ut = jnp.triu(jnp.ones((LANES, LANES), BF16))
    v8 = jnp.zeros((8, LANES), BF16).at[0].set(jnp.arange(LANES).astype(BF16))
    blk = pl.BlockSpec((nt, ne, LANES), lambda i: (0, 0, 0))
    return pl.pallas_call(
        functools.partial(_select_kernel, cap=cap, first=first, slot0=slot0),
        grid=(1,),
        in_specs=[blk, pl.BlockSpec((LANES, LANES), lambda i: (0, 0)), pl.BlockSpec((8, LANES), lambda i: (0, 0))],
        out_specs=[blk, blk],
        out_shape=[jax.ShapeDtypeStruct((nt, ne, LANES), jnp.int32),
                   jax.ShapeDtypeStruct((nt, ne, LANES), jnp.int32)],
        scratch_shapes=[pltpu.VMEM((nt, ne, LANES), F32)],
        compiler_params=_cparams(("arbitrary",)),
        name="moe_select",
    )(aff_t, ut, v8)


def _lists_kernel(starts_ref, counts_ref, cidx_ref, idx_ref, *, rows):
    e = pl.program_id(0)
    nt = counts_ref.shape[1]
    def clear(u, carry):
        idx_ref[rows + u] = 0
        return carry

    lax.fori_loop(0, idx_ref.shape[0] - rows, clear, 0)

    def tile(t, carry):
        s = starts_ref[e, t]
        groups = (counts_ref[e, t] + 7) >> 3

        def group(j, c2):
            for u in range(8):
                idx_ref[s + 8 * j + u] = cidx_ref[t * LANES + 8 * j + u]
            return c2

        lax.fori_loop(0, groups, group, 0)
        return carry

    lax.fori_loop(0, nt, tile, 0)


def moe_build_lists(starts, counts, cidx_flat, rows):
    ne, nt = counts.shape
    rows_pad = -(-(rows + 8) // 1024) * 1024
    return pl.pallas_call(
        functools.partial(_lists_kernel, rows=rows),
        grid_spec=pltpu.PrefetchScalarGridSpec(
            num_scalar_prefetch=2,
            grid=(ne,),
            in_specs=[pl.BlockSpec((nt * LANES,), lambda e, s, c: (e,), memory_space=pltpu.SMEM)],
            out_specs=pl.BlockSpec((rows_pad,), lambda e, s, c: (e,), memory_space=pltpu.SMEM)),
        out_shape=jax.ShapeDtypeStruct((ne * rows_pad,), jnp.int32),
        compiler_params=_cparams(("arbitrary",)),
        name="moe_build_lists",
    )(starts, counts, cidx_flat)


def _expert_kernel(idx_ref, h_hbm, wr_ref, wg_ref, wu_ref, wd_ref, y_ref, xbuf, wgb, wub, wdb, sem, *, rows_pad):
    e = pl.program_id(0)
    ch = pl.program_id(1)
    nch = pl.num_programs(1)
    tr = xbuf.shape[1]
    step = e * nch + ch
    slot = step % 2

    @pl.when(ch == 0)
    def _():
        wgb[...] = wg_ref[0].astype(BF16)
        wub[...] = wu_ref[0].astype(BF16)
        wdb[...] = wd_ref[0].astype(BF16)

    def gather(e_i, ch_i, slot_i):
        base = e_i * rows_pad + ch_i * tr

        def group(j, carry):
            for u in range(8):
                r = 8 * j + u
                pltpu.make_async_copy(h_hbm.at[pl.ds(idx_ref[base + r], 1)], xbuf.at[slot_i, pl.ds(r, 1)],
                                      sem.at[slot_i]).start()
            return carry

        lax.fori_loop(0, tr // 8, group, 0)

    @pl.when(step == 0)
    def _():
        gather(e, ch, slot)

    @pl.when(step + 1 < pl.num_programs(0) * nch)
    def _():
        last = ch == nch - 1
        gather(jnp.where(last, e + 1, e), jnp.where(last, 0, ch + 1), 1 - slot)

    pltpu.make_async_copy(h_hbm.at[pl.ds(0, tr)], xbuf.at[slot], sem.at[slot]).wait()

    x = xbuf[slot].astype(BF16)
    logits = _dot(x, wr_ref[...])
    lane = lax.broadcasted_iota(jnp.int32, logits.shape, 1)
    ne = pl.num_programs(0)
    logits = jnp.where(lane < ne, logits, NEG_INF)
    ex = jnp.exp(logits - jnp.max(logits, axis=1, keepdims=True))
    gate = jnp.sum(jnp.where(lane == e, ex, 0.0), axis=1, keepdims=True) / jnp.sum(ex, axis=1, keepdims=True)
    hmid = (_silu(_dot(x, wgb[...])) * _dot(x, wub[...])).astype(BF16)
    y_ref[0] = _dot(hmid, wdb[...]) * gate


def moe_experts(idx, rows, h, wr_pad_bf16, w_gate, w_up, w_down):
    ne = w_gate.shape[0]
    rows_pad = idx.shape[0] // ne
    d = h.shape[1]
    f = w_gate.shape[2]
    tr = 512
    return pl.pallas_call(
        functools.partial(_expert_kernel, rows_pad=rows_pad),
        grid_spec=pltpu.PrefetchScalarGridSpec(
            num_scalar_prefetch=1,
            grid=(ne, rows // tr),
            in_specs=[pl.BlockSpec(memory_space=pl.ANY),
                      pl.BlockSpec((d, LANES), lambda e, c, idx: (0, 0)),
                      pl.BlockSpec((1, d, f), lambda e, c, idx: (e, 0, 0)),
                      pl.BlockSpec((1, d, f), lambda e, c, idx: (e, 0, 0)),
                      pl.BlockSpec((1, f, d), lambda e, c, idx: (e, 0, 0))],
            out_specs=pl.BlockSpec((1, tr, d), lambda e, c, idx: (e, c, 0)),
            scratch_shapes=[pltpu.VMEM((2, tr, d), F32),
                            pltpu.VMEM((d, f), BF16), pltpu.VMEM((d, f), BF16), pltpu.VMEM((f, d), BF16),
                            pltpu.SemaphoreType.DMA((2,))]),
        out_shape=jax.ShapeDtypeStruct((ne, rows, d), F32),
        compiler_params=_cparams(("arbitrary", "arbitrary")),
        name="moe_experts",
    )(idx, h, wr_pad_bf16, w_gate, w_up, w_down)


def _combine_kernel(starts_ref, idx_ref, y_hbm, x_ref, g2_ref, o_ref, acc, stage, sem, *, rows_pad):
    tb = pl.program_id(0)
    tm = o_ref.shape[0]
    ne = y_hbm.shape[0]
    acc[...] = jnp.zeros_like(acc)

    def chunk_copy(e, src_row, dst_row):
        return pltpu.make_async_copy(y_hbm.at[e, pl.ds(src_row, 8)], stage.at[pl.ds(dst_row, 8)], sem)

    spans = []
    off = jnp.int32(0)
    for e in range(ne):
        s0 = starts_ref[e, tb]
        s1 = starts_ref[e, tb + 1]
        a = (s0 >> 3) << 3
        nchunk = jnp.where(s1 > s0, (s1 - a + 7) >> 3, 0)

        def issue(c, carry, e=e, a=a, off=off):
            chunk_copy(e, pl.multiple_of(a + 8 * c, 8), pl.multiple_of(off + 8 * c, 8)).start()
            return carry

        lax.fori_loop(0, nchunk, issue, 0)
        spans.append((s0, s1, off - a))
        off = off + 8 * nchunk

    def drain(c, carry):
        chunk_copy(0, 0, 0).wait()
        return carry

    lax.fori_loop(0, off >> 3, drain, 0)

    unroll = 4
    for e in range(ne):
        s0, s1, shift = spans[e]

        def add_rows(j, carry, e=e, s0=s0, s1=s1, shift=shift):
            for u in range(unroll):
                r = s0 + unroll * j + u
                live = r < s1
                rr = jnp.where(live, r, s0)
                n = jnp.where(live, idx_ref[e * rows_pad + rr] - tb * tm, tm)
                acc[pl.ds(n, 1), :] = acc[pl.ds(n, 1), :] + stage[pl.ds(rr + shift, 1), :]
            return carry

        lax.fori_loop(0, (s1 - s0 + unroll - 1) >> 2, add_rows, 0)

    o_ref[...] = x_ref[...] + g2_ref[0] * acc[0:tm, :]


def moe_combine(starts, idx, y, x, mod, rowmap):
    n_tok, d = x.shape
    ne = y.shape[0]
    rows_pad = idx.shape[0] // ne
    tm = TOK_TILE
    stage_rows = ne * tm + ne * 16
    return pl.pallas_call(
        functools.partial(_combine_kernel, rows_pad=rows_pad),
        grid_spec=pltpu.PrefetchScalarGridSpec(
            num_scalar_prefetch=2,
            grid=(n_tok // tm,),
            in_specs=[pl.BlockSpec(memory_space=pl.ANY),
                      pl.BlockSpec((tm, d), lambda i, s, ix: (i, 0)),
                      pl.BlockSpec((1, 1, d), lambda i, s, ix: (rowmap(i), 0, 5))],
            out_specs=pl.BlockSpec((tm, d), lambda i, s, ix: (i, 0)),
            scratch_shapes=[pltpu.VMEM((tm + 8, d), F32), pltpu.VMEM((stage_rows, d), F32),
                            pltpu.SemaphoreType.DMA(())]),
        out_shape=jax.ShapeDtypeStruct((n_tok, d), F32),
        compiler_params=_cparams(("arbitrary",)),
        name="moe_combine",
    )(starts, idx, y, x, mod)


def moe_layer(x, nw, mod, rowmap, groups, wr, w_gate, w_up, w_down):
    n_tok, d = x.shape
    ne = wr.shape[1]
    h, aff_t = moe_router(x, nw, mod, rowmap, wr.T.astype(BF16))
    cidx_parts, meta_parts = [], []
    rows = 0
    for first, count in groups:
        cap = EC_FACTOR * count // ne
        cidx, meta = moe_select(aff_t[first // LANES:(first + count) // LANES], cap, first, rows)
        cidx_parts.append(cidx)
        meta_parts.append(meta[:, :, :2])
        rows += cap
    cidx = jnp.concatenate(cidx_parts, axis=0).transpose(1, 0, 2).reshape(-1)
    meta = jnp.concatenate(meta_parts, axis=0)
    starts = meta[:, :, 0].T
    counts = meta[:, :, 1].T
    idx = moe_build_lists(starts, counts, cidx, rows)
    per = TOK_TILE // LANES
    starts_blk = jnp.concatenate([starts[:, ::per], jnp.full((ne, 1), rows, jnp.int32)], axis=1)
    wr_pad = jnp.zeros((d, LANES), BF16).at[:, :ne].set(wr.astype(BF16))
    y = moe_experts(idx, rows, h, wr_pad, w_gate, w_up, w_down)
    return moe_combine(starts_blk, idx, y, x, mod, rowmap)


def kernel(x_prompt, x_sample, cache_k, cache_v, state_hgrn, state_s5, c, c_ctx, norm_w, ada_w, ada_b, w_in_ab, w_out_ab, hgrn_lb_logits, hgrn_norm_w, q_norm_w, k_norm_w, attn_sink, s5_a_re, s5_a_im, s5_log_dt, s5_b_re, s5_b_im, s5_c_re, s5_c_im, s5_d, glu_w_a, glu_w_b, router_w, exp_w_gate, exp_w_up, exp_w_down):
    b_ctx, l_ctx, d = x_prompt.shape
    b_lat, l_lat, _ = x_sample.shape
    depth = norm_w.shape[0]
    n_ctx = b_ctx * l_ctx
    n_lat = b_lat * l_lat
    tiles_ctx_seq = l_ctx // TOK_TILE
    tiles_lat_seq = l_lat // TOK_TILE
    ctx_tiles = n_ctx // TOK_TILE

    def rowmap(i):
        return jnp.where(i < ctx_tiles, i // tiles_ctx_seq, b_ctx + (i - ctx_tiles) // tiles_lat_seq)

    cond = jnp.concatenate([c_ctx[None, :], c, jnp.zeros((8 - 1 - b_lat, d), F32)], axis=0)
    mod_small = ada_modulation(cond, ada_w, ada_b)
    seq_rows = jnp.concatenate([jnp.zeros((b_ctx,), jnp.int32), 1 + jnp.arange(b_lat, dtype=jnp.int32)])
    mods = mod_small[:, seq_rows][:, :, None, :]

    x = jnp.concatenate([x_prompt.reshape(n_ctx, d), x_sample.reshape(n_lat, d)], axis=0)
    groups = ((0, n_ctx), (n_ctx, n_lat))
    rope = rope_tables(l_lat)
    ks, vs, hs, ss = [], [], [], []
    for l in range(depth):
        mod = mods[l]
        nw1 = norm_w[l, 0].reshape(1, d)
        nw2 = norm_w[l, 1].reshape(1, d)
        if l % 2 == 0:
            e = l // 2
            proj = norm_mod_matmul(x, nw1, mod, rowmap, 1, 0, w_in_ab[e].astype(BF16))
            width = proj.shape[1]
            proj_c = proj[:n_ctx].reshape(b_ctx, l_ctx, width)
            proj_l = proj[n_ctx:].reshape(b_lat, l_lat, width)
            zero_state = jnp.zeros((b_ctx, 2, A_HEADS, A_DK, A_DK), F32)
            of_c, ob_c, st_c = hgrn2_mixer(proj_c, hgrn_lb_logits, zero_state, e)
            of_l, ob_l, _ = hgrn2_mixer(proj_l, hgrn_lb_logits, jnp.swapaxes(state_hgrn[:, e], -1, -2), e)
            hs.append(jnp.swapaxes(st_c, -1, -2))
            qn_c, kn_c = qk_prepare(proj_c, q_norm_w[e], k_norm_w[e], None)
            att_c = context_attention(qn_c, kn_c, proj_c, attn_sink[e])
            ks.append(kn_c.reshape(b_ctx, l_ctx, B_KV_HEADS, HEAD_DIM).transpose(0, 2, 1, 3))
            v_off = 5 * A_WIDTH + B_WIDTH + KV_WIDTH
            vs.append(proj_c[:, :, v_off:].reshape(b_ctx, l_ctx, B_KV_HEADS, HEAD_DIM).transpose(0, 2, 1, 3))
            qr_l, kr_l = qk_prepare(proj_l, q_norm_w[e], k_norm_w[e], rope)
            att_l = latent_attention(qr_l, kr_l, proj_l, cache_k[:, e], cache_v[:, e], attn_sink[e])
            o_f = jnp.concatenate([of_c.reshape(n_ctx, A_WIDTH), of_l.reshape(n_lat, A_WIDTH)], axis=0)
            o_b = jnp.concatenate([ob_c.reshape(n_ctx, A_WIDTH), ob_l.reshape(n_lat, A_WIDTH)], axis=0)
            o_att = jnp.concatenate([att_c.reshape(n_ctx, B_WIDTH), att_l.reshape(n_lat, B_WIDTH)], axis=0)
            x = even_out_proj(o_f, o_b, proj, o_att, x, mod, rowmap, hgrn_norm_w[e].reshape(1, A_DK),
                              w_out_ab[e].astype(BF16))
        else:
            o = l // 2
            h = norm_mod(x, nw1, mod, rowmap, 1, 0)
            ops = s5_operators(s5_a_re[o], s5_a_im[o], s5_log_dt[o], s5_b_re[o], s5_b_im[o], s5_c_re[o], s5_c_im[o])
            g = d // S5_GROUP
            zero_s5 = jnp.zeros((b_ctx, 2, 2, g, S5_STATE), F32)
            y_c, fin_c = s5_mixer(h[:n_ctx].reshape(b_ctx, l_ctx, d), ops, s5_d[o], zero_s5)
            y_l, _ = s5_mixer(h[n_ctx:].reshape(b_lat, l_lat, d), ops, s5_d[o], state_s5[:, o])
            ss.append(fin_c)
            y = jnp.concatenate([y_c.reshape(n_ctx, d), y_l.reshape(n_lat, d)], axis=0)
            x = glu_residual(y, x, mod, rowmap, glu_w_a[o].astype(BF16), glu_w_b[o].astype(BF16))
        x = moe_layer(x, nw2, mod, rowmap, groups, router_w[l], exp_w_gate[l], exp_w_up[l], exp_w_down[l])
    y_prompt = x[:n_ctx].reshape(b_ctx, l_ctx, d)
    y_sample = x[n_ctx:].reshape(b_lat, l_lat, d)
    return (y_prompt, y_sample, jnp.stack(ks, axis=1), jnp.stack(vs, axis=1),
            jnp.stack(hs, axis=1), jnp.stack(ss, axis=1))
```

```python
import functools

import jax
import jax.numpy as jnp
from jax import lax
from jax.experimental import pallas as pl
from jax.experimental.pallas import tpu as pltpu

F32 = jnp.float32
BF16 = jnp.bfloat16

A_HEADS = 4
A_DK = 128
A_WIDTH = A_HEADS * A_DK
B_HEADS = 8
B_KV_HEADS = 2
HEAD_DIM = 64
B_GROUP = B_HEADS // B_KV_HEADS
B_WIDTH = B_HEADS * HEAD_DIM
KV_WIDTH = B_KV_HEADS * HEAD_DIM
Q_COL = 5 * A_WIDTH
K_COL = Q_COL + B_WIDTH
V_COL = K_COL + KV_WIDTH
WINDOW = 128
GRID_W = 64
ROPE_THETA = 10000.0
S5_GROUP = 16
S5_STATE = 64
S5_CHUNK = 16
S5_TC = S5_CHUNK * S5_GROUP
N_EXPERTS = 16
EC_FACTOR = 2
EPS = 1e-6
NEG_INF = -1e30

HGRN_C = 64
HGRN_SB = 16
TOK_TILE = 256
LANES = 128
VMEM_LIMIT = 56 * 1024 * 1024
STATE_W = 2 * S5_STATE
KCH = 256


def _cparams(sem):
    return pltpu.CompilerParams(dimension_semantics=sem, vmem_limit_bytes=VMEM_LIMIT)


def _sigmoid(x):
    return 1.0 / (1.0 + jnp.exp(-x))


def _silu(x):
    return x * _sigmoid(x)


def _norm_mod(x, nw, sc, sh):
    ms = jnp.mean(x * x, axis=-1, keepdims=True)
    return (x * lax.rsqrt(ms + EPS) * nw) * (1.0 + sc) + sh


def _dot(a, b):
    return jnp.dot(a, b, preferred_element_type=F32)


def _dot_nt(a, b):
    return lax.dot_general(a, b, (((1,), (1,)), ((), ())), preferred_element_type=F32)


def _split3(x):
    hi = x.astype(BF16)
    r1 = x - hi.astype(F32)
    mid = r1.astype(BF16)
    lo = (r1 - mid.astype(F32)).astype(BF16)
    return hi, mid, lo


class Group:
    def __init__(self, row0, bsz, seq):
        self.row0, self.bsz, self.seq = row0, bsz, seq
        self.tokens = bsz * seq
        self.tile0 = row0 // TOK_TILE
        self.tiles = self.tokens // TOK_TILE
        self.seq_tiles = seq // TOK_TILE


def _ada_kernel(c_ref, w_ref, b_ref, o_ref):
    s = _silu(c_ref[...])
    o_ref[0] = _dot(s.astype(BF16), w_ref[0].astype(BF16)) + b_ref[0]


def ada_modulation(cond, ada_w, ada_b):
    depth, d, n = ada_w.shape
    rows = cond.shape[0]
    tn = 1536
    return pl.pallas_call(
        _ada_kernel,
        grid=(depth, n // tn),
        in_specs=[pl.BlockSpec((rows, d), lambda l, j: (0, 0)),
                  pl.BlockSpec((1, d, tn), lambda l, j: (l, 0, j)),
                  pl.BlockSpec((1, 1, tn), lambda l, j: (l, 0, j))],
        out_specs=pl.BlockSpec((1, rows, tn), lambda l, j: (l, 0, j)),
        out_shape=jax.ShapeDtypeStruct((depth, rows, n), F32),
        compiler_params=_cparams(("arbitrary", "arbitrary")),
        name="ada_modulation",
    )(cond, ada_w, ada_b.reshape(depth, 1, n))


def _mod_spec(rowmap, k, d):
    return pl.BlockSpec((1, 1, d), lambda i: (rowmap(i), 0, k))


def _inproj_kernel(x_ref, nw_ref, sc_ref, sh_ref, w_ref, o_ref):
    h = _norm_mod(x_ref[...], nw_ref[...], sc_ref[0], sh_ref[0])
    o_ref[...] = _dot(h.astype(BF16), w_ref[...])


def norm_mod_matmul(x, nw, mod, rowmap, k_sc, k_sh, w_bf16):
    n_tok, d = x.shape
    n = w_bf16.shape[1]
    tm = TOK_TILE
    return pl.pallas_call(
        _inproj_kernel,
        grid=(n_tok // tm,),
        in_specs=[pl.BlockSpec((tm, d), lambda i: (i, 0)),
                  pl.BlockSpec((1, d), lambda i: (0, 0)),
                  _mod_spec(rowmap, k_sc, d), _mod_spec(rowmap, k_sh, d),
                  pl.BlockSpec((d, n), lambda i: (0, 0))],
        out_specs=pl.BlockSpec((tm, n), lambda i: (i, 0)),
        out_shape=jax.ShapeDtypeStruct((n_tok, n), F32),
        compiler_params=_cparams(("arbitrary",)),
        name="norm_mod_matmul",
    )(x, nw, mod, mod, w_bf16)


def _two_group_specs(groups, shape_of):
    first, second = groups
    return [pl.BlockSpec(shape_of, lambda i: (jnp.minimum(i, first.tiles - 1), 0)),
            pl.BlockSpec(shape_of, lambda i: (jnp.maximum(i - first.tiles, 0), 0))]


def _outproj_kernel(ofc_ref, ofl_ref, obc_ref, obl_ref, atc_ref, atl_ref, ga_ref, x_ref, g1_ref, hw_ref, w_ref,
                    o_ref, *, first_tiles):
    in_first = pl.program_id(0) < first_tiles
    o = jnp.where(in_first, ofc_ref[...] + obc_ref[...], ofl_ref[...] + obl_ref[...])
    o_att = jnp.where(in_first, atc_ref[...], atl_ref[...])
    gate = _silu(ga_ref[...])
    hw = hw_ref[...]
    parts = []
    for h in range(A_HEADS):
        sl = slice(h * A_DK, (h + 1) * A_DK)
        oh = o[:, sl]
        ms = jnp.mean(oh * oh, axis=-1, keepdims=True)
        parts.append(((oh * lax.rsqrt(ms + EPS) * hw) * gate[:, sl]).astype(BF16))
    parts.append(o_att.astype(BF16))
    y = _dot(jnp.concatenate(parts, axis=1), w_ref[...])
    o_ref[...] = x_ref[...] + g1_ref[0] * y


def even_out_proj(o_f, o_b, o_att, proj, x, mod, rowmap, groups, hw, w_bf16):
    n_tok, d = x.shape
    tm = TOK_TILE
    aw = A_WIDTH
    return pl.pallas_call(
        functools.partial(_outproj_kernel, first_tiles=groups[0].tiles),
        grid=(n_tok // tm,),
        in_specs=_two_group_specs(groups, (tm, aw)) + _two_group_specs(groups, (tm, aw))
                 + _two_group_specs(groups, (tm, B_WIDTH))
                 + [pl.BlockSpec((tm, aw), lambda i: (i, 4)),
                    pl.BlockSpec((tm, d), lambda i: (i, 0)),
                    _mod_spec(rowmap, 2, d),
                    pl.BlockSpec((1, A_DK), lambda i: (0, 0)),
                    pl.BlockSpec((aw + B_WIDTH, d), lambda i: (0, 0))],
        out_specs=pl.BlockSpec((tm, d), lambda i: (i, 0)),
        out_shape=jax.ShapeDtypeStruct((n_tok, d), F32),
        compiler_params=_cparams(("arbitrary",)),
        name="even_out_proj",
    )(o_f[0], o_f[1], o_b[0], o_b[1], o_att[0], o_att[1], proj, x, mod, hw, w_bf16)


def _hgrn_chunk(q, k, v, g, st, msel, rev):
    c = q.shape[0]
    nb = c // HGRN_SB
    row = lax.broadcasted_iota(jnp.int32, (c, c), 0)
    col = lax.broadcasted_iota(jnp.int32, (c, c), 1)
    tri = jnp.where((col >= row) if rev else (col <= row), 1.0, 0.0).astype(BF16)
    gh, gm, gl = _split3(g)
    b = _dot(tri, gh) + _dot(tri, gm) + _dot(tri, gl)
    b_edge = b[0:1] if rev else b[c - 1:c]
    qs = q * jnp.exp(b)
    kdec = k * jnp.exp(b_edge - b)

    lk = jnp.log(jnp.maximum(k, 0.0))
    half = HGRN_SB // 2
    zero_half = jnp.zeros((half, A_DK), F32)
    slabs = []
    for i in range(nb):
        sl = slice(i * HGRN_SB, (i + 1) * HGRN_SB)
        bi, qi, lki = b[sl], q[sl], lk[sl]
        ci = bi - lki
        pieces = []
        for s in range(HGRN_SB):
            s_half = s // half
            halves = []
            for hh in range(2):
                rows = slice(hh * half, (hh + 1) * half)
                if (hh > s_half) if rev else (hh < s_half):
                    halves.append(zero_half)
                    continue
                d = bi[rows] - ci[s:s + 1]
                if hh == s_half:
                    d = jnp.minimum(d, lki[s:s + 1])
                halves.append(qi[rows] * jnp.exp(d))
            pieces.append(jnp.concatenate(halves, axis=0).astype(BF16))
        slabs.append(jnp.concatenate(pieces, axis=1))
    a_loc = _dot(jnp.concatenate(slabs, axis=0), msel)

    lane = lax.broadcasted_iota(jnp.int32, (HGRN_SB, LANES), 1)
    rloc = lax.broadcasted_iota(jnp.int32, (HGRN_SB, LANES), 0)
    dmask = ((lane >= rloc) & (lane < HGRN_SB)) if rev else (lane <= rloc)
    krow = lax.broadcasted_iota(jnp.int32, (c, A_DK), 0)
    att_rows = []
    for i in range(nb):
        sl = slice(i * HGRN_SB, (i + 1) * HGRN_SB)
        a_d = jnp.where(dmask, a_loc[sl], 0.0)
        if i > 0:
            a_d = pltpu.roll(a_d, i * HGRN_SB, axis=1)
        a_i = a_d[:, :c]
        has_off = (i < nb - 1) if rev else (i > 0)
        if has_off:
            edge = (i + 1) * HGRN_SB if rev else i * HGRN_SB
            r = b[edge:edge + 1] if rev else b[edge - 1:edge]
            qp = q[sl] * jnp.exp(b[sl] - r)
            live = (krow >= edge) if rev else (krow < edge)
            kp = jnp.where(live, k * jnp.exp(jnp.minimum(r - b, 0.0)), 0.0)
            a_i = a_i + _dot_nt(qp.astype(BF16), kp.astype(BF16))
        att_rows.append(a_i)
    att = jnp.concatenate(att_rows, axis=0)

    vb = v.astype(BF16)
    o = _dot(att.astype(BF16), vb) + _dot_nt(qs.astype(BF16), st.astype(BF16))
    st_new = st * jnp.exp(b_edge) + _dot(v.T.astype(BF16), kdec.astype(BF16))
    return o, st_new


def _hgrn_kernel(qf_ref, vf_ref, ff_ref, qb_ref, vb_ref, fb_ref, lbl_ref, msel_ref, s0_ref,
                 of_ref, ob_ref, sout_ref, st_ref, *, layer):
    c_idx = pl.program_id(1)

    @pl.when(c_idx == 0)
    def _():
        st_ref[...] = s0_ref[0]

    lg = lbl_ref[...]
    ex = jnp.exp(lg - jnp.max(lg, axis=0, keepdims=True))
    pr = ex / jnp.sum(ex, axis=0, keepdims=True)
    lb = jnp.zeros_like(pr[0])
    for e in range(1, layer + 1):
        lb = lb + pr[e]
    msel = msel_ref[...]

    for d, (q_ref, v_ref, f_ref, o_ref) in enumerate(((qf_ref, vf_ref, ff_ref, of_ref),
                                                       (qb_ref, vb_ref, fb_ref, ob_ref))):
        q_all = _silu(q_ref[...])
        v_all = v_ref[...]
        lbd = lb[d:d + 1]
        forget = lbd + (1.0 - lbd) * _sigmoid(f_ref[...])
        k_all = 1.0 - forget
        g_all = jnp.log(forget)
        for h in range(A_HEADS):
            sl = slice(h * A_DK, (h + 1) * A_DK)
            o, st_new = _hgrn_chunk(q_all[:, sl], k_all[:, sl], v_all[:, sl], g_all[:, sl],
                                    st_ref[d, h], msel, rev=(d == 1))
            o_ref[:, sl] = o
            st_ref[d, h] = st_new

    @pl.when(c_idx == pl.num_programs(1) - 1)
    def _():
        sout_ref[0] = st_ref[...]


def hgrn2_mixer(proj, grp, lb_logits, s0t, layer):
    c = HGRN_C
    nc = grp.seq // c
    blk0 = grp.row0 // c
    aw = A_WIDTH
    msel = jnp.repeat(jnp.eye(HGRN_SB, LANES, dtype=BF16), A_DK, axis=0)

    def fwd(col):
        return pl.BlockSpec((c, aw), lambda b, i: (blk0 + b * nc + i, col))

    def bwd(col):
        return pl.BlockSpec((c, aw), lambda b, i: (blk0 + b * nc + nc - 1 - i, col))

    st_spec = pl.BlockSpec((1, 2, A_HEADS, A_DK, A_DK), lambda b, i: (b, 0, 0, 0, 0))
    return pl.pallas_call(
        functools.partial(_hgrn_kernel, layer=layer),
        grid=(grp.bsz, nc),
        in_specs=[fwd(0), fwd(3), fwd(1), bwd(0), bwd(3), bwd(2),
                  pl.BlockSpec(lb_logits.shape, lambda b, i: (0, 0, 0)),
                  pl.BlockSpec(msel.shape, lambda b, i: (0, 0)),
                  st_spec],
        out_specs=[pl.BlockSpec((c, aw), lambda b, i: (b * nc + i, 0)),
                   pl.BlockSpec((c, aw), lambda b, i: (b * nc + nc - 1 - i, 0)),
                   st_spec],
        out_shape=[jax.ShapeDtypeStruct((grp.tokens, aw), F32),
                   jax.ShapeDtypeStruct((grp.tokens, aw), F32),
                   jax.ShapeDtypeStruct((grp.bsz, 2, A_HEADS, A_DK, A_DK), F32)],
        scratch_shapes=[pltpu.VMEM((2, A_HEADS, A_DK, A_DK), F32)],
        compiler_params=_cparams(("arbitrary", "arbitrary")),
        name="hgrn2_mixer",
    )(proj, proj, proj, proj, proj, proj, lb_logits, msel, s0t)


def _head_norm(x, w, gmat):
    hi, mid, lo = _split3(x * x)
    ms = _dot(hi, gmat) + _dot(mid, gmat) + _dot(lo, gmat)
    return x * lax.rsqrt(ms + EPS) * w


def _rope(x, cos, sin_signed):
    width = x.shape[1]
    lane = lax.broadcasted_iota(jnp.int32, x.shape, 1)
    nxt = pltpu.roll(x, width - 1, axis=1)
    prv = pltpu.roll(x, 1, axis=1)
    partner = jnp.where(lane % 2 == 0, nxt, prv)
    return x * cos + partner * sin_signed


def _qkprep_kernel(*refs, rope):
    if rope:
        q_ref, k_ref, qw_ref, kw_ref, gm_ref, cos_ref, sin_ref, qo_ref, ko_ref = refs
    else:
        q_ref, k_ref, qw_ref, kw_ref, gm_ref, qo_ref, ko_ref = refs
    gm = gm_ref[...]
    qn = _head_norm(q_ref[...], qw_ref[...], gm)
    kn = _head_norm(k_ref[...], kw_ref[...], gm[:KV_WIDTH, :KV_WIDTH])
    if rope:
        cos = cos_ref[...]
        sin = sin_ref[...]
        qn = _rope(qn, cos, sin)
        kn = _rope(kn, cos[:, :KV_WIDTH], sin[:, :KV_WIDTH])
    qo_ref[...] = (qn * (HEAD_DIM ** -0.5)).astype(BF16)
    ko_ref[...] = kn


def qk_prepare(proj, grp, qw, kw, rope_tabs):
    tm = TOK_TILE
    st = grp.seq_tiles
    gidx = jnp.arange(B_WIDTH) // HEAD_DIM
    gmat = jnp.where(gidx[:, None] == gidx[None, :], 1.0 / HEAD_DIM, 0.0).astype(BF16)
    qw_t = jnp.tile(qw, B_HEADS).reshape(1, B_WIDTH)
    kw_t = jnp.tile(kw, B_KV_HEADS).reshape(1, KV_WIDTH)
    in_specs = [pl.BlockSpec((tm, B_WIDTH), lambda i: (grp.tile0 + i, Q_COL // B_WIDTH)),
                pl.BlockSpec((tm, KV_WIDTH), lambda i: (grp.tile0 + i, K_COL // KV_WIDTH)),
                pl.BlockSpec((1, B_WIDTH), lambda i: (0, 0)),
                pl.BlockSpec((1, KV_WIDTH), lambda i: (0, 0)),
                pl.BlockSpec((B_WIDTH, B_WIDTH), lambda i: (0, 0))]
    args = [proj, proj, qw_t, kw_t, gmat]
    if rope_tabs is not None:
        in_specs += [pl.BlockSpec((tm, B_WIDTH), lambda i: (i % st, 0)),
                     pl.BlockSpec((tm, B_WIDTH), lambda i: (i % st, 0))]
        args += list(rope_tabs)
    return pl.pallas_call(
        functools.partial(_qkprep_kernel, rope=rope_tabs is not None),
        grid=(grp.tiles,),
        in_specs=in_specs,
        out_specs=[pl.BlockSpec((tm, B_WIDTH), lambda i: (i, 0)),
                   pl.BlockSpec((tm, KV_WIDTH), lambda i: (i, 0))],
        out_shape=[jax.ShapeDtypeStruct((grp.tokens, B_WIDTH), BF16),
                   jax.ShapeDtypeStruct((grp.tokens, KV_WIDTH), F32)],
        compiler_params=_cparams(("arbitrary",)),
        name="qk_prepare",
    )(*args)


def rope_tables(seq):
    pos = jnp.arange(seq)
    row = (pos // GRID_W).astype(F32)
    col = (pos % GRID_W).astype(F32)
    n_pair = HEAD_DIM // 4
    freqs = ROPE_THETA ** (-jnp.arange(n_pair, dtype=F32) / n_pair)
    ang = jnp.concatenate([row[:, None] * freqs, col[:, None] * freqs], axis=-1)
    cos = jnp.repeat(jnp.cos(ang), 2, axis=-1)
    sin = jnp.repeat(jnp.sin(ang), 2, axis=-1) * jnp.tile(jnp.array([-1.0, 1.0], F32), HEAD_DIM // 2)
    return jnp.tile(cos, (1, B_HEADS)), jnp.tile(sin, (1, B_HEADS))


def _value_with_ones(v2, kvh):
    lane = lax.broadcasted_iota(jnp.int32, v2.shape, 1)
    if kvh == 1:
        v2 = pltpu.roll(v2, HEAD_DIM, axis=1)
    return jnp.where(lane < HEAD_DIM, v2, 1.0).astype(BF16)


def _group_attention(q, kk, vv1, valid, sink_ref, kvh, o_ref):
    tq = q.shape[0]
    heads = [kvh * B_GROUP + gq for gq in range(B_GROUP)]
    qs = jnp.concatenate([q[:, h * HEAD_DIM:(h + 1) * HEAD_DIM] for h in heads], axis=0)
    sink = jnp.concatenate([jnp.broadcast_to(sink_ref[h:h + 1, 0:1], (tq, 1)) for h in heads], axis=0)
    s = _dot_nt(qs, kk)
    if valid is not None:
        s = jnp.where(jnp.concatenate([valid] * B_GROUP, axis=0), s, NEG_INF)
    m = jnp.maximum(jnp.max(s, axis=1, keepdims=True), sink)
    pv = _dot(jnp.exp(s - m).astype(BF16), vv1)
    den = pv[:, HEAD_DIM:HEAD_DIM + 1] + jnp.exp(sink - m)
    o = pv[:, :HEAD_DIM] / den
    for gq, h in enumerate(heads):
        o_ref[:, h * HEAD_DIM:(h + 1) * HEAD_DIM] = o[gq * tq:(gq + 1) * tq]


def _ctx_attn_kernel(q_ref, k_ref, v_ref, sink_ref, o_ref):
    q = q_ref[...]
    k = k_ref[...].astype(BF16)
    v = v_ref[...]
    for kvh in range(B_KV_HEADS):
        ks = slice(kvh * HEAD_DIM, (kvh + 1) * HEAD_DIM)
        _group_attention(q, k[:, ks], _value_with_ones(v, kvh), None, sink_ref, kvh, o_ref)


def context_attention(qn, kn, proj, grp, sink):
    seq = grp.seq
    blk0 = grp.row0 // seq
    sink_t = jnp.broadcast_to(sink.reshape(B_HEADS, 1), (B_HEADS, LANES))
    return pl.pallas_call(
        _ctx_attn_kernel,
        grid=(grp.bsz,),
        in_specs=[pl.BlockSpec((seq, B_WIDTH), lambda b: (b, 0)),
                  pl.BlockSpec((seq, KV_WIDTH), lambda b: (b, 0)),
                  pl.BlockSpec((seq, KV_WIDTH), lambda b: (blk0 + b, V_COL // KV_WIDTH)),
                  pl.BlockSpec((B_HEADS, LANES), lambda b: (0, 0))],
        out_specs=pl.BlockSpec((seq, B_WIDTH), lambda b: (b, 0)),
        out_shape=jax.ShapeDtypeStruct((grp.tokens, B_WIDTH), F32),
        compiler_params=_cparams(("arbitrary",)),
        name="context_attention",
    )(qn, kn, proj, sink_t)


def _lat_attn_kernel(q_ref, kp_ref, kc_ref, kn_ref, vp_ref, vc_ref, vn_ref, kx_ref, vx_ref, sink_ref, o_ref):
    blk = pl.program_id(1)
    nblk = pl.num_programs(1)
    tq = q_ref.shape[0]
    q = q_ref[...]
    kl = jnp.concatenate([kp_ref[...], kc_ref[...], kn_ref[...]], axis=0).astype(BF16)
    vl = jnp.concatenate([vp_ref[...], vc_ref[...], vn_ref[...]], axis=0)
    n_ctx = kx_ref.shape[2]
    span = 3 * tq
    i = lax.broadcasted_iota(jnp.int32, (tq, span + n_ctx), 0)
    j = lax.broadcasted_iota(jnp.int32, (tq, span + n_ctx), 1)
    dist = j - tq - i
    valid = (dist >= -WINDOW) & (dist <= WINDOW)
    valid = valid & ((j >= tq) | (blk > 0)) & ((j < 2 * tq) | (blk < nblk - 1))
    valid = valid | (j >= span)
    for kvh in range(B_KV_HEADS):
        ks = slice(kvh * HEAD_DIM, (kvh + 1) * HEAD_DIM)
        kk = jnp.concatenate([kl[:, ks], kx_ref[0, kvh].astype(BF16)], axis=0)
        vv1 = jnp.concatenate([_value_with_ones(vl, kvh), vx_ref[0, kvh].astype(BF16)], axis=0)
        _group_attention(q, kk, vv1, valid, sink_ref, kvh, o_ref)


def latent_attention(qr, kr, proj, grp, k_ctx, v_ctx, sink):
    tq = WINDOW
    nblk = grp.seq // tq
    blk0 = grp.row0 // tq
    n_ctx = k_ctx.shape[2]
    sink_t = jnp.broadcast_to(sink.reshape(B_HEADS, 1), (B_HEADS, LANES))
    v_ctx1 = jnp.concatenate([v_ctx, jnp.ones_like(v_ctx)], axis=-1)

    def kv_specs(off, col):
        return [pl.BlockSpec((tq, KV_WIDTH), lambda b, i: (off + b * nblk + jnp.maximum(i - 1, 0), col)),
                pl.BlockSpec((tq, KV_WIDTH), lambda b, i: (off + b * nblk + i, col)),
                pl.BlockSpec((tq, KV_WIDTH), lambda b, i: (off + b * nblk + jnp.minimum(i + 1, nblk - 1), col))]

    return pl.pallas_call(
        _lat_attn_kernel,
        grid=(grp.bsz, nblk),
        in_specs=[pl.BlockSpec((tq, B_WIDTH), lambda b, i: (b * nblk + i, 0))]
                 + kv_specs(0, 0) + kv_specs(blk0, V_COL // KV_WIDTH)
                 + [pl.BlockSpec((1, B_KV_HEADS, n_ctx, HEAD_DIM), lambda b, i: (b, 0, 0, 0)),
                    pl.BlockSpec((1, B_KV_HEADS, n_ctx, 2 * HEAD_DIM), lambda b, i: (b, 0, 0, 0)),
                    pl.BlockSpec((B_HEADS, LANES), lambda b, i: (0, 0))],
        out_specs=pl.BlockSpec((tq, B_WIDTH), lambda b, i: (b * nblk + i, 0)),
        out_shape=jax.ShapeDtypeStruct((grp.tokens, B_WIDTH), F32),
        compiler_params=_cparams(("arbitrary", "arbitrary")),
        name="latent_attention",
    )(qr, kr, kr, kr, proj, proj, proj, k_ctx, v_ctx1, sink_t)


def s5_operators(a_re, a_im, log_dt, b_re, b_im, c_re, c_im):
    t = S5_CHUNK
    hi = lax.Precision.HIGHEST
    ks, ws, wsw, vs, a1s, a2s = [], [], [], [], [], []
    for d in range(2):
        are, aim = a_re[d].astype(F32), a_im[d].astype(F32)
        dt = jnp.exp(log_dt[d].astype(F32))[:, None]
        den = are * are + aim * aim
        steps = jnp.arange(t + 1, dtype=F32)[:, None, None]
        mag = jnp.exp(steps * (dt * are))
        pw_re = mag * jnp.cos(steps * (dt * aim))
        pw_im = mag * jnp.sin(steps * (dt * aim))
        ab_re, ab_im = pw_re[1], pw_im[1]
        f_re = ((ab_re - 1.0) * are + ab_im * aim) / den
        f_im = (ab_im * are - (ab_re - 1.0) * aim) / den
        bre, bim = b_re[d].astype(F32), b_im[d].astype(F32)
        bb_re = f_re[..., None] * bre - f_im[..., None] * bim
        bb_im = f_re[..., None] * bim + f_im[..., None] * bre
        cre, cim = c_re[d].astype(F32), c_im[d].astype(F32)
        pgr = pw_re.transpose(1, 0, 2)[:, :, None, :]
        pgi = pw_im.transpose(1, 0, 2)[:, :, None, :]
        cp_re = cre[:, None] * pgr - cim[:, None] * pgi
        cp_im = cre[:, None] * pgi + cim[:, None] * pgr
        m = (jnp.einsum('gkcp,gpd->gkcd', cp_re[:, :t], bb_re, precision=hi)
             - jnp.einsum('gkcp,gpd->gkcd', cp_im[:, :t], bb_im, precision=hi))
        s_i = jnp.arange(t)[:, None]
        t_i = jnp.arange(t)[None, :]
        lag = (t_i - s_i) if d == 0 else (s_i - t_i)
        blk = jnp.where((lag >= 0)[None, :, :, None, None], m[:, jnp.clip(lag, 0, t - 1)], 0.0)
        ks.append(blk.transpose(0, 1, 4, 2, 3).reshape(-1, S5_TC, S5_TC))
        pidx = (t - 1 - jnp.arange(t)) if d == 0 else jnp.arange(t)
        pr = pw_re[pidx].transpose(1, 0, 2)[:, :, None, :]
        pi = pw_im[pidx].transpose(1, 0, 2)[:, :, None, :]
        bbr = bb_re.transpose(0, 2, 1)[:, None]
        bbi = bb_im.transpose(0, 2, 1)[:, None]
        w_re = pr * bbr - pi * bbi
        w_im = pr * bbi + pi * bbr
        ws.append(jnp.concatenate([w_re, w_im], axis=-1).reshape(-1, S5_TC, 2 * S5_STATE))
        wsw.append(jnp.concatenate([w_im, w_re], axis=-1).reshape(-1, S5_TC, 2 * S5_STATE))
        kidx = (jnp.arange(t) + 1) if d == 0 else (t - jnp.arange(t))
        v = jnp.concatenate([cp_re[:, kidx], -cp_im[:, kidx]], axis=-1)
        vs.append(v.transpose(0, 3, 1, 2).reshape(-1, 2 * S5_STATE, S5_TC))
        a1s.append(jnp.concatenate([pw_re[t], pw_re[t]], axis=-1))
        a2s.append(jnp.concatenate([-pw_im[t], pw_im[t]], axis=-1))
    return (ks[0] + ks[1], jnp.concatenate(ws + wsw, axis=-1), jnp.concatenate(vs, axis=1),
            jnp.concatenate(a1s, axis=-1), jnp.concatenate(a2s, axis=-1))


def _to_chunks_kernel(x_ref, nw_ref, sc_ref, sh_ref, o_ref, hbuf):
    h = _norm_mod(x_ref[...], nw_ref[...], sc_ref[0], sh_ref[0])
    gpl = LANES // S5_GROUP
    for c in range(hbuf.shape[0]):
        hbuf[c] = h[:, c * LANES:(c + 1) * LANES]
    for c in range(hbuf.shape[0]):
        for t in range(S5_CHUNK):
            rows = hbuf[c, pl.ds(t, TOK_TILE // S5_CHUNK, stride=S5_CHUNK), :]
            for g in range(gpl):
                o_ref[c * gpl + g, :, t * S5_GROUP:(t + 1) * S5_GROUP] = rows[:, g * S5_GROUP:(g + 1) * S5_GROUP]


def s5_to_chunks(x, grp, nw, mod, rowmap):
    d = x.shape[1]
    ng = d // S5_GROUP
    tm = TOK_TILE
    cpt = tm // S5_CHUNK
    st = grp.seq_tiles

    def mspec(k):
        return pl.BlockSpec((1, 1, d), lambda i: (rowmap(grp.tile0 + i), 0, k))

    return pl.pallas_call(
        _to_chunks_kernel,
        grid=(grp.tiles,),
        in_specs=[pl.BlockSpec((tm, d), lambda i: (grp.tile0 + i, 0)),
                  pl.BlockSpec((1, d), lambda i: (0, 0)), mspec(1), mspec(0)],
        out_specs=pl.BlockSpec((ng, cpt, S5_TC), lambda i: (0, i % st, i // st)),
        out_shape=jax.ShapeDtypeStruct((ng, grp.seq // S5_CHUNK, grp.bsz * S5_TC), F32),
        scratch_shapes=[pltpu.VMEM((d // LANES, tm, LANES), F32)],
        compiler_params=_cparams(("arbitrary",)),
        name="s5_to_chunks",
    )(x, nw, mod, mod)


def _s5_states_kernel(uf_ref, ub_ref, w_ref, a1_ref, a2_ref, s0_ref, stf_ref, stb_ref, fin_ref, dbuf, carry, *, bsz):
    r = pl.program_id(1)
    gb, tr, _ = uf_ref.shape
    nc = tr // bsz
    sw = STATE_W

    @pl.when(r == 0)
    def _():
        for g in range(gb):
            s0 = s0_ref[g]
            for d in range(2):
                s = s0[:, d * sw:(d + 1) * sw]
                carry[g, 2 * d] = s
                carry[g, 2 * d + 1] = pltpu.roll(s, sw // 2, axis=1)

    for g in range(gb):
        w = w_ref[g]
        dbuf[g, 0] = _dot(uf_ref[g].astype(BF16), w[:, :2 * sw])
        dbuf[g, 1] = _dot(ub_ref[g].astype(BF16), w[:, 2 * sw:])

    for g in range(gb):
        a1 = a1_ref[g]
        a2 = a2_ref[g]
        for d, st_ref in enumerate((stf_ref, stb_ref)):
            a1d = a1[:, d * sw:(d + 1) * sw]
            a2d = a2[:, d * sw:(d + 1) * sw]
            s = carry[g, 2 * d]
            x = carry[g, 2 * d + 1]
            for c in range(nc):
                rows = (c if d == 0 else nc - 1 - c) * bsz
                st_ref[rows:rows + bsz, g * sw:(g + 1) * sw] = s
                dd = dbuf[g, d, rows:rows + bsz, :]
                s, x = a1d * s + a2d * x + dd[:, :sw], a1d * x - a2d * s + dd[:, sw:]
            carry[g, 2 * d] = s
            carry[g, 2 * d + 1] = x

    @pl.when(r == pl.num_programs(1) - 1)
    def _():
        for g in range(gb):
            fin_ref[g, :, :sw] = carry[g, 0]
            fin_ref[g, :, sw:] = carry[g, 2]


def s5_chunk_states_scan(u, w_bf16, a1, a2, s0g, bsz):
    g, r, tc = u.shape
    n = w_bf16.shape[2]
    sw = STATE_W
    gb = 8
    tr = min(r, 256)
    nblk = r // tr
    return pl.pallas_call(
        functools.partial(_s5_states_kernel, bsz=bsz),
        grid=(g // gb, nblk),
        in_specs=[pl.BlockSpec((gb, tr, tc), lambda i, j: (i, j, 0)),
                  pl.BlockSpec((gb, tr, tc), lambda i, j: (i, nblk - 1 - j, 0)),
                  pl.BlockSpec((gb, tc, n), lambda i, j: (i, 0, 0)),
                  pl.BlockSpec((gb, 1, 2 * sw), lambda i, j: (i, 0, 0)),
                  pl.BlockSpec((gb, 1, 2 * sw), lambda i, j: (i, 0, 0)),
                  pl.BlockSpec((gb, bsz, 2 * sw), lambda i, j: (i, 0, 0))],
        out_specs=[pl.BlockSpec((tr, gb * sw), lambda i, j: (j, i)),
                   pl.BlockSpec((tr, gb * sw), lambda i, j: (nblk - 1 - j, i)),
                   pl.BlockSpec((gb, bsz, 2 * sw), lambda i, j: (i, 0, 0))],
        out_shape=[jax.ShapeDtypeStruct((r, g * sw), F32),
                   jax.ShapeDtypeStruct((r, g * sw), F32),
                   jax.ShapeDtypeStruct((g, bsz, 2 * sw), F32)],
        scratch_shapes=[pltpu.VMEM((gb, 2, tr, 2 * sw), F32), pltpu.VMEM((gb, 4, bsz, sw), F32)],
        compiler_params=_cparams(("arbitrary", "arbitrary")),
        name="s5_chunk_states_scan",
    )(u, u, w_bf16, a1.reshape(g, 1, 2 * sw), a2.reshape(g, 1, 2 * sw), s0g)


def _s5_out_kernel(u_ref, sf_ref, sb_ref, k_ref, v_ref, o_ref):
    gb = u_ref.shape[0]
    sw = STATE_W
    for g in range(gb):
        s = jnp.concatenate([sf_ref[:, g * sw:(g + 1) * sw], sb_ref[:, g * sw:(g + 1) * sw]], axis=1)
        o_ref[g] = _dot(u_ref[g].astype(BF16), k_ref[g]) + _dot(s.astype(BF16), v_ref[g])


def s5_outputs(u, st_f, st_b, k_bf16, v_bf16):
    g, r, tc = u.shape
    sw = STATE_W
    gb = 8
    tr = min(r, 512)
    return pl.pallas_call(
        _s5_out_kernel,
        grid=(g // gb, r // tr),
        in_specs=[pl.BlockSpec((gb, tr, tc), lambda i, j: (i, j, 0)),
                  pl.BlockSpec((tr, gb * sw), lambda i, j: (j, i)),
                  pl.BlockSpec((tr, gb * sw), lambda i, j: (j, i)),
                  pl.BlockSpec((gb, tc, tc), lambda i, j: (i, 0, 0)),
                  pl.BlockSpec((gb, 2 * sw, tc), lambda i, j: (i, 0, 0))],
        out_specs=pl.BlockSpec((gb, tr, tc), lambda i, j: (i, j, 0)),
        out_shape=jax.ShapeDtypeStruct((g, r, tc), F32),
        compiler_params=_cparams(("arbitrary", "arbitrary")),
        name="s5_outputs",
    )(u, st_f, st_b, k_bf16, v_bf16)


def s5_mixer(x, grp, nw, mod, rowmap, ops, s0):
    k_tot, w_tot, v_tot, a1, a2 = ops
    ng = k_tot.shape[0]
    bsz = grp.bsz
    nj = grp.seq // S5_CHUNK
    sw = STATE_W
    u4 = s5_to_chunks(x, grp, nw, mod, rowmap)
    u = u4.reshape(ng, nj * bsz, S5_TC)
    w = jnp.concatenate([w_tot[:, :, :sw], w_tot[:, :, 2 * sw:3 * sw], w_tot[:, :, sw:2 * sw], w_tot[:, :, 3 * sw:]], axis=-1)
    s0g = s0.transpose(3, 0, 1, 2, 4).reshape(ng, bsz, 2 * sw)
    st_f, st_b, final = s5_chunk_states_scan(u, w.astype(BF16), a1, a2, s0g, bsz)
    y = s5_outputs(u, st_f, st_b, k_tot.astype(BF16), v_tot.astype(BF16))
    final = final.reshape(ng, bsz, 2, 2, S5_STATE).transpose(1, 2, 3, 0, 4)
    return y.reshape(ng, nj, bsz * S5_TC), final


def _glu_kernel(yc_ref, yl_ref, x_ref, nw_ref, sc_ref, sh_ref, g1_ref, dsk_ref, wa_ref, wb_ref, o_ref, ybuf,
                *, first_tiles):
    gpl = LANES // S5_GROUP

    def from_chunks(y_ref):
        cpt = y_ref.shape[1]
        for c in range(ybuf.shape[0]):
            for t in range(S5_CHUNK):
                for g in range(gpl):
                    ybuf[c, t * cpt:(t + 1) * cpt, g * S5_GROUP:(g + 1) * S5_GROUP] = (
                        y_ref[c * gpl + g, :, t * S5_GROUP:(t + 1) * S5_GROUP])

    in_first = pl.program_id(0) < first_tiles

    @pl.when(in_first)
    def _():
        from_chunks(yc_ref)

    @pl.when(jnp.logical_not(in_first))
    def _():
        from_chunks(yl_ref)

    cpt = TOK_TILE // S5_CHUNK
    y = jnp.concatenate(
        [jnp.concatenate([ybuf[c, pl.ds(j, S5_CHUNK, stride=cpt), :] for c in range(ybuf.shape[0])], axis=1)
         for j in range(cpt)], axis=0)
    x = x_ref[...]
    y = y + dsk_ref[...] * _norm_mod(x, nw_ref[...], sc_ref[0], sh_ref[0])
    yb = jax.nn.gelu(y, approximate=True).astype(BF16)
    a = _dot(yb, wa_ref[...])
    b = _dot(yb, wb_ref[...])
    o_ref[...] = x + g1_ref[0] * (a * _sigmoid(b))


def glu_residual(y_chunks, x, nw, mod, rowmap, groups, dskip, wa_bf16, wb_bf16):
    n_tok, d = x.shape
    tm = TOK_TILE
    ng = d // S5_GROUP
    cpt = tm // S5_CHUNK
    first, second = groups
    blk = (ng, cpt, S5_TC)
    return pl.pallas_call(
        functools.partial(_glu_kernel, first_tiles=first.tiles),
        grid=(n_tok // tm,),
        in_specs=[pl.BlockSpec(blk, lambda i: (0, jnp.minimum(i, first.tiles - 1) % first.seq_tiles,
                                               jnp.minimum(i, first.tiles - 1) // first.seq_tiles)),
                  pl.BlockSpec(blk, lambda i: (0, jnp.maximum(i - first.tiles, 0) % second.seq_tiles,
                                               jnp.maximum(i - first.tiles, 0) // second.seq_tiles)),
                  pl.BlockSpec((tm, d), lambda i: (i, 0)),
                  pl.BlockSpec((1, d), lambda i: (0, 0)),
                  _mod_spec(rowmap, 1, d), _mod_spec(rowmap, 0, d), _mod_spec(rowmap, 2, d),
                  pl.BlockSpec((1, d), lambda i: (0, 0)),
                  pl.BlockSpec((d, d), lambda i: (0, 0)),
                  pl.BlockSpec((d, d), lambda i: (0, 0))],
        out_specs=pl.BlockSpec((tm, d), lambda i: (i, 0)),
        out_shape=jax.ShapeDtypeStruct((n_tok, d), F32),
        scratch_shapes=[pltpu.VMEM((d // LANES, tm, LANES), F32)],
        compiler_params=_cparams(("arbitrary",)),
        name="glu_residual",
    )(y_chunks[0], y_chunks[1], x, nw, mod, mod, mod, dskip, wa_bf16, wb_bf16)


def _router_kernel(x_ref, nw_ref, sc_ref, sh_ref, wr_ref, h_ref, aff_ref):
    d = x_ref.shape[1]
    tm = x_ref.shape[0]
    h = _norm_mod(x_ref[...], nw_ref[...], sc_ref[0], sh_ref[0])
    h_ref[:, :d] = h
    tok = pl.program_id(0) * tm + lax.broadcasted_iota(jnp.int32, (tm, LANES), 0)
    h_ref[:, d:] = tok.astype(F32)
    logits = _dot_nt(wr_ref[...], h.astype(BF16))
    ex = jnp.exp(logits - jnp.max(logits, axis=0, keepdims=True))
    p = ex / jnp.sum(ex, axis=0, keepdims=True)
    for k in range(aff_ref.shape[0]):
        aff_ref[k] = p[:, k * LANES:(k + 1) * LANES]


def moe_router(x, nw, mod, rowmap, wr_t_bf16):
    n_tok, d = x.shape
    tm = TOK_TILE
    ne = wr_t_bf16.shape[0]
    return pl.pallas_call(
        _router_kernel,
        grid=(n_tok // tm,),
        in_specs=[pl.BlockSpec((tm, d), lambda i: (i, 0)),
                  pl.BlockSpec((1, d), lambda i: (0, 0)),
                  _mod_spec(rowmap, 4, d), _mod_spec(rowmap, 3, d),
                  pl.BlockSpec((ne, d), lambda i: (0, 0))],
        out_specs=[pl.BlockSpec((tm, d + LANES), lambda i: (i, 0)),
                   pl.BlockSpec((tm // LANES, ne, LANES), lambda i: (i, 0, 0))],
        out_shape=[jax.ShapeDtypeStruct((n_tok, d + LANES), F32),
                   jax.ShapeDtypeStruct((n_tok // LANES, ne, LANES), F32)],
        compiler_params=_cparams(("arbitrary",)),
        name="moe_router",
    )(x, nw, mod, mod, wr_t_bf16)


def _select_kernel(aff_ref, ut_ref, v8_ref, cidx_ref, meta_ref, inc_ref, *, cap, first, slot0):
    nt, ne, _ = aff_ref.shape
    aff = aff_ref[...]

    def count(mask):
        c = jnp.sum(jnp.where(mask, 1.0, 0.0), axis=0)
        return jnp.sum(c, axis=1, keepdims=True)

    def as_float(bits):
        return lax.bitcast_convert_type(bits, F32)

    def radix(k, bits):
        cand = bits | (jnp.int32(1) << (30 - k))
        return jnp.where(count(aff >= as_float(cand)[None]) >= cap, cand, bits)

    thr_bits = lax.fori_loop(0, 31, radix, jnp.zeros((ne, 1), jnp.int32))
    thr = as_float(thr_bits)[None]
    nxt = as_float(thr_bits + 1)[None]
    above = aff >= nxt
    bucket = (aff >= thr) & jnp.logical_not(above)
    need = cap - count(above)
    width = nxt - thr
    pos = jnp.where(bucket & (width > 0.0), (aff - thr) / width, 0.0)

    def refine(k, t):
        cand = t + lax.convert_element_type(jnp.int32(1) << (29 - k), F32) * (2.0 ** -30)
        return jnp.where(count(bucket & (pos >= cand[None])) >= need, cand, t)

    t = lax.fori_loop(0, 30, refine, jnp.zeros((ne, 1), F32))
    upper = bucket & (pos >= (t + 2.0 ** -30)[None])
    tied = bucket & (pos >= t[None]) & jnp.logical_not(upper)
    ut = ut_ref[...]

    def excl_rank(mask):
        m = jnp.where(mask, 1.0, 0.0)
        inc_ref[...] = _dot(m.reshape(nt * ne, LANES).astype(BF16), ut).reshape(nt, ne, LANES)

        def body(tt, carry):
            inc = inc_ref[tt]
            inc_ref[tt] = inc + carry
            return carry + inc[:, LANES - 1:LANES]

        lax.fori_loop(0, nt, body, jnp.zeros((ne, 1), F32))
        return inc_ref[...] - m

    sel = above | upper | (tied & (excl_rank(tied) < (need - count(upper))[None]))
    rank = excl_rank(sel)
    start = rank[:, :, 0:1]
    nsel = rank[:, :, LANES - 1:LANES] + jnp.where(sel[:, :, LANES - 1:LANES], 1.0, 0.0) - start
    lane3 = lax.broadcasted_iota(jnp.int32, (nt, ne, LANES), 2)
    meta_ref[...] = jnp.where(lane3 == 0, start + float(slot0), jnp.where(lane3 == 1, nsel, 0.0)).astype(jnp.int32)

    inc_ref[...] = jnp.where(sel, rank - start, -1.0)
    sub = lax.broadcasted_iota(jnp.int32, (LANES, LANES), 0).astype(F32)
    v8 = v8_ref[...]

    def tile_body(tt, carry):
        rho = inc_ref[tt]
        base = lax.convert_element_type(first + tt * LANES, F32)
        for e in range(ne):
            onehot = jnp.where(sub == rho[e:e + 1, :], 1.0, 0.0).astype(BF16)
            packed = _dot_nt(v8, onehot)
            cidx_ref[tt, e:e + 1, :] = (packed[0:1] + base).astype(jnp.int32)
        return carry

    lax.fori_loop(0, nt, tile_body, 0)


def moe_select(aff_t, cap, first, slot0):
    nt, ne, _ = aff_t.shape
    ut = jnp.triu(jnp.ones((LANES, LANES), BF16))
    v8 = jnp.zeros((8, LANES), BF16).at[0].set(jnp.arange(LANES).astype(BF16))
    blk = pl.BlockSpec((nt, ne, LANES), lambda i: (0, 0, 0))
    return pl.pallas_call(
        functools.partial(_select_kernel, cap=cap, first=first, slot0=slot0),
        grid=(1,),
        in_specs=[blk, pl.BlockSpec((LANES, LANES), lambda i: (0, 0)), pl.BlockSpec((8, LANES), lambda i: (0, 0))],
        out_specs=[blk, blk],
        out_shape=[jax.ShapeDtypeStruct((nt, ne, LANES), jnp.int32),
                   jax.ShapeDtypeStruct((nt, ne, LANES), jnp.int32)],
        scratch_shapes=[pltpu.VMEM((nt, ne, LANES), F32)],
        compiler_params=_cparams(("arbitrary",)),
        name="moe_select",
    )(aff_t, ut, v8)


def _lists_kernel(starts_ref, counts_ref, cidx_ref, idx_ref):
    nt, ne, _ = cidx_ref.shape
    idx_ref[...] = jnp.zeros_like(idx_ref)
    lane = lax.broadcasted_iota(jnp.int32, (1, LANES), 1)

    def tile(t, carry):
        for e in range(ne):
            s = starts_ref[e, t]
            c = counts_ref[e, t]
            j0 = s >> 7
            o = s & (LANES - 1)
            rolled = pltpu.roll(cidx_ref[t, e:e + 1, :], o, axis=1)
            end = o + c
            row0 = idx_ref[e, pl.ds(j0, 1), :]
            idx_ref[e, pl.ds(j0, 1), :] = jnp.where((lane >= o) & (lane < end), rolled, row0)
            row1 = idx_ref[e, pl.ds(j0 + 1, 1), :]
            idx_ref[e, pl.ds(j0 + 1, 1), :] = jnp.where(lane < end - LANES, rolled, row1)
        return carry

    lax.fori_loop(0, nt, tile, 0)


def moe_build_lists(starts, counts, cidx, rows):
    nt, ne, _ = cidx.shape
    rt = rows // LANES + 2
    out = pl.pallas_call(
        _lists_kernel,
        grid_spec=pltpu.PrefetchScalarGridSpec(
            num_scalar_prefetch=2,
            grid=(1,),
            in_specs=[pl.BlockSpec((nt, ne, LANES), lambda i, s, c: (0, 0, 0))],
            out_specs=pl.BlockSpec((ne, rt, LANES), lambda i, s, c: (0, 0, 0))),
        out_shape=jax.ShapeDtypeStruct((ne, rt, LANES), jnp.int32),
        compiler_params=_cparams(("arbitrary",)),
        name="moe_build_lists",
    )(starts, counts, cidx)
    return out.reshape(ne * rt * LANES)


def _expert_kernel(idx_ref, h_hbm, wr_ref, wg_ref, wu_ref, wd_ref, y_ref, xbuf, wgb, wub, wdb, sem, *, rows_pad):
    e = pl.program_id(0)
    ch = pl.program_id(1)
    nch = pl.num_programs(1)
    tr = xbuf.shape[1]
    d = wg_ref.shape[1]
    step = e * nch + ch
    slot = step % 2

    @pl.when(ch == 0)
    def _():
        wgb[...] = wg_ref[0].astype(BF16)
        wub[...] = wu_ref[0].astype(BF16)
        wdb[...] = wd_ref[0].astype(BF16)

    def gather(e_i, ch_i, slot_i):
        base = e_i * rows_pad + ch_i * tr

        def group(j, carry):
            for u in range(8):
                r = 8 * j + u
                pltpu.make_async_copy(h_hbm.at[pl.ds(idx_ref[base + r], 1)], xbuf.at[slot_i, pl.ds(r, 1)],
                                      sem.at[slot_i]).start()
            return carry

        lax.fori_loop(0, tr // 8, group, 0)

    @pl.when(step == 0)
    def _():
        gather(e, ch, slot)

    @pl.when(step + 1 < pl.num_programs(0) * nch)
    def _():
        last = ch == nch - 1
        gather(jnp.where(last, e + 1, e), jnp.where(last, 0, ch + 1), 1 - slot)

    pltpu.make_async_copy(h_hbm.at[pl.ds(0, tr)], xbuf.at[slot], sem.at[slot]).wait()

    x = xbuf[slot, :, :d].astype(BF16)
    logits = _dot(x, wr_ref[...])
    lane = lax.broadcasted_iota(jnp.int32, logits.shape, 1)
    ne = pl.num_programs(0)
    logits = jnp.where(lane < ne, logits, NEG_INF)
    ex = jnp.exp(logits - jnp.max(logits, axis=1, keepdims=True))
    gate = jnp.sum(jnp.where(lane == e, ex, 0.0), axis=1, keepdims=True) / jnp.sum(ex, axis=1, keepdims=True)
    hmid = (_silu(_dot(x, wgb[...])) * _dot(x, wub[...])).astype(BF16)
    y_ref[0, :, :d] = _dot(hmid, wdb[...]) * gate
    y_ref[0, :, d:] = xbuf[slot, :, d:]


def moe_experts(idx, rows, h_ext, wr_pad_bf16, w_gate, w_up, w_down):
    ne, d, f = w_gate.shape
    rows_pad = idx.shape[0] // ne
    dx = h_ext.shape[1]
    tr = 512
    return pl.pallas_call(
        functools.partial(_expert_kernel, rows_pad=rows_pad),
        grid_spec=pltpu.PrefetchScalarGridSpec(
            num_scalar_prefetch=1,
            grid=(ne, rows // tr),
            in_specs=[pl.BlockSpec(memory_space=pl.ANY),
                      pl.BlockSpec((d, LANES), lambda e, c, idx: (0, 0)),
                      pl.BlockSpec((1, d, f), lambda e, c, idx: (e, 0, 0)),
                      pl.BlockSpec((1, d, f), lambda e, c, idx: (e, 0, 0)),
                      pl.BlockSpec((1, f, d), lambda e, c, idx: (e, 0, 0))],
            out_specs=pl.BlockSpec((1, tr, dx), lambda e, c, idx: (e, c, 0)),
            scratch_shapes=[pltpu.VMEM((2, tr, dx), F32),
                            pltpu.VMEM((d, f), BF16), pltpu.VMEM((d, f), BF16), pltpu.VMEM((f, d), BF16),
                            pltpu.SemaphoreType.DMA((2,))]),
        out_shape=jax.ShapeDtypeStruct((ne, rows, dx), F32),
        compiler_params=_cparams(("arbitrary", "arbitrary")),
        name="moe_experts",
    )(idx, h_ext, wr_pad_bf16, w_gate, w_up, w_down)


def _combine_kernel(starts_ref, y_hbm, x_ref, g2_ref, *rest, first_tiles):
    if first_tiles is None:
        o_ref, acc, stage, sem = rest
    else:
        oc_ref, ol_ref, acc, stage, sem = rest
    tb = pl.program_id(0)
    tm, d = x_ref.shape
    ne = y_hbm.shape[0]

    @pl.when(tb == 0)
    def _():
        stage[...] = jnp.zeros(stage.shape, F32)

    def chunk_copy(e, src_row, dst_row):
        return pltpu.make_async_copy(y_hbm.at[e, pl.ds(src_row, 8)], stage.at[pl.ds(dst_row, 8)], sem)

    off = jnp.int32(0)
    for e in range(ne):
        s0 = starts_ref[e, tb]
        s1 = starts_ref[e, tb + 1]
        a = (s0 >> 3) << 3
        nchunk = jnp.where(s1 > s0, (s1 - a + 7) >> 3, 0)

        def issue(c, carry, e=e, a=a, off=off):
            chunk_copy(e, pl.multiple_of(a + 8 * c, 8), pl.multiple_of(off + 8 * c, 8)).start()
            return carry

        lax.fori_loop(0, nchunk, issue, 0)
        off = off + 8 * nchunk

    def drain(c, carry):
        chunk_copy(0, 0, 0).wait()
        return carry

    lax.fori_loop(0, off >> 3, drain, 0)

    acc[...] = jnp.zeros_like(acc)
    want = (lax.broadcasted_iota(jnp.int32, (tm, KCH), 0) + tb * tm).astype(F32)

    def fold(kc, carry):
        rows = stage[pl.ds(pl.multiple_of(kc * KCH, KCH), KCH), :]
        tok = rows[:, d:].T[0:1, :]
        fresh = lax.broadcasted_iota(jnp.int32, (tm, KCH), 1) + kc * KCH < off
        onehot = jnp.where((want == tok) & fresh, 1.0, 0.0).astype(BF16)
        y = rows[:, :d]
        hi = y.astype(BF16)
        lo = (y - hi.astype(F32)).astype(BF16)
        acc[...] += _dot(onehot, hi) + _dot(onehot, lo)
        return carry

    lax.fori_loop(0, (off + KCH - 1) >> 8, fold, 0)
    res = x_ref[...] + g2_ref[0] * acc[...]
    if first_tiles is None:
        o_ref[...] = res
    else:
        @pl.when(tb < first_tiles)
        def _():
            oc_ref[...] = res

        @pl.when(tb >= first_tiles)
        def _():
            ol_ref[...] = res


def moe_combine(starts, y, x, mod, rowmap, split=None):
    n_tok, d = x.shape
    ne, _, dx = y.shape
    tm = TOK_TILE
    stage_rows = -(-(ne * tm + ne * 16) // KCH) * KCH
    if split is None:
        out_specs = pl.BlockSpec((tm, d), lambda i, s: (i, 0))
        out_shape = jax.ShapeDtypeStruct((n_tok, d), F32)
        first_tiles = None
    else:
        first_tiles = split[0]
        out_specs = [pl.BlockSpec((tm, d), lambda i, s: (jnp.minimum(i, first_tiles - 1), 0)),
                     pl.BlockSpec((tm, d), lambda i, s: (jnp.maximum(i - first_tiles, 0), 0))]
        out_shape = [jax.ShapeDtypeStruct((split[0] * tm, d), F32), jax.ShapeDtypeStruct((split[1] * tm, d), F32)]
    return pl.pallas_call(
        functools.partial(_combine_kernel, first_tiles=first_tiles),
        grid_spec=pltpu.PrefetchScalarGridSpec(
            num_scalar_prefetch=1,
            grid=(n_tok // tm,),
            in_specs=[pl.BlockSpec(memory_space=pl.ANY),
                      pl.BlockSpec((tm, d), lambda i, s: (i, 0)),
                      pl.BlockSpec((1, 1, d), lambda i, s: (rowmap(i), 0, 5))],
            out_specs=out_specs,
            scratch_shapes=[pltpu.VMEM((tm, d), F32), pltpu.VMEM((stage_rows, dx), F32),
                            pltpu.SemaphoreType.DMA(())]),
        out_shape=out_shape,
        compiler_params=_cparams(("arbitrary",)),
        name="moe_combine",
    )(starts, y, x, mod)


def moe_layer(x, nw, mod, rowmap, groups, wr, w_gate, w_up, w_down, split=False):
    n_tok, d = x.shape
    ne = wr.shape[1]
    h_ext, aff_t = moe_router(x, nw, mod, rowmap, wr.T.astype(BF16))
    cidx_parts, meta_parts = [], []
    rows = 0
    for grp in groups:
        cap = EC_FACTOR * grp.tokens // ne
        t0 = grp.row0 // LANES
        cidx, meta = moe_select(aff_t[t0:t0 + grp.tokens // LANES], cap, grp.row0, rows)
        cidx_parts.append(cidx)
        meta_parts.append(meta[:, :, :2])
        rows += cap
    cidx = jnp.concatenate(cidx_parts, axis=0)
    meta = jnp.concatenate(meta_parts, axis=0)
    starts = meta[:, :, 0].T
    counts = meta[:, :, 1].T
    idx = moe_build_lists(starts, counts, cidx, rows)
    per = TOK_TILE // LANES
    starts_blk = jnp.concatenate([starts[:, ::per], jnp.full((ne, 1), rows, jnp.int32)], axis=1)
    wr_pad = jnp.zeros((d, LANES), BF16).at[:, :ne].set(wr.astype(BF16))
    y = moe_experts(idx, rows, h_ext, wr_pad, w_gate, w_up, w_down)
    return moe_combine(starts_blk, y, x, mod, rowmap, (groups[0].tiles, groups[1].tiles) if split else None)


def kernel(x_prompt, x_sample, cache_k, cache_v, state_hgrn, state_s5, c, c_ctx, norm_w, ada_w, ada_b, w_in_ab, w_out_ab, hgrn_lb_logits, hgrn_norm_w, q_norm_w, k_norm_w, attn_sink, s5_a_re, s5_a_im, s5_log_dt, s5_b_re, s5_b_im, s5_c_re, s5_c_im, s5_d, glu_w_a, glu_w_b, router_w, exp_w_gate, exp_w_up, exp_w_down):
    b_ctx, l_ctx, d = x_prompt.shape
    b_lat, l_lat, _ = x_sample.shape
    depth = norm_w.shape[0]
    ctx = Group(0, b_ctx, l_ctx)
    lat = Group(ctx.tokens, b_lat, l_lat)
    groups = (ctx, lat)

    def rowmap(i):
        return jnp.where(i < ctx.tiles, i // ctx.seq_tiles, b_ctx + (i - ctx.tiles) // lat.seq_tiles)

    cond = jnp.concatenate([c_ctx[None, :], c, jnp.zeros((8 - 1 - b_lat, d), F32)], axis=0)
    mod_small = ada_modulation(cond, ada_w, ada_b)
    seq_rows = jnp.concatenate([jnp.zeros((b_ctx,), jnp.int32), 1 + jnp.arange(b_lat, dtype=jnp.int32)])
    mods = mod_small[:, seq_rows][:, :, None, :]

    x = jnp.concatenate([x_prompt.reshape(ctx.tokens, d), x_sample.reshape(lat.tokens, d)], axis=0)
    rope = rope_tables(l_lat)
    ks, vs, hs, ss = [], [], [], []
    for l in range(depth):
        mod = mods[l]
        nw1 = norm_w[l, 0].reshape(1, d)
        nw2 = norm_w[l, 1].reshape(1, d)
        if l % 2 == 0:
            e = l // 2
            proj = norm_mod_matmul(x, nw1, mod, rowmap, 1, 0, w_in_ab[e].astype(BF16))
            zero_state = jnp.zeros((b_ctx, 2, A_HEADS, A_DK, A_DK), F32)
            of_c, ob_c, st_c = hgrn2_mixer(proj, ctx, hgrn_lb_logits, zero_state, e)
            of_l, ob_l, _ = hgrn2_mixer(proj, lat, hgrn_lb_logits, jnp.swapaxes(state_hgrn[:, e], -1, -2), e)
            hs.append(jnp.swapaxes(st_c, -1, -2))
            qn_c, kn_c = qk_prepare(proj, ctx, q_norm_w[e], k_norm_w[e], None)
            att_c = context_attention(qn_c, kn_c, proj, ctx, attn_sink[e])
            ks.append(kn_c.reshape(b_ctx, l_ctx, B_KV_HEADS, HEAD_DIM).transpose(0, 2, 1, 3))
            vs.append(proj[:ctx.tokens, V_COL:].reshape(b_ctx, l_ctx, B_KV_HEADS, HEAD_DIM).transpose(0, 2, 1, 3))
            qr_l, kr_l = qk_prepare(proj, lat, q_norm_w[e], k_norm_w[e], rope)
            att_l = latent_attention(qr_l, kr_l, proj, lat, cache_k[:, e], cache_v[:, e], attn_sink[e])
            x = even_out_proj((of_c, of_l), (ob_c, ob_l), (att_c, att_l), proj, x, mod, rowmap, groups,
                              hgrn_norm_w[e].reshape(1, A_DK), w_out_ab[e].astype(BF16))
        else:
            o = l // 2
            ops = s5_operators(s5_a_re[o], s5_a_im[o], s5_log_dt[o], s5_b_re[o], s5_b_im[o], s5_c_re[o], s5_c_im[o])
            zero_s5 = jnp.zeros((b_ctx, 2, 2, d // S5_GROUP, S5_STATE), F32)
            y_c, fin_c = s5_mixer(x, ctx, nw1, mod, rowmap, ops, zero_s5)
            y_l, _ = s5_mixer(x, lat, nw1, mod, rowmap, ops, state_s5[:, o])
            ss.append(fin_c)
            x = glu_residual((y_c, y_l), x, nw1, mod, rowmap, groups, s5_d[o].reshape(1, d),
                             glu_w_a[o].astype(BF16), glu_w_b[o].astype(BF16))
        x = moe_layer(x, nw2, mod, rowmap, groups, router_w[l], exp_w_gate[l], exp_w_up[l], exp_w_down[l],
                      split=(l == depth - 1))
    y_prompt = x[0].reshape(b_ctx, l_ctx, d)
    y_sample = x[1].reshape(b_lat, l_lat, d)
    return (y_prompt, y_sample, jnp.stack(ks, axis=1), jnp.stack(vs, axis=1),
            jnp.stack(hs, axis=1), jnp.stack(ss, axis=1))
```

```python
import functools

import jax
import jax.numpy as jnp
from jax import lax
from jax.experimental import pallas as pl
from jax.experimental.pallas import tpu as pltpu

F32 = jnp.float32
BF16 = jnp.bfloat16

A_HEADS = 4
A_DK = 128
A_WIDTH = A_HEADS * A_DK
B_HEADS = 8
B_KV_HEADS = 2
HEAD_DIM = 64
B_GROUP = B_HEADS // B_KV_HEADS
B_WIDTH = B_HEADS * HEAD_DIM
KV_WIDTH = B_KV_HEADS * HEAD_DIM
Q_COL = 5 * A_WIDTH
K_COL = Q_COL + B_WIDTH
V_COL = K_COL + KV_WIDTH
WINDOW = 128
GRID_W = 64
ROPE_THETA = 10000.0
S5_GROUP = 16
S5_STATE = 64
S5_CHUNK = 16
S5_TC = S5_CHUNK * S5_GROUP
N_EXPERTS = 16
EC_FACTOR = 2
EPS = 1e-6
NEG_INF = -1e30

HGRN_C = 128
HGRN_SB = 16
TOK_TILE = 256
LANES = 128
VMEM_LIMIT = 56 * 1024 * 1024
STATE_W = 2 * S5_STATE
KCH = 256
FFN_SPLIT = 4


def _cparams(sem):
    return pltpu.CompilerParams(dimension_semantics=sem, vmem_limit_bytes=VMEM_LIMIT)


def _sigmoid(x):
    return 1.0 / (1.0 + jnp.exp(-x))


def _silu(x):
    return x * _sigmoid(x)


def _norm_mod(x, nw, sc, sh):
    ms = jnp.mean(x * x, axis=-1, keepdims=True)
    return (x * lax.rsqrt(ms + EPS) * nw) * (1.0 + sc) + sh


def _dot(a, b):
    return jnp.dot(a, b, preferred_element_type=F32)


def _dot_nt(a, b):
    return lax.dot_general(a, b, (((1,), (1,)), ((), ())), preferred_element_type=F32)


def _split3(x):
    hi = x.astype(BF16)
    r1 = x - hi.astype(F32)
    mid = r1.astype(BF16)
    lo = (r1 - mid.astype(F32)).astype(BF16)
    return hi, mid, lo


class Group:
    def __init__(self, row0, bsz, seq):
        self.row0, self.bsz, self.seq = row0, bsz, seq
        self.tokens = bsz * seq
        self.tile0 = row0 // TOK_TILE
        self.tiles = self.tokens // TOK_TILE
        self.seq_tiles = seq // TOK_TILE


def _ada_kernel(c_ref, w_ref, b_ref, o_ref):
    s = _silu(c_ref[...])
    o_ref[0] = _dot(s.astype(BF16), w_ref[0].astype(BF16)) + b_ref[0]


def ada_modulation(cond, ada_w, ada_b):
    depth, d, n = ada_w.shape
    rows = cond.shape[0]
    tn = 1536
    return pl.pallas_call(
        _ada_kernel,
        grid=(depth, n // tn),
        in_specs=[pl.BlockSpec((rows, d), lambda l, j: (0, 0)),
                  pl.BlockSpec((1, d, tn), lambda l, j: (l, 0, j)),
                  pl.BlockSpec((1, 1, tn), lambda l, j: (l, 0, j))],
        out_specs=pl.BlockSpec((1, rows, tn), lambda l, j: (l, 0, j)),
        out_shape=jax.ShapeDtypeStruct((depth, rows, n), F32),
        compiler_params=_cparams(("arbitrary", "arbitrary")),
        name="ada_modulation",
    )(cond, ada_w, ada_b.reshape(depth, 1, n))


def _mod_spec(rowmap, k, d):
    return pl.BlockSpec((1, 1, d), lambda i: (rowmap(i), 0, k))


def _inproj_kernel(x_ref, nw_ref, sc_ref, sh_ref, w_ref, o_ref):
    h = _norm_mod(x_ref[...], nw_ref[...], sc_ref[0], sh_ref[0])
    o_ref[...] = _dot(h.astype(BF16), w_ref[...])


def norm_mod_matmul(x, nw, mod, rowmap, k_sc, k_sh, w_bf16):
    n_tok, d = x.shape
    n = w_bf16.shape[1]
    tm = TOK_TILE
    return pl.pallas_call(
        _inproj_kernel,
        grid=(n_tok // tm,),
        in_specs=[pl.BlockSpec((tm, d), lambda i: (i, 0)),
                  pl.BlockSpec((1, d), lambda i: (0, 0)),
                  _mod_spec(rowmap, k_sc, d), _mod_spec(rowmap, k_sh, d),
                  pl.BlockSpec((d, n), lambda i: (0, 0))],
        out_specs=pl.BlockSpec((tm, n), lambda i: (i, 0)),
        out_shape=jax.ShapeDtypeStruct((n_tok, n), F32),
        compiler_params=_cparams(("arbitrary",)),
        name="norm_mod_matmul",
    )(x, nw, mod, mod, w_bf16)


def _two_group_specs(groups, shape_of):
    first, second = groups
    return [pl.BlockSpec(shape_of, lambda i: (jnp.minimum(i, first.tiles - 1), 0)),
            pl.BlockSpec(shape_of, lambda i: (jnp.maximum(i - first.tiles, 0), 0))]


def _outproj_kernel(ofc_ref, ofl_ref, obc_ref, obl_ref, atc_ref, atl_ref, ga_ref, x_ref, g1_ref, hw_ref, w_ref,
                    o_ref, *, first_tiles):
    in_first = pl.program_id(0) < first_tiles
    o = jnp.where(in_first, ofc_ref[...] + obc_ref[...], ofl_ref[...] + obl_ref[...])
    o_att = jnp.where(in_first, atc_ref[...], atl_ref[...])
    gate = _silu(ga_ref[...])
    hw = hw_ref[...]
    parts = []
    for h in range(A_HEADS):
        sl = slice(h * A_DK, (h + 1) * A_DK)
        oh = o[:, sl]
        ms = jnp.mean(oh * oh, axis=-1, keepdims=True)
        parts.append(((oh * lax.rsqrt(ms + EPS) * hw) * gate[:, sl]).astype(BF16))
    parts.append(o_att.astype(BF16))
    y = _dot(jnp.concatenate(parts, axis=1), w_ref[...])
    o_ref[...] = x_ref[...] + g1_ref[0] * y


def even_out_proj(o_f, o_b, o_att, proj, x, mod, rowmap, groups, hw, w_bf16):
    n_tok, d = x.shape
    tm = TOK_TILE
    aw = A_WIDTH
    return pl.pallas_call(
        functools.partial(_outproj_kernel, first_tiles=groups[0].tiles),
        grid=(n_tok // tm,),
        in_specs=_two_group_specs(groups, (tm, aw)) + _two_group_specs(groups, (tm, aw))
                 + _two_group_specs(groups, (tm, B_WIDTH))
                 + [pl.BlockSpec((tm, aw), lambda i: (i, 4)),
                    pl.BlockSpec((tm, d), lambda i: (i, 0)),
                    _mod_spec(rowmap, 2, d),
                    pl.BlockSpec((1, A_DK), lambda i: (0, 0)),
                    pl.BlockSpec((aw + B_WIDTH, d), lambda i: (0, 0))],
        out_specs=pl.BlockSpec((tm, d), lambda i: (i, 0)),
        out_shape=jax.ShapeDtypeStruct((n_tok, d), F32),
        compiler_params=_cparams(("arbitrary",)),
        name="even_out_proj",
    )(o_f[0], o_f[1], o_b[0], o_b[1], o_att[0], o_att[1], proj, x, mod, hw, w_bf16)


def _hgrn_chunk(q, k, v, g, st, msel, rev):
    c = q.shape[0]
    nb = c // HGRN_SB
    row = lax.broadcasted_iota(jnp.int32, (c, c), 0)
    col = lax.broadcasted_iota(jnp.int32, (c, c), 1)
    tri = jnp.where((col >= row) if rev else (col <= row), 1.0, 0.0).astype(BF16)
    gh, gm, gl = _split3(g)
    b = _dot(tri, gh) + _dot(tri, gm) + _dot(tri, gl)
    b_edge = b[0:1] if rev else b[c - 1:c]
    qs = q * jnp.exp(b)
    kdec = k * jnp.exp(b_edge - b)

    lk = jnp.log(jnp.maximum(k, 0.0))
    half = HGRN_SB // 2
    zero_half = jnp.zeros((half, A_DK), F32)
    slabs = []
    for i in range(nb):
        sl = slice(i * HGRN_SB, (i + 1) * HGRN_SB)
        bi, qi, lki = b[sl], q[sl], lk[sl]
        ci = bi - lki
        pieces = []
        for s in range(HGRN_SB):
            s_half = s // half
            halves = []
            for hh in range(2):
                rows = slice(hh * half, (hh + 1) * half)
                if (hh > s_half) if rev else (hh < s_half):
                    halves.append(zero_half)
                    continue
                d = bi[rows] - ci[s:s + 1]
                if hh == s_half:
                    d = jnp.minimum(d, lki[s:s + 1])
                halves.append(qi[rows] * jnp.exp(d))
            pieces.append(jnp.concatenate(halves, axis=0).astype(BF16))
        slabs.append(jnp.concatenate(pieces, axis=1))
    a_loc = _dot(jnp.concatenate(slabs, axis=0), msel)

    lane = lax.broadcasted_iota(jnp.int32, (HGRN_SB, LANES), 1)
    rloc = lax.broadcasted_iota(jnp.int32, (HGRN_SB, LANES), 0)
    dmask = ((lane >= rloc) & (lane < HGRN_SB)) if rev else (lane <= rloc)
    krow = lax.broadcasted_iota(jnp.int32, (c, A_DK), 0)
    att_rows = []
    for i in range(nb):
        sl = slice(i * HGRN_SB, (i + 1) * HGRN_SB)
        a_d = jnp.where(dmask, a_loc[sl], 0.0)
        if i > 0:
            a_d = pltpu.roll(a_d, i * HGRN_SB, axis=1)
        a_i = a_d[:, :c]
        has_off = (i < nb - 1) if rev else (i > 0)
        if has_off:
            edge = (i + 1) * HGRN_SB if rev else i * HGRN_SB
            r = b[edge:edge + 1] if rev else b[edge - 1:edge]
            qp = q[sl] * jnp.exp(b[sl] - r)
            live = (krow >= edge) if rev else (krow < edge)
            kp = jnp.where(live, k * jnp.exp(jnp.minimum(r - b, 0.0)), 0.0)
            a_i = a_i + _dot_nt(qp.astype(BF16), kp.astype(BF16))
        att_rows.append(a_i)
    att = jnp.concatenate(att_rows, axis=0)

    vb = v.astype(BF16)
    o = _dot(att.astype(BF16), vb) + _dot_nt(qs.astype(BF16), st.astype(BF16))
    st_new = st * jnp.exp(b_edge) + _dot(v.T.astype(BF16), kdec.astype(BF16))
    return o, st_new


def _hgrn_kernel(qf_ref, vf_ref, ff_ref, qb_ref, vb_ref, fb_ref, lbl_ref, msel_ref, s0_ref,
                 of_ref, ob_ref, sout_ref, st_ref, *, layer):
    c_idx = pl.program_id(1)

    @pl.when(c_idx == 0)
    def _():
        st_ref[...] = s0_ref[0]

    lg = lbl_ref[...]
    ex = jnp.exp(lg - jnp.max(lg, axis=0, keepdims=True))
    pr = ex / jnp.sum(ex, axis=0, keepdims=True)
    lb = jnp.zeros_like(pr[0])
    for e in range(1, layer + 1):
        lb = lb + pr[e]
    msel = msel_ref[...]

    for d, (q_ref, v_ref, f_ref, o_ref) in enumerate(((qf_ref, vf_ref, ff_ref, of_ref),
                                                       (qb_ref, vb_ref, fb_ref, ob_ref))):
        q_all = _silu(q_ref[...])
        v_all = v_ref[...]
        lbd = lb[d:d + 1]
        forget = lbd + (1.0 - lbd) * _sigmoid(f_ref[...])
        k_all = 1.0 - forget
        g_all = jnp.log(forget)
        for h in range(A_HEADS):
            sl = slice(h * A_DK, (h + 1) * A_DK)
            o, st_new = _hgrn_chunk(q_all[:, sl], k_all[:, sl], v_all[:, sl], g_all[:, sl],
                                    st_ref[d, h], msel, rev=(d == 1))
            o_ref[:, sl] = o
            st_ref[d, h] = st_new

    @pl.when(c_idx == pl.num_programs(1) - 1)
    def _():
        sout_ref[0] = st_ref[...]


def hgrn2_mixer(proj, grp, lb_logits, s0t, layer):
    c = HGRN_C
    nc = grp.seq // c
    blk0 = grp.row0 // c
    aw = A_WIDTH
    msel = jnp.repeat(jnp.eye(HGRN_SB, LANES, dtype=BF16), A_DK, axis=0)

    def fwd(col):
        return pl.BlockSpec((c, aw), lambda b, i: (blk0 + b * nc + i, col))

    def bwd(col):
        return pl.BlockSpec((c, aw), lambda b, i: (blk0 + b * nc + nc - 1 - i, col))

    st_spec = pl.BlockSpec((1, 2, A_HEADS, A_DK, A_DK), lambda b, i: (b, 0, 0, 0, 0))
    return pl.pallas_call(
        functools.partial(_hgrn_kernel, layer=layer),
        grid=(grp.bsz, nc),
        in_specs=[fwd(0), fwd(3), fwd(1), bwd(0), bwd(3), bwd(2),
                  pl.BlockSpec(lb_logits.shape, lambda b, i: (0, 0, 0)),
                  pl.BlockSpec(msel.shape, lambda b, i: (0, 0)),
                  st_spec],
        out_specs=[pl.BlockSpec((c, aw), lambda b, i: (b * nc + i, 0)),
                   pl.BlockSpec((c, aw), lambda b, i: (b * nc + nc - 1 - i, 0)),
                   st_spec],
        out_shape=[jax.ShapeDtypeStruct((grp.tokens, aw), F32),
                   jax.ShapeDtypeStruct((grp.tokens, aw), F32),
                   jax.ShapeDtypeStruct((grp.bsz, 2, A_HEADS, A_DK, A_DK), F32)],
        scratch_shapes=[pltpu.VMEM((2, A_HEADS, A_DK, A_DK), F32)],
        compiler_params=_cparams(("arbitrary", "arbitrary")),
        name="hgrn2_mixer",
    )(proj, proj, proj, proj, proj, proj, lb_logits, msel, s0t)


def _head_norm(x, w, gmat):
    hi, mid, lo = _split3(x * x)
    ms = _dot(hi, gmat) + _dot(mid, gmat) + _dot(lo, gmat)
    return x * lax.rsqrt(ms + EPS) * w


def _rope(x, cos, sin_signed):
    width = x.shape[1]
    lane = lax.broadcasted_iota(jnp.int32, x.shape, 1)
    nxt = pltpu.roll(x, width - 1, axis=1)
    prv = pltpu.roll(x, 1, axis=1)
    partner = jnp.where(lane % 2 == 0, nxt, prv)
    return x * cos + partner * sin_signed


def _qkprep_kernel(*refs, rope):
    if rope:
        q_ref, k_ref, qw_ref, kw_ref, gm_ref, cos_ref, sin_ref, qo_ref, ko_ref = refs
    else:
        q_ref, k_ref, qw_ref, kw_ref, gm_ref, qo_ref, ko_ref = refs
    gm = gm_ref[...]
    qn = _head_norm(q_ref[...], qw_ref[...], gm)
    kn = _head_norm(k_ref[...], kw_ref[...], gm[:KV_WIDTH, :KV_WIDTH])
    if rope:
        cos = cos_ref[...]
        sin = sin_ref[...]
        qn = _rope(qn, cos, sin)
        kn = _rope(kn, cos[:, :KV_WIDTH], sin[:, :KV_WIDTH])
    qo_ref[...] = (qn * (HEAD_DIM ** -0.5)).astype(BF16)
    ko_ref[...] = kn


def qk_prepare(proj, grp, qw, kw, rope_tabs):
    tm = TOK_TILE
    st = grp.seq_tiles
    gidx = jnp.arange(B_WIDTH) // HEAD_DIM
    gmat = jnp.where(gidx[:, None] == gidx[None, :], 1.0 / HEAD_DIM, 0.0).astype(BF16)
    qw_t = jnp.tile(qw, B_HEADS).reshape(1, B_WIDTH)
    kw_t = jnp.tile(kw, B_KV_HEADS).reshape(1, KV_WIDTH)
    in_specs = [pl.BlockSpec((tm, B_WIDTH), lambda i: (grp.tile0 + i, Q_COL // B_WIDTH)),
                pl.BlockSpec((tm, KV_WIDTH), lambda i: (grp.tile0 + i, K_COL // KV_WIDTH)),
                pl.BlockSpec((1, B_WIDTH), lambda i: (0, 0)),
                pl.BlockSpec((1, KV_WIDTH), lambda i: (0, 0)),
                pl.BlockSpec((B_WIDTH, B_WIDTH), lambda i: (0, 0))]
    args = [proj, proj, qw_t, kw_t, gmat]
    if rope_tabs is not None:
        in_specs += [pl.BlockSpec((tm, B_WIDTH), lambda i: (i % st, 0)),
                     pl.BlockSpec((tm, B_WIDTH), lambda i: (i % st, 0))]
        args += list(rope_tabs)
    return pl.pallas_call(
        functools.partial(_qkprep_kernel, rope=rope_tabs is not None),
        grid=(grp.tiles,),
        in_specs=in_specs,
        out_specs=[pl.BlockSpec((tm, B_WIDTH), lambda i: (i, 0)),
                   pl.BlockSpec((tm, KV_WIDTH), lambda i: (i, 0))],
        out_shape=[jax.ShapeDtypeStruct((grp.tokens, B_WIDTH), BF16),
                   jax.ShapeDtypeStruct((grp.tokens, KV_WIDTH), F32)],
        compiler_params=_cparams(("arbitrary",)),
        name="qk_prepare",
    )(*args)


def rope_tables(seq):
    pos = jnp.arange(seq)
    row = (pos // GRID_W).astype(F32)
    col = (pos % GRID_W).astype(F32)
    n_pair = HEAD_DIM // 4
    freqs = ROPE_THETA ** (-jnp.arange(n_pair, dtype=F32) / n_pair)
    ang = jnp.concatenate([row[:, None] * freqs, col[:, None] * freqs], axis=-1)
    cos = jnp.repeat(jnp.cos(ang), 2, axis=-1)
    sin = jnp.repeat(jnp.sin(ang), 2, axis=-1) * jnp.tile(jnp.array([-1.0, 1.0], F32), HEAD_DIM // 2)
    return jnp.tile(cos, (1, B_HEADS)), jnp.tile(sin, (1, B_HEADS))


def _value_with_ones(v2, kvh):
    lane = lax.broadcasted_iota(jnp.int32, v2.shape, 1)
    if kvh == 1:
        v2 = pltpu.roll(v2, HEAD_DIM, axis=1)
    return jnp.where(lane < HEAD_DIM, v2, 1.0).astype(BF16)


def _group_attention(q, kk, vv1, valid, sink_ref, kvh, o_ref):
    tq = q.shape[0]
    heads = [kvh * B_GROUP + gq for gq in range(B_GROUP)]
    qs = jnp.concatenate([q[:, h * HEAD_DIM:(h + 1) * HEAD_DIM] for h in heads], axis=0)
    sink = jnp.concatenate([jnp.broadcast_to(sink_ref[h:h + 1, 0:1], (tq, 1)) for h in heads], axis=0)
    s = _dot_nt(qs, kk)
    if valid is not None:
        s = jnp.where(jnp.concatenate([valid] * B_GROUP, axis=0), s, NEG_INF)
    m = jnp.maximum(jnp.max(s, axis=1, keepdims=True), sink)
    pv = _dot(jnp.exp(s - m).astype(BF16), vv1)
    den = pv[:, HEAD_DIM:HEAD_DIM + 1] + jnp.exp(sink - m)
    o = pv[:, :HEAD_DIM] / den
    for gq, h in enumerate(heads):
        o_ref[:, h * HEAD_DIM:(h + 1) * HEAD_DIM] = o[gq * tq:(gq + 1) * tq]


def _ctx_attn_kernel(q_ref, k_ref, v_ref, sink_ref, o_ref):
    q = q_ref[...]
    k = k_ref[...].astype(BF16)
    v = v_ref[...]
    for kvh in range(B_KV_HEADS):
        ks = slice(kvh * HEAD_DIM, (kvh + 1) * HEAD_DIM)
        _group_attention(q, k[:, ks], _value_with_ones(v, kvh), None, sink_ref, kvh, o_ref)


def context_attention(qn, kn, proj, grp, sink):
    seq = grp.seq
    blk0 = grp.row0 // seq
    sink_t = jnp.broadcast_to(sink.reshape(B_HEADS, 1), (B_HEADS, LANES))
    return pl.pallas_call(
        _ctx_attn_kernel,
        grid=(grp.bsz,),
        in_specs=[pl.BlockSpec((seq, B_WIDTH), lambda b: (b, 0)),
                  pl.BlockSpec((seq, KV_WIDTH), lambda b: (b, 0)),
                  pl.BlockSpec((seq, KV_WIDTH), lambda b: (blk0 + b, V_COL // KV_WIDTH)),
                  pl.BlockSpec((B_HEADS, LANES), lambda b: (0, 0))],
        out_specs=pl.BlockSpec((seq, B_WIDTH), lambda b: (b, 0)),
        out_shape=jax.ShapeDtypeStruct((grp.tokens, B_WIDTH), F32),
        compiler_params=_cparams(("arbitrary",)),
        name="context_attention",
    )(qn, kn, proj, sink_t)


def _lat_attn_kernel(q_ref, kp_ref, kc_ref, kn_ref, vp_ref, vc_ref, vn_ref, kx_ref, vx_ref, sink_ref, o_ref):
    blk = pl.program_id(1)
    nblk = pl.num_programs(1)
    tq = q_ref.shape[0]
    q = q_ref[...]
    kl = jnp.concatenate([kp_ref[...], kc_ref[...], kn_ref[...]], axis=0).astype(BF16)
    vl = jnp.concatenate([vp_ref[...], vc_ref[...], vn_ref[...]], axis=0)
    n_ctx = kx_ref.shape[2]
    span = 3 * tq
    i = lax.broadcasted_iota(jnp.int32, (tq, span + n_ctx), 0)
    j = lax.broadcasted_iota(jnp.int32, (tq, span + n_ctx), 1)
    dist = j - tq - i
    valid = (dist >= -WINDOW) & (dist <= WINDOW)
    valid = valid & ((j >= tq) | (blk > 0)) & ((j < 2 * tq) | (blk < nblk - 1))
    valid = valid | (j >= span)
    for kvh in range(B_KV_HEADS):
        ks = slice(kvh * HEAD_DIM, (kvh + 1) * HEAD_DIM)
        kk = jnp.concatenate([kl[:, ks], kx_ref[0, kvh].astype(BF16)], axis=0)
        vv1 = jnp.concatenate([_value_with_ones(vl, kvh), vx_ref[0, kvh].astype(BF16)], axis=0)
        _group_attention(q, kk, vv1, valid, sink_ref, kvh, o_ref)


def latent_attention(qr, kr, proj, grp, k_ctx, v_ctx, sink):
    tq = WINDOW
    nblk = grp.seq // tq
    blk0 = grp.row0 // tq
    n_ctx = k_ctx.shape[2]
    sink_t = jnp.broadcast_to(sink.reshape(B_HEADS, 1), (B_HEADS, LANES))
    v_ctx1 = jnp.concatenate([v_ctx, jnp.ones_like(v_ctx)], axis=-1)

    def kv_specs(off, col):
        return [pl.BlockSpec((tq, KV_WIDTH), lambda b, i: (off + b * nblk + jnp.maximum(i - 1, 0), col)),
                pl.BlockSpec((tq, KV_WIDTH), lambda b, i: (off + b * nblk + i, col)),
                pl.BlockSpec((tq, KV_WIDTH), lambda b, i: (off + b * nblk + jnp.minimum(i + 1, nblk - 1), col))]

    return pl.pallas_call(
        _lat_attn_kernel,
        grid=(grp.bsz, nblk),
        in_specs=[pl.BlockSpec((tq, B_WIDTH), lambda b, i: (b * nblk + i, 0))]
                 + kv_specs(0, 0) + kv_specs(blk0, V_COL // KV_WIDTH)
                 + [pl.BlockSpec((1, B_KV_HEADS, n_ctx, HEAD_DIM), lambda b, i: (b, 0, 0, 0)),
                    pl.BlockSpec((1, B_KV_HEADS, n_ctx, 2 * HEAD_DIM), lambda b, i: (b, 0, 0, 0)),
                    pl.BlockSpec((B_HEADS, LANES), lambda b, i: (0, 0))],
        out_specs=pl.BlockSpec((tq, B_WIDTH), lambda b, i: (b * nblk + i, 0)),
        out_shape=jax.ShapeDtypeStruct((grp.tokens, B_WIDTH), F32),
        compiler_params=_cparams(("arbitrary", "arbitrary")),
        name="latent_attention",
    )(qr, kr, kr, kr, proj, proj, proj, k_ctx, v_ctx1, sink_t)


def s5_operators(a_re, a_im, log_dt, b_re, b_im, c_re, c_im):
    t = S5_CHUNK
    hi = lax.Precision.HIGHEST
    ks, ws, wsw, vs, a1s, a2s = [], [], [], [], [], []
    for d in range(2):
        are, aim = a_re[d].astype(F32), a_im[d].astype(F32)
        dt = jnp.exp(log_dt[d].astype(F32))[:, None]
        den = are * are + aim * aim
        steps = jnp.arange(t + 1, dtype=F32)[:, None, None]
        mag = jnp.exp(steps * (dt * are))
        pw_re = mag * jnp.cos(steps * (dt * aim))
        pw_im = mag * jnp.sin(steps * (dt * aim))
        ab_re, ab_im = pw_re[1], pw_im[1]
        f_re = ((ab_re - 1.0) * are + ab_im * aim) / den
        f_im = (ab_im * are - (ab_re - 1.0) * aim) / den
        bre, bim = b_re[d].astype(F32), b_im[d].astype(F32)
        bb_re = f_re[..., None] * bre - f_im[..., None] * bim
        bb_im = f_re[..., None] * bim + f_im[..., None] * bre
        cre, cim = c_re[d].astype(F32), c_im[d].astype(F32)
        pgr = pw_re.transpose(1, 0, 2)[:, :, None, :]
        pgi = pw_im.transpose(1, 0, 2)[:, :, None, :]
        cp_re = cre[:, None] * pgr - cim[:, None] * pgi
        cp_im = cre[:, None] * pgi + cim[:, None] * pgr
        m = (jnp.einsum('gkcp,gpd->gkcd', cp_re[:, :t], bb_re, precision=hi)
             - jnp.einsum('gkcp,gpd->gkcd', cp_im[:, :t], bb_im, precision=hi))
        s_i = jnp.arange(t)[:, None]
        t_i = jnp.arange(t)[None, :]
        lag = (t_i - s_i) if d == 0 else (s_i - t_i)
        blk = jnp.where((lag >= 0)[None, :, :, None, None], m[:, jnp.clip(lag, 0, t - 1)], 0.0)
        ks.append(blk.transpose(0, 1, 4, 2, 3).reshape(-1, S5_TC, S5_TC))
        pidx = (t - 1 - jnp.arange(t)) if d == 0 else jnp.arange(t)
        pr = pw_re[pidx].transpose(1, 0, 2)[:, :, None, :]
        pi = pw_im[pidx].transpose(1, 0, 2)[:, :, None, :]
        bbr = bb_re.transpose(0, 2, 1)[:, None]
        bbi = bb_im.transpose(0, 2, 1)[:, None]
        w_re = pr * bbr - pi * bbi
        w_im = pr * bbi + pi * bbr
        ws.append(jnp.concatenate([w_re, w_im], axis=-1).reshape(-1, S5_TC, 2 * S5_STATE))
        wsw.append(jnp.concatenate([w_im, w_re], axis=-1).reshape(-1, S5_TC, 2 * S5_STATE))
        kidx = (jnp.arange(t) + 1) if d == 0 else (t - jnp.arange(t))
        v = jnp.concatenate([cp_re[:, kidx], -cp_im[:, kidx]], axis=-1)
        vs.append(v.transpose(0, 3, 1, 2).reshape(-1, 2 * S5_STATE, S5_TC))
        a1s.append(jnp.concatenate([pw_re[t], pw_re[t]], axis=-1))
        a2s.append(jnp.concatenate([-pw_im[t], pw_im[t]], axis=-1))
    return (ks[0] + ks[1], jnp.concatenate(ws + wsw, axis=-1), jnp.concatenate(vs, axis=1),
            jnp.concatenate(a1s, axis=-1), jnp.concatenate(a2s, axis=-1))


def _to_chunks_kernel(x_ref, nw_ref, sc_ref, sh_ref, o_ref, hbuf):
    h = _norm_mod(x_ref[...], nw_ref[...], sc_ref[0], sh_ref[0])
    gpl = LANES // S5_GROUP
    for c in range(hbuf.shape[0]):
        hbuf[c] = h[:, c * LANES:(c + 1) * LANES]
    for c in range(hbuf.shape[0]):
        for t in range(S5_CHUNK):
            rows = hbuf[c, pl.ds(t, TOK_TILE // S5_CHUNK, stride=S5_CHUNK), :]
            for g in range(gpl):
                o_ref[c * gpl + g, :, t * S5_GROUP:(t + 1) * S5_GROUP] = rows[:, g * S5_GROUP:(g + 1) * S5_GROUP]


def s5_to_chunks(x, grp, nw, mod, rowmap):
    d = x.shape[1]
    ng = d // S5_GROUP
    tm = TOK_TILE
    cpt = tm // S5_CHUNK
    st = grp.seq_tiles

    def mspec(k):
        return pl.BlockSpec((1, 1, d), lambda i: (rowmap(grp.tile0 + i), 0, k))

    return pl.pallas_call(
        _to_chunks_kernel,
        grid=(grp.tiles,),
        in_specs=[pl.BlockSpec((tm, d), lambda i: (grp.tile0 + i, 0)),
                  pl.BlockSpec((1, d), lambda i: (0, 0)), mspec(1), mspec(0)],
        out_specs=pl.BlockSpec((ng, cpt, S5_TC), lambda i: (0, i % st, i // st)),
        out_shape=jax.ShapeDtypeStruct((ng, grp.seq // S5_CHUNK, grp.bsz * S5_TC), F32),
        scratch_shapes=[pltpu.VMEM((d // LANES, tm, LANES), F32)],
        compiler_params=_cparams(("arbitrary",)),
        name="s5_to_chunks",
    )(x, nw, mod, mod)


def _s5_states_kernel(uf_ref, ub_ref, w_ref, a1_ref, a2_ref, s0_ref, stf_ref, stb_ref, fin_ref, dbuf, carry, *, bsz):
    r = pl.program_id(1)
    gb, tr, _ = uf_ref.shape
    nc = tr // bsz
    sw = STATE_W

    @pl.when(r == 0)
    def _():
        for g in range(gb):
            s0 = s0_ref[g]
            for d in range(2):
                s = s0[:, d * sw:(d + 1) * sw]
                carry[g, 2 * d] = s
                carry[g, 2 * d + 1] = pltpu.roll(s, sw // 2, axis=1)

    for g in range(gb):
        w = w_ref[g]
        dbuf[g, 0] = _dot(uf_ref[g].astype(BF16), w[:, :2 * sw])
        dbuf[g, 1] = _dot(ub_ref[g].astype(BF16), w[:, 2 * sw:])

    for g in range(gb):
        a1 = a1_ref[g]
        a2 = a2_ref[g]
        for d, st_ref in enumerate((stf_ref, stb_ref)):
            a1d = a1[:, d * sw:(d + 1) * sw]
            a2d = a2[:, d * sw:(d + 1) * sw]
            s = carry[g, 2 * d]
            x = carry[g, 2 * d + 1]
            for c in range(nc):
                rows = (c if d == 0 else nc - 1 - c) * bsz
                st_ref[rows:rows + bsz, g * sw:(g + 1) * sw] = s
                dd = dbuf[g, d, rows:rows + bsz, :]
                s, x = a1d * s + a2d * x + dd[:, :sw], a1d * x - a2d * s + dd[:, sw:]
            carry[g, 2 * d] = s
            carry[g, 2 * d + 1] = x

    @pl.when(r == pl.num_programs(1) - 1)
    def _():
        for g in range(gb):
            fin_ref[g, :, :sw] = carry[g, 0]
            fin_ref[g, :, sw:] = carry[g, 2]


def s5_chunk_states_scan(u, w_bf16, a1, a2, s0g, bsz):
    g, r, tc = u.shape
    n = w_bf16.shape[2]
    sw = STATE_W
    gb = 8
    tr = min(r, 256)
    nblk = r // tr
    return pl.pallas_call(
        functools.partial(_s5_states_kernel, bsz=bsz),
        grid=(g // gb, nblk),
        in_specs=[pl.BlockSpec((gb, tr, tc), lambda i, j: (i, j, 0)),
                  pl.BlockSpec((gb, tr, tc), lambda i, j: (i, nblk - 1 - j, 0)),
                  pl.BlockSpec((gb, tc, n), lambda i, j: (i, 0, 0)),
                  pl.BlockSpec((gb, 1, 2 * sw), lambda i, j: (i, 0, 0)),
                  pl.BlockSpec((gb, 1, 2 * sw), lambda i, j: (i, 0, 0)),
                  pl.BlockSpec((gb, bsz, 2 * sw), lambda i, j: (i, 0, 0))],
        out_specs=[pl.BlockSpec((tr, gb * sw), lambda i, j: (j, i)),
                   pl.BlockSpec((tr, gb * sw), lambda i, j: (nblk - 1 - j, i)),
                   pl.BlockSpec((gb, bsz, 2 * sw), lambda i, j: (i, 0, 0))],
        out_shape=[jax.ShapeDtypeStruct((r, g * sw), F32),
                   jax.ShapeDtypeStruct((r, g * sw), F32),
                   jax.ShapeDtypeStruct((g, bsz, 2 * sw), F32)],
        scratch_shapes=[pltpu.VMEM((gb, 2, tr, 2 * sw), F32), pltpu.VMEM((gb, 4, bsz, sw), F32)],
        compiler_params=_cparams(("arbitrary", "arbitrary")),
        name="s5_chunk_states_scan",
    )(u, u, w_bf16, a1.reshape(g, 1, 2 * sw), a2.reshape(g, 1, 2 * sw), s0g)


def _s5_out_kernel(u_ref, sf_ref, sb_ref, k_ref, v_ref, o_ref):
    gb = u_ref.shape[0]
    sw = STATE_W
    for g in range(gb):
        s = jnp.concatenate([sf_ref[:, g * sw:(g + 1) * sw], sb_ref[:, g * sw:(g + 1) * sw]], axis=1)
        o_ref[g] = _dot(u_ref[g].astype(BF16), k_ref[g]) + _dot(s.astype(BF16), v_ref[g])


def s5_outputs(u, st_f, st_b, k_bf16, v_bf16):
    g, r, tc = u.shape
    sw = STATE_W
    gb = 8
    tr = min(r, 512)
    return pl.pallas_call(
        _s5_out_kernel,
        grid=(g // gb, r // tr),
        in_specs=[pl.BlockSpec((gb, tr, tc), lambda i, j: (i, j, 0)),
                  pl.BlockSpec((tr, gb * sw), lambda i, j: (j, i)),
                  pl.BlockSpec((tr, gb * sw), lambda i, j: (j, i)),
                  pl.BlockSpec((gb, tc, tc), lambda i, j: (i, 0, 0)),
                  pl.BlockSpec((gb, 2 * sw, tc), lambda i, j: (i, 0, 0))],
        out_specs=pl.BlockSpec((gb, tr, tc), lambda i, j: (i, j, 0)),
        out_shape=jax.ShapeDtypeStruct((g, r, tc), F32),
        compiler_params=_cparams(("arbitrary", "arbitrary")),
        name="s5_outputs",
    )(u, st_f, st_b, k_bf16, v_bf16)


def s5_mixer(x, grp, nw, mod, rowmap, ops, s0):
    k_tot, w_tot, v_tot, a1, a2 = ops
    ng = k_tot.shape[0]
    bsz = grp.bsz
    nj = grp.seq // S5_CHUNK
    sw = STATE_W
    u4 = s5_to_chunks(x, grp, nw, mod, rowmap)
    u = u4.reshape(ng, nj * bsz, S5_TC)
    w = jnp.concatenate([w_tot[:, :, :sw], w_tot[:, :, 2 * sw:3 * sw], w_tot[:, :, sw:2 * sw], w_tot[:, :, 3 * sw:]], axis=-1)
    s0g = s0.transpose(3, 0, 1, 2, 4).reshape(ng, bsz, 2 * sw)
    st_f, st_b, final = s5_chunk_states_scan(u, w.astype(BF16), a1, a2, s0g, bsz)
    y = s5_outputs(u, st_f, st_b, k_tot.astype(BF16), v_tot.astype(BF16))
    final = final.reshape(ng, bsz, 2, 2, S5_STATE).transpose(1, 2, 3, 0, 4)
    return y.reshape(ng, nj, bsz * S5_TC), final


def _glu_kernel(yc_ref, yl_ref, x_ref, nw_ref, sc_ref, sh_ref, g1_ref, dsk_ref, wa_ref, wb_ref, o_ref, ybuf,
                *, first_tiles):
    gpl = LANES // S5_GROUP

    def from_chunks(y_ref):
        cpt = y_ref.shape[1]
        for c in range(ybuf.shape[0]):
            for t in range(S5_CHUNK):
                for g in range(gpl):
                    ybuf[c, t * cpt:(t + 1) * cpt, g * S5_GROUP:(g + 1) * S5_GROUP] = (
                        y_ref[c * gpl + g, :, t * S5_GROUP:(t + 1) * S5_GROUP])

    in_first = pl.program_id(0) < first_tiles

    @pl.when(in_first)
    def _():
        from_chunks(yc_ref)

    @pl.when(jnp.logical_not(in_first))
    def _():
        from_chunks(yl_ref)

    cpt = TOK_TILE // S5_CHUNK
    y = jnp.concatenate(
        [jnp.concatenate([ybuf[c, pl.ds(j, S5_CHUNK, stride=cpt), :] for c in range(ybuf.shape[0])], axis=1)
         for j in range(cpt)], axis=0)
    x = x_ref[...]
    y = y + dsk_ref[...] * _norm_mod(x, nw_ref[...], sc_ref[0], sh_ref[0])
    yb = jax.nn.gelu(y, approximate=True).astype(BF16)
    a = _dot(yb, wa_ref[...])
    b = _dot(yb, wb_ref[...])
    o_ref[...] = x + g1_ref[0] * (a * _sigmoid(b))


def glu_residual(y_chunks, x, nw, mod, rowmap, groups, dskip, wa_bf16, wb_bf16):
    n_tok, d = x.shape
    tm = TOK_TILE
    ng = d // S5_GROUP
    cpt = tm // S5_CHUNK
    first, second = groups
    blk = (ng, cpt, S5_TC)
    return pl.pallas_call(
        functools.partial(_glu_kernel, first_tiles=first.tiles),
        grid=(n_tok // tm,),
        in_specs=[pl.BlockSpec(blk, lambda i: (0, jnp.minimum(i, first.tiles - 1) % first.seq_tiles,
                                               jnp.minimum(i, first.tiles - 1) // first.seq_tiles)),
                  pl.BlockSpec(blk, lambda i: (0, jnp.maximum(i - first.tiles, 0) % second.seq_tiles,
                                               jnp.maximum(i - first.tiles, 0) // second.seq_tiles)),
                  pl.BlockSpec((tm, d), lambda i: (i, 0)),
                  pl.BlockSpec((1, d), lambda i: (0, 0)),
                  _mod_spec(rowmap, 1, d), _mod_spec(rowmap, 0, d), _mod_spec(rowmap, 2, d),
                  pl.BlockSpec((1, d), lambda i: (0, 0)),
                  pl.BlockSpec((d, d), lambda i: (0, 0)),
                  pl.BlockSpec((d, d), lambda i: (0, 0))],
        out_specs=pl.BlockSpec((tm, d), lambda i: (i, 0)),
        out_shape=jax.ShapeDtypeStruct((n_tok, d), F32),
        scratch_shapes=[pltpu.VMEM((d // LANES, tm, LANES), F32)],
        compiler_params=_cparams(("arbitrary",)),
        name="glu_residual",
    )(y_chunks[0], y_chunks[1], x, nw, mod, mod, mod, dskip, wa_bf16, wb_bf16)


def _router_kernel(x_ref, nw_ref, sc_ref, sh_ref, wr_ref, h_ref, aff_ref):
    d = x_ref.shape[1]
    tm = x_ref.shape[0]
    h = _norm_mod(x_ref[...], nw_ref[...], sc_ref[0], sh_ref[0])
    h_ref[:, :d] = h
    tok = pl.program_id(0) * tm + lax.broadcasted_iota(jnp.int32, (tm, LANES), 0)
    h_ref[:, d:] = tok.astype(F32)
    logits = _dot_nt(wr_ref[...], h.astype(BF16))
    ex = jnp.exp(logits - jnp.max(logits, axis=0, keepdims=True))
    p = ex / jnp.sum(ex, axis=0, keepdims=True)
    for k in range(aff_ref.shape[0]):
        aff_ref[k] = p[:, k * LANES:(k + 1) * LANES]


def moe_router(x, nw, mod, rowmap, wr_t_bf16):
    n_tok, d = x.shape
    tm = TOK_TILE
    ne = wr_t_bf16.shape[0]
    return pl.pallas_call(
        _router_kernel,
        grid=(n_tok // tm,),
        in_specs=[pl.BlockSpec((tm, d), lambda i: (i, 0)),
                  pl.BlockSpec((1, d), lambda i: (0, 0)),
                  _mod_spec(rowmap, 4, d), _mod_spec(rowmap, 3, d),
                  pl.BlockSpec((ne, d), lambda i: (0, 0))],
        out_specs=[pl.BlockSpec((tm, d + LANES), lambda i: (i, 0)),
                   pl.BlockSpec((tm // LANES, ne, LANES), lambda i: (i, 0, 0))],
        out_shape=[jax.ShapeDtypeStruct((n_tok, d + LANES), F32),
                   jax.ShapeDtypeStruct((n_tok // LANES, ne, LANES), F32)],
        compiler_params=_cparams(("arbitrary",)),
        name="moe_router",
    )(x, nw, mod, mod, wr_t_bf16)


def _select_kernel(aff_ref, ut_ref, v8_ref, cidx_ref, meta_ref, inc_ref, *, cap, first, slot0):
    nt, ne, _ = aff_ref.shape
    aff = aff_ref[...]

    def count(mask):
        c = jnp.sum(jnp.where(mask, 1.0, 0.0), axis=0)
        return jnp.sum(c, axis=1, keepdims=True)

    def as_float(bits):
        return lax.bitcast_convert_type(bits, F32)

    def radix(k, bits):
        cand = bits | (jnp.int32(1) << (30 - k))
        return jnp.where(count(aff >= as_float(cand)[None]) >= cap, cand, bits)

    thr_bits = lax.fori_loop(0, 31, radix, jnp.zeros((ne, 1), jnp.int32))
    thr = as_float(thr_bits)[None]
    nxt = as_float(thr_bits + 1)[None]
    above = aff >= nxt
    bucket = (aff >= thr) & jnp.logical_not(above)
    need = cap - count(above)
    width = nxt - thr
    pos = jnp.where(bucket & (width > 0.0), (aff - thr) / width, 0.0)

    def refine(k, t):
        cand = t + lax.convert_element_type(jnp.int32(1) << (29 - k), F32) * (2.0 ** -30)
        return jnp.where(count(bucket & (pos >= cand[None])) >= need, cand, t)

    t = lax.fori_loop(0, 30, refine, jnp.zeros((ne, 1), F32))
    upper = bucket & (pos >= (t + 2.0 ** -30)[None])
    tied = bucket & (pos >= t[None]) & jnp.logical_not(upper)
    ut = ut_ref[...]

    def excl_rank(mask):
        m = jnp.where(mask, 1.0, 0.0)
        inc_ref[...] = _dot(m.reshape(nt * ne, LANES).astype(BF16), ut).reshape(nt, ne, LANES)

        def body(tt, carry):
            inc = inc_ref[tt]
            inc_ref[tt] = inc + carry
            return carry + inc[:, LANES - 1:LANES]

        lax.fori_loop(0, nt, body, jnp.zeros((ne, 1), F32))
        return inc_ref[...] - m

    sel = above | upper | (tied & (excl_rank(tied) < (need - count(upper))[None]))
    rank = excl_rank(sel)
    start = rank[:, :, 0:1]
    nsel = rank[:, :, LANES - 1:LANES] + jnp.where(sel[:, :, LANES - 1:LANES], 1.0, 0.0) - start
    lane3 = lax.broadcasted_iota(jnp.int32, (nt, ne, LANES), 2)
    meta_ref[...] = jnp.where(lane3 == 0, start + float(slot0), jnp.where(lane3 == 1, nsel, 0.0)).astype(jnp.int32)

    inc_ref[...] = jnp.where(sel, rank - start, -1.0)
    sub = lax.broadcasted_iota(jnp.int32, (LANES, LANES), 0).astype(F32)
    v8 = v8_ref[...]

    def tile_body(tt, carry):
        rho = inc_ref[tt]
        base = lax.convert_element_type(first + tt * LANES, F32)
        for e in range(ne):
            onehot = jnp.where(sub == rho[e:e + 1, :], 1.0, 0.0).astype(BF16)
            packed = _dot_nt(v8, onehot)
            cidx_ref[tt, e:e + 1, :] = (packed[0:1] + base).astype(jnp.int32)
        return carry

    lax.fori_loop(0, nt, tile_body, 0)


def moe_select(aff_t, cap, first, slot0):
    nt, ne, _ = aff_t.shape
    ut = jnp.triu(jnp.ones((LANES, LANES), BF16))
    v8 = jnp.zeros((8, LANES), BF16).at[0].set(jnp.arange(LANES).astype(BF16))
    blk = pl.BlockSpec((nt, ne, LANES), lambda i: (0, 0, 0))
    return pl.pallas_call(
        functools.partial(_select_kernel, cap=cap, first=first, slot0=slot0),
        grid=(1,),
        in_specs=[blk, pl.BlockSpec((LANES, LANES), lambda i: (0, 0)), pl.BlockSpec((8, LANES), lambda i: (0, 0))],
        out_specs=[blk, blk],
        out_shape=[jax.ShapeDtypeStruct((nt, ne, LANES), jnp.int32),
                   jax.ShapeDtypeStruct((nt, ne, LANES), jnp.int32)],
        scratch_shapes=[pltpu.VMEM((nt, ne, LANES), F32)],
        compiler_params=_cparams(("arbitrary",)),
        name="moe_select",
    )(aff_t, ut, v8)


def _lists_kernel(starts_ref, counts_ref, cidx_ref, idx_ref):
    nt, ne, _ = cidx_ref.shape
    idx_ref[...] = jnp.zeros_like(idx_ref)
    lane = lax.broadcasted_iota(jnp.int32, (1, LANES), 1)

    def tile(t, carry):
        for e in range(ne):
            s = starts_ref[e, t]
            c = counts_ref[e, t]
            j0 = s >> 7
            o = s & (LANES - 1)
            rolled = pltpu.roll(cidx_ref[t, e:e + 1, :], o, axis=1)
            end = o + c
            row0 = idx_ref[e, pl.ds(j0, 1), :]
            idx_ref[e, pl.ds(j0, 1), :] = jnp.where((lane >= o) & (lane < end), rolled, row0)
            row1 = idx_ref[e, pl.ds(j0 + 1, 1), :]
            idx_ref[e, pl.ds(j0 + 1, 1), :] = jnp.where(lane < end - LANES, rolled, row1)
        return carry

    lax.fori_loop(0, nt, tile, 0)


def moe_build_lists(starts, counts, cidx, rows):
    nt, ne, _ = cidx.shape
    rt = rows // LANES + 2
    out = pl.pallas_call(
        _lists_kernel,
        grid_spec=pltpu.PrefetchScalarGridSpec(
            num_scalar_prefetch=2,
            grid=(1,),
            in_specs=[pl.BlockSpec((nt, ne, LANES), lambda i, s, c: (0, 0, 0))],
            out_specs=pl.BlockSpec((ne, rt, LANES), lambda i, s, c: (0, 0, 0))),
        out_shape=jax.ShapeDtypeStruct((ne, rt, LANES), jnp.int32),
        compiler_params=_cparams(("arbitrary",)),
        name="moe_build_lists",
    )(starts, counts, cidx)
    return out.reshape(ne * rt * LANES)


def _expert_kernel(idx_ref, h_hbm, wr_ref, wg_ref, wu_ref, wd_ref, y_ref, xbuf, wgb, wub, wdb, sem, *, rows_pad):
    e = pl.program_id(0)
    ch = pl.program_id(1)
    nch = pl.num_programs(1)
    tr = xbuf.shape[1]
    d, f = wgb.shape
    step = e * nch + ch
    slot = step % 2

    @pl.when(ch == 0)
    def _():
        wgb[...] = wg_ref[0, 0].astype(BF16)
        wub[...] = wu_ref[0, 0].astype(BF16)
        wdb[...] = wd_ref[0, 0].astype(BF16)

    def row_copy(base, r, slot_i):
        return pltpu.make_async_copy(h_hbm.at[pl.ds(idx_ref[base + r], 1)], xbuf.at[slot_i, pl.ds(r, 1)], sem.at[slot_i])

    @pl.when(step == 0)
    def _():
        def group(j, carry):
            for u in range(8):
                row_copy(0, 8 * j + u, 0).start()
            return carry

        lax.fori_loop(0, tr // 8, group, 0)

    has_next = step + 1 < pl.num_programs(0) * nch
    last = ch == nch - 1
    e_n = jnp.where(has_next, jnp.where(last, e + 1, e), e)
    ch_n = jnp.where(has_next, jnp.where(last, 0, ch + 1), ch)
    base_n = e_n * rows_pad + ch_n * tr

    pltpu.make_async_copy(h_hbm.at[pl.ds(0, tr)], xbuf.at[slot], sem.at[slot]).wait()

    x = xbuf[slot, :, :d].astype(BF16)
    logits = _dot(x, wr_ref[...])
    lane = lax.broadcasted_iota(jnp.int32, logits.shape, 1)
    ne = pl.num_programs(0)
    logits = jnp.where(lane < ne, logits, NEG_INF)
    ex = jnp.exp(logits - jnp.max(logits, axis=1, keepdims=True))
    gate = jnp.sum(jnp.where(lane == e, ex, 0.0), axis=1, keepdims=True) / jnp.sum(ex, axis=1, keepdims=True)

    fb = f // FFN_SPLIT
    rb = tr // FFN_SPLIT
    y = None
    for nb in range(FFN_SPLIT):
        for r in range(nb * rb, (nb + 1) * rb):
            row_copy(base_n, r, 1 - slot).start()
        cols = slice(nb * fb, (nb + 1) * fb)
        hmid = (_silu(_dot(x, wgb[:, cols])) * _dot(x, wub[:, cols])).astype(BF16)
        part = _dot(hmid, wdb[cols, :])
        y = part if y is None else y + part
    y_ref[0, :, :d] = y * gate
    y_ref[0, :, d:] = xbuf[slot, :, d:]

    @pl.when(jnp.logical_not(has_next))
    def _():
        pltpu.make_async_copy(h_hbm.at[pl.ds(0, tr)], xbuf.at[1 - slot], sem.at[1 - slot]).wait()


def moe_experts(idx, rows, h_ext, wr_pad_bf16, w_gate, w_up, w_down, layer):
    _, ne, d, f = w_gate.shape
    rows_pad = idx.shape[0] // ne
    dx = h_ext.shape[1]
    tr = 512
    return pl.pallas_call(
        functools.partial(_expert_kernel, rows_pad=rows_pad),
        grid_spec=pltpu.PrefetchScalarGridSpec(
            num_scalar_prefetch=1,
            grid=(ne, rows // tr),
            in_specs=[pl.BlockSpec(memory_space=pl.ANY),
                      pl.BlockSpec((d, LANES), lambda e, c, idx: (0, 0)),
                      pl.BlockSpec((1, 1, d, f), lambda e, c, idx: (layer, e, 0, 0)),
                      pl.BlockSpec((1, 1, d, f), lambda e, c, idx: (layer, e, 0, 0)),
                      pl.BlockSpec((1, 1, f, d), lambda e, c, idx: (layer, e, 0, 0))],
            out_specs=pl.BlockSpec((1, tr, dx), lambda e, c, idx: (e, c, 0)),
            scratch_shapes=[pltpu.VMEM((2, tr, dx), F32),
                            pltpu.VMEM((d, f), BF16), pltpu.VMEM((d, f), BF16), pltpu.VMEM((f, d), BF16),
                            pltpu.SemaphoreType.DMA((2,))]),
        out_shape=jax.ShapeDtypeStruct((ne, rows, dx), F32),
        compiler_params=_cparams(("arbitrary", "arbitrary")),
        name="moe_experts",
    )(idx, h_ext, wr_pad_bf16, w_gate, w_up, w_down)


def _combine_kernel(starts_ref, y_hbm, x_ref, g2_ref, *rest, first_tiles):
    if first_tiles is None:
        o_ref, acc, stage, sem = rest
    else:
        oc_ref, ol_ref, acc, stage, sem = rest
    tb = pl.program_id(0)
    tm, d = x_ref.shape
    ne = y_hbm.shape[0]

    @pl.when(tb == 0)
    def _():
        stage[...] = jnp.zeros(stage.shape, F32)

    def chunk_copy(e, src_row, dst_row):
        return pltpu.make_async_copy(y_hbm.at[e, pl.ds(src_row, 8)], stage.at[pl.ds(dst_row, 8)], sem)

    off = jnp.int32(0)
    for e in range(ne):
        s0 = starts_ref[e, tb]
        s1 = starts_ref[e, tb + 1]
        a = (s0 >> 3) << 3
        nchunk = jnp.where(s1 > s0, (s1 - a + 7) >> 3, 0)

        def issue(c, carry, e=e, a=a, off=off):
            chunk_copy(e, pl.multiple_of(a + 8 * c, 8), pl.multiple_of(off + 8 * c, 8)).start()
            return carry

        lax.fori_loop(0, nchunk, issue, 0)
        off = off + 8 * nchunk

    def drain(c, carry):
        chunk_copy(0, 0, 0).wait()
        return carry

    lax.fori_loop(0, off >> 3, drain, 0)

    acc[...] = jnp.zeros_like(acc)
    want = (lax.broadcasted_iota(jnp.int32, (tm, KCH), 0) + tb * tm).astype(F32)

    def fold(kc, carry):
        rows = stage[pl.ds(pl.multiple_of(kc * KCH, KCH), KCH), :]
        tok = rows[:, d:].T[0:1, :]
        fresh = lax.broadcasted_iota(jnp.int32, (tm, KCH), 1) + kc * KCH < off
        onehot = jnp.where((want == tok) & fresh, 1.0, 0.0).astype(BF16)
        y = rows[:, :d]
        hi = y.astype(BF16)
        lo = (y - hi.astype(F32)).astype(BF16)
        acc[...] += _dot(onehot, hi) + _dot(onehot, lo)
        return carry

    lax.fori_loop(0, (off + KCH - 1) >> 8, fold, 0)
    res = x_ref[...] + g2_ref[0] * acc[...]
    if first_tiles is None:
        o_ref[...] = res
    else:
        @pl.when(tb < first_tiles)
        def _():
            oc_ref[...] = res

        @pl.when(tb >= first_tiles)
        def _():
            ol_ref[...] = res


def moe_combine(starts, y, x, mod, rowmap, split=None):
    n_tok, d = x.shape
    ne, _, dx = y.shape
    tm = TOK_TILE
    stage_rows = -(-(ne * tm + ne * 16) // KCH) * KCH
    if split is None:
        out_specs = pl.BlockSpec((tm, d), lambda i, s: (i, 0))
        out_shape = jax.ShapeDtypeStruct((n_tok, d), F32)
        first_tiles = None
    else:
        first_tiles = split[0]
        out_specs = [pl.BlockSpec((tm, d), lambda i, s: (jnp.minimum(i, first_tiles - 1), 0)),
                     pl.BlockSpec((tm, d), lambda i, s: (jnp.maximum(i - first_tiles, 0), 0))]
        out_shape = [jax.ShapeDtypeStruct((split[0] * tm, d), F32), jax.ShapeDtypeStruct((split[1] * tm, d), F32)]
    return pl.pallas_call(
        functools.partial(_combine_kernel, first_tiles=first_tiles),
        grid_spec=pltpu.PrefetchScalarGridSpec(
            num_scalar_prefetch=1,
            grid=(n_tok // tm,),
            in_specs=[pl.BlockSpec(memory_space=pl.ANY),
                      pl.BlockSpec((tm, d), lambda i, s: (i, 0)),
                      pl.BlockSpec((1, 1, d), lambda i, s: (rowmap(i), 0, 5))],
            out_specs=out_specs,
            scratch_shapes=[pltpu.VMEM((tm, d), F32), pltpu.VMEM((stage_rows, dx), F32),
                            pltpu.SemaphoreType.DMA(())]),
        out_shape=out_shape,
        compiler_params=_cparams(("arbitrary",)),
        name="moe_combine",
    )(starts, y, x, mod)


def moe_layer(x, nw, mod, rowmap, groups, wr, w_gate, w_up, w_down, layer, split=False):
    n_tok, d = x.shape
    ne = wr.shape[1]
    h_ext, aff_t = moe_router(x, nw, mod, rowmap, wr.T.astype(BF16))
    cidx_parts, meta_parts = [], []
    rows = 0
    for grp in groups:
        cap = EC_FACTOR * grp.tokens // ne
        t0 = grp.row0 // LANES
        cidx, meta = moe_select(aff_t[t0:t0 + grp.tokens // LANES], cap, grp.row0, rows)
        cidx_parts.append(cidx)
        meta_parts.append(meta[:, :, :2])
        rows += cap
    cidx = jnp.concatenate(cidx_parts, axis=0)
    meta = jnp.concatenate(meta_parts, axis=0)
    starts = meta[:, :, 0].T
    counts = meta[:, :, 1].T
    idx = moe_build_lists(starts, counts, cidx, rows)
    per = TOK_TILE // LANES
    starts_blk = jnp.concatenate([starts[:, ::per], jnp.full((ne, 1), rows, jnp.int32)], axis=1)
    wr_pad = jnp.zeros((d, LANES), BF16).at[:, :ne].set(wr.astype(BF16))
    y = moe_experts(idx, rows, h_ext, wr_pad, w_gate, w_up, w_down, layer)
    return moe_combine(starts_blk, y, x, mod, rowmap, (groups[0].tiles, groups[1].tiles) if split else None)


def kernel(x_prompt, x_sample, cache_k, cache_v, state_hgrn, state_s5, c, c_ctx, norm_w, ada_w, ada_b, w_in_ab, w_out_ab, hgrn_lb_logits, hgrn_norm_w, q_norm_w, k_norm_w, attn_sink, s5_a_re, s5_a_im, s5_log_dt, s5_b_re, s5_b_im, s5_c_re, s5_c_im, s5_d, glu_w_a, glu_w_b, router_w, exp_w_gate, exp_w_up, exp_w_down):
    b_ctx, l_ctx, d = x_prompt.shape
    b_lat, l_lat, _ = x_sample.shape
    depth = norm_w.shape[0]
    ctx = Group(0, b_ctx, l_ctx)
    lat = Group(ctx.tokens, b_lat, l_lat)
    groups = (ctx, lat)

    def rowmap(i):
        return jnp.where(i < ctx.tiles, i // ctx.seq_tiles, b_ctx + (i - ctx.tiles) // lat.seq_tiles)

    cond = jnp.concatenate([c_ctx[None, :], c, jnp.zeros((8 - 1 - b_lat, d), F32)], axis=0)
    mod_small = ada_modulation(cond, ada_w, ada_b)
    seq_rows = jnp.concatenate([jnp.zeros((b_ctx,), jnp.int32), 1 + jnp.arange(b_lat, dtype=jnp.int32)])
    mods = mod_small[:, seq_rows][:, :, None, :]

    x = jnp.concatenate([x_prompt.reshape(ctx.tokens, d), x_sample.reshape(lat.tokens, d)], axis=0)
    rope = rope_tables(l_lat)
    ks, vs, hs, ss = [], [], [], []
    for l in range(depth):
        mod = mods[l]
        nw1 = norm_w[l, 0].reshape(1, d)
        nw2 = norm_w[l, 1].reshape(1, d)
        if l % 2 == 0:
            e = l // 2
            proj = norm_mod_matmul(x, nw1, mod, rowmap, 1, 0, w_in_ab[e].astype(BF16))
            zero_state = jnp.zeros((b_ctx, 2, A_HEADS, A_DK, A_DK), F32)
            of_c, ob_c, st_c = hgrn2_mixer(proj, ctx, hgrn_lb_logits, zero_state, e)
            of_l, ob_l, _ = hgrn2_mixer(proj, lat, hgrn_lb_logits, jnp.swapaxes(state_hgrn[:, e], -1, -2), e)
            hs.append(jnp.swapaxes(st_c, -1, -2))
            qn_c, kn_c = qk_prepare(proj, ctx, q_norm_w[e], k_norm_w[e], None)
            att_c = context_attention(qn_c, kn_c, proj, ctx, attn_sink[e])
            ks.append(kn_c.reshape(b_ctx, l_ctx, B_KV_HEADS, HEAD_DIM).transpose(0, 2, 1, 3))
            vs.append(proj[:ctx.tokens, V_COL:].reshape(b_ctx, l_ctx, B_KV_HEADS, HEAD_DIM).transpose(0, 2, 1, 3))
            qr_l, kr_l = qk_prepare(proj, lat, q_norm_w[e], k_norm_w[e], rope)
            att_l = latent_attention(qr_l, kr_l, proj, lat, cache_k[:, e], cache_v[:, e], attn_sink[e])
            x = even_out_proj((of_c, of_l), (ob_c, ob_l), (att_c, att_l), proj, x, mod, rowmap, groups,
                              hgrn_norm_w[e].reshape(1, A_DK), w_out_ab[e].astype(BF16))
        else:
            o = l // 2
            ops = s5_operators(s5_a_re[o], s5_a_im[o], s5_log_dt[o], s5_b_re[o], s5_b_im[o], s5_c_re[o], s5_c_im[o])
            zero_s5 = jnp.zeros((b_ctx, 2, 2, d // S5_GROUP, S5_STATE), F32)
            y_c, fin_c = s5_mixer(x, ctx, nw1, mod, rowmap, ops, zero_s5)
            y_l, _ = s5_mixer(x, lat, nw1, mod, rowmap, ops, state_s5[:, o])
            ss.append(fin_c)
            x = glu_residual((y_c, y_l), x, nw1, mod, rowmap, groups, s5_d[o].reshape(1, d),
                             glu_w_a[o].astype(BF16), glu_w_b[o].astype(BF16))
        x = moe_layer(x, nw2, mod, rowmap, groups, router_w[l], exp_w_gate, exp_w_up, exp_w_down, l,
                      split=(l == depth - 1))
    y_prompt = x[0].reshape(b_ctx, l_ctx, d)
    y_sample = x[1].reshape(b_lat, l_lat, d)
    return (y_prompt, y_sample, jnp.stack(ks, axis=1), jnp.stack(vs, axis=1),
            jnp.stack(hs, axis=1), jnp.stack(ss, axis=1))
```

```python
import functools

import jax
import jax.numpy as jnp
from jax import lax
from jax.experimental import pallas as pl
from jax.experimental.pallas import tpu as pltpu

F32 = jnp.float32
BF16 = jnp.bfloat16

A_HEADS = 4
A_DK = 128
A_WIDTH = A_HEADS * A_DK
B_HEADS = 8
B_KV_HEADS = 2
HEAD_DIM = 64
B_GROUP = B_HEADS // B_KV_HEADS
B_WIDTH = B_HEADS * HEAD_DIM
KV_WIDTH = B_KV_HEADS * HEAD_DIM
Q_COL = 5 * A_WIDTH
K_COL = Q_COL + B_WIDTH
V_COL = K_COL + KV_WIDTH
WINDOW = 128
GRID_W = 64
ROPE_THETA = 10000.0
S5_GROUP = 16
S5_STATE = 64
S5_CHUNK = 16
S5_TC = S5_CHUNK * S5_GROUP
N_EXPERTS = 16
EC_FACTOR = 2
EPS = 1e-6
NEG_INF = -1e30

HGRN_C = 128
HGRN_SB = 16
TOK_TILE = 256
LANES = 128
VMEM_LIMIT = 56 * 1024 * 1024
STATE_W = 2 * S5_STATE
KCH = 256
FFN_SPLIT = 4


def _cparams(sem):
    return pltpu.CompilerParams(dimension_semantics=sem, vmem_limit_bytes=VMEM_LIMIT)


def _sigmoid(x):
    return 1.0 / (1.0 + jnp.exp(-x))


def _silu(x):
    return x * _sigmoid(x)


def _norm_mod(x, nw, sc, sh):
    ms = jnp.mean(x * x, axis=-1, keepdims=True)
    return (x * lax.rsqrt(ms + EPS) * nw) * (1.0 + sc) + sh


def _dot(a, b):
    return jnp.dot(a, b, preferred_element_type=F32)


def _dot_nt(a, b):
    return lax.dot_general(a, b, (((1,), (1,)), ((), ())), preferred_element_type=F32)


def _split3(x):
    hi = x.astype(BF16)
    r1 = x - hi.astype(F32)
    mid = r1.astype(BF16)
    lo = (r1 - mid.astype(F32)).astype(BF16)
    return hi, mid, lo


class Group:
    def __init__(self, row0, bsz, seq):
        self.row0, self.bsz, self.seq = row0, bsz, seq
        self.tokens = bsz * seq
        self.tile0 = row0 // TOK_TILE
        self.tiles = self.tokens // TOK_TILE
        self.seq_tiles = seq // TOK_TILE


def _ada_kernel(c_ref, w_ref, b_ref, o_ref):
    s = _silu(c_ref[...])
    o_ref[0] = _dot(s.astype(BF16), w_ref[0].astype(BF16)) + b_ref[0]


def ada_modulation(cond, ada_w, ada_b):
    depth, d, n = ada_w.shape
    rows = cond.shape[0]
    tn = 1536
    return pl.pallas_call(
        _ada_kernel,
        grid=(depth, n // tn),
        in_specs=[pl.BlockSpec((rows, d), lambda l, j: (0, 0)),
                  pl.BlockSpec((1, d, tn), lambda l, j: (l, 0, j)),
                  pl.BlockSpec((1, 1, tn), lambda l, j: (l, 0, j))],
        out_specs=pl.BlockSpec((1, rows, tn), lambda l, j: (l, 0, j)),
        out_shape=jax.ShapeDtypeStruct((depth, rows, n), F32),
        compiler_params=_cparams(("arbitrary", "arbitrary")),
        name="ada_modulation",
    )(cond, ada_w, ada_b.reshape(depth, 1, n))


def _mod_spec(rowmap, k, d):
    return pl.BlockSpec((1, 1, d), lambda i: (rowmap(i), 0, k))


def _inproj_kernel(x_ref, nw_ref, sc_ref, sh_ref, w_ref, o_ref):
    h = _norm_mod(x_ref[...], nw_ref[...], sc_ref[0], sh_ref[0])
    o_ref[...] = _dot(h.astype(BF16), w_ref[...])


def norm_mod_matmul(x, nw, mod, rowmap, k_sc, k_sh, w_bf16):
    n_tok, d = x.shape
    n = w_bf16.shape[1]
    tm = TOK_TILE
    return pl.pallas_call(
        _inproj_kernel,
        grid=(n_tok // tm,),
        in_specs=[pl.BlockSpec((tm, d), lambda i: (i, 0)),
                  pl.BlockSpec((1, d), lambda i: (0, 0)),
                  _mod_spec(rowmap, k_sc, d), _mod_spec(rowmap, k_sh, d),
                  pl.BlockSpec((d, n), lambda i: (0, 0))],
        out_specs=pl.BlockSpec((tm, n), lambda i: (i, 0)),
        out_shape=jax.ShapeDtypeStruct((n_tok, n), F32),
        compiler_params=_cparams(("arbitrary",)),
        name="norm_mod_matmul",
    )(x, nw, mod, mod, w_bf16)


def _two_group_specs(groups, shape_of):
    first, second = groups
    return [pl.BlockSpec(shape_of, lambda i: (jnp.minimum(i, first.tiles - 1), 0)),
            pl.BlockSpec(shape_of, lambda i: (jnp.maximum(i - first.tiles, 0), 0))]


def _outproj_kernel(ofc_ref, ofl_ref, obc_ref, obl_ref, atc_ref, atl_ref, ga_ref, x_ref, g1_ref, hw_ref, w_ref,
                    o_ref, *, first_tiles):
    in_first = pl.program_id(0) < first_tiles
    o = jnp.where(in_first, ofc_ref[...] + obc_ref[...], ofl_ref[...] + obl_ref[...])
    o_att = jnp.where(in_first, atc_ref[...], atl_ref[...])
    gate = _silu(ga_ref[...])
    hw = hw_ref[...]
    parts = []
    for h in range(A_HEADS):
        sl = slice(h * A_DK, (h + 1) * A_DK)
        oh = o[:, sl]
        ms = jnp.mean(oh * oh, axis=-1, keepdims=True)
        parts.append(((oh * lax.rsqrt(ms + EPS) * hw) * gate[:, sl]).astype(BF16))
    parts.append(o_att.astype(BF16))
    y = _dot(jnp.concatenate(parts, axis=1), w_ref[...])
    o_ref[...] = x_ref[...] + g1_ref[0] * y


def even_out_proj(o_f, o_b, o_att, proj, x, mod, rowmap, groups, hw, w_bf16):
    n_tok, d = x.shape
    tm = TOK_TILE
    aw = A_WIDTH
    return pl.pallas_call(
        functools.partial(_outproj_kernel, first_tiles=groups[0].tiles),
        grid=(n_tok // tm,),
        in_specs=_two_group_specs(groups, (tm, aw)) + _two_group_specs(groups, (tm, aw))
                 + _two_group_specs(groups, (tm, B_WIDTH))
                 + [pl.BlockSpec((tm, aw), lambda i: (i, 4)),
                    pl.BlockSpec((tm, d), lambda i: (i, 0)),
                    _mod_spec(rowmap, 2, d),
                    pl.BlockSpec((1, A_DK), lambda i: (0, 0)),
                    pl.BlockSpec((aw + B_WIDTH, d), lambda i: (0, 0))],
        out_specs=pl.BlockSpec((tm, d), lambda i: (i, 0)),
        out_shape=jax.ShapeDtypeStruct((n_tok, d), F32),
        compiler_params=_cparams(("arbitrary",)),
        name="even_out_proj",
    )(o_f[0], o_f[1], o_b[0], o_b[1], o_att[0], o_att[1], proj, x, mod, hw, w_bf16)


def _hgrn_chunk(q, k, v, g, st, msel, rev):
    c = q.shape[0]
    nb = c // HGRN_SB
    row = lax.broadcasted_iota(jnp.int32, (c, c), 0)
    col = lax.broadcasted_iota(jnp.int32, (c, c), 1)
    tri = jnp.where((col >= row) if rev else (col <= row), 1.0, 0.0).astype(BF16)
    gh, gm, gl = _split3(g)
    b = _dot(tri, gh) + _dot(tri, gm) + _dot(tri, gl)
    b_edge = b[0:1] if rev else b[c - 1:c]
    qs = q * jnp.exp(b)
    kdec = k * jnp.exp(b_edge - b)

    lk = jnp.log(jnp.maximum(k, 0.0))
    half = HGRN_SB // 2
    zero_half = jnp.zeros((half, A_DK), F32)
    slabs = []
    for i in range(nb):
        sl = slice(i * HGRN_SB, (i + 1) * HGRN_SB)
        bi, qi, lki = b[sl], q[sl], lk[sl]
        ci = bi - lki
        pieces = []
        for s in range(HGRN_SB):
            s_half = s // half
            halves = []
            for hh in range(2):
                rows = slice(hh * half, (hh + 1) * half)
                if (hh > s_half) if rev else (hh < s_half):
                    halves.append(zero_half)
                    continue
                d = bi[rows] - ci[s:s + 1]
                if hh == s_half:
                    d = jnp.minimum(d, lki[s:s + 1])
                halves.append(qi[rows] * jnp.exp(d))
            pieces.append(jnp.concatenate(halves, axis=0).astype(BF16))
        slabs.append(jnp.concatenate(pieces, axis=1))
    a_loc = _dot(jnp.concatenate(slabs, axis=0), msel)

    lane = lax.broadcasted_iota(jnp.int32, (HGRN_SB, LANES), 1)
    rloc = lax.broadcasted_iota(jnp.int32, (HGRN_SB, LANES), 0)
    dmask = ((lane >= rloc) & (lane < HGRN_SB)) if rev else (lane <= rloc)
    krow = lax.broadcasted_iota(jnp.int32, (c, A_DK), 0)
    att_rows = []
    for i in range(nb):
        sl = slice(i * HGRN_SB, (i + 1) * HGRN_SB)
        a_d = jnp.where(dmask, a_loc[sl], 0.0)
        if i > 0:
            a_d = pltpu.roll(a_d, i * HGRN_SB, axis=1)
        a_i = a_d[:, :c]
        has_off = (i < nb - 1) if rev else (i > 0)
        if has_off:
            edge = (i + 1) * HGRN_SB if rev else i * HGRN_SB
            r = b[edge:edge + 1] if rev else b[edge - 1:edge]
            qp = q[sl] * jnp.exp(b[sl] - r)
            live = (krow >= edge) if rev else (krow < edge)
            kp = jnp.where(live, k * jnp.exp(jnp.minimum(r - b, 0.0)), 0.0)
            a_i = a_i + _dot_nt(qp.astype(BF16), kp.astype(BF16))
        att_rows.append(a_i)
    att = jnp.concatenate(att_rows, axis=0)

    vb = v.astype(BF16)
    o = _dot(att.astype(BF16), vb) + _dot_nt(qs.astype(BF16), st.astype(BF16))
    st_new = st * jnp.exp(b_edge) + _dot(v.T.astype(BF16), kdec.astype(BF16))
    return o, st_new


def _hgrn_kernel(qf_ref, vf_ref, ff_ref, qb_ref, vb_ref, fb_ref, lbl_ref, msel_ref, s0_ref,
                 of_ref, ob_ref, sout_ref, st_ref, *, layer):
    c_idx = pl.program_id(1)

    @pl.when(c_idx == 0)
    def _():
        st_ref[...] = s0_ref[0]

    lg = lbl_ref[...]
    ex = jnp.exp(lg - jnp.max(lg, axis=0, keepdims=True))
    pr = ex / jnp.sum(ex, axis=0, keepdims=True)
    lb = jnp.zeros_like(pr[0])
    for e in range(1, layer + 1):
        lb = lb + pr[e]
    msel = msel_ref[...]

    for d, (q_ref, v_ref, f_ref, o_ref) in enumerate(((qf_ref, vf_ref, ff_ref, of_ref),
                                                       (qb_ref, vb_ref, fb_ref, ob_ref))):
        q_all = _silu(q_ref[...])
        v_all = v_ref[...]
        lbd = lb[d:d + 1]
        forget = lbd + (1.0 - lbd) * _sigmoid(f_ref[...])
        k_all = 1.0 - forget
        g_all = jnp.log(forget)
        for h in range(A_HEADS):
            sl = slice(h * A_DK, (h + 1) * A_DK)
            o, st_new = _hgrn_chunk(q_all[:, sl], k_all[:, sl], v_all[:, sl], g_all[:, sl],
                                    st_ref[d, h], msel, rev=(d == 1))
            o_ref[:, sl] = o
            st_ref[d, h] = st_new

    @pl.when(c_idx == pl.num_programs(1) - 1)
    def _():
        sout_ref[0] = st_ref[...]


def hgrn2_mixer(proj, grp, lb_logits, s0t, layer):
    c = HGRN_C
    nc = grp.seq // c
    blk0 = grp.row0 // c
    aw = A_WIDTH
    msel = jnp.repeat(jnp.eye(HGRN_SB, LANES, dtype=BF16), A_DK, axis=0)

    def fwd(col):
        return pl.BlockSpec((c, aw), lambda b, i: (blk0 + b * nc + i, col))

    def bwd(col):
        return pl.BlockSpec((c, aw), lambda b, i: (blk0 + b * nc + nc - 1 - i, col))

    st_spec = pl.BlockSpec((1, 2, A_HEADS, A_DK, A_DK), lambda b, i: (b, 0, 0, 0, 0))
    return pl.pallas_call(
        functools.partial(_hgrn_kernel, layer=layer),
        grid=(grp.bsz, nc),
        in_specs=[fwd(0), fwd(3), fwd(1), bwd(0), bwd(3), bwd(2),
                  pl.BlockSpec(lb_logits.shape, lambda b, i: (0, 0, 0)),
                  pl.BlockSpec(msel.shape, lambda b, i: (0, 0)),
                  st_spec],
        out_specs=[pl.BlockSpec((c, aw), lambda b, i: (b * nc + i, 0)),
                   pl.BlockSpec((c, aw), lambda b, i: (b * nc + nc - 1 - i, 0)),
                   st_spec],
        out_shape=[jax.ShapeDtypeStruct((grp.tokens, aw), F32),
                   jax.ShapeDtypeStruct((grp.tokens, aw), F32),
                   jax.ShapeDtypeStruct((grp.bsz, 2, A_HEADS, A_DK, A_DK), F32)],
        scratch_shapes=[pltpu.VMEM((2, A_HEADS, A_DK, A_DK), F32)],
        compiler_params=_cparams(("arbitrary", "arbitrary")),
        name="hgrn2_mixer",
    )(proj, proj, proj, proj, proj, proj, lb_logits, msel, s0t)


def _head_norm(x, w, gmat):
    hi, mid, lo = _split3(x * x)
    ms = _dot(hi, gmat) + _dot(mid, gmat) + _dot(lo, gmat)
    return x * lax.rsqrt(ms + EPS) * w


def _rope(x, cos, sin_signed):
    width = x.shape[1]
    lane = lax.broadcasted_iota(jnp.int32, x.shape, 1)
    nxt = pltpu.roll(x, width - 1, axis=1)
    prv = pltpu.roll(x, 1, axis=1)
    partner = jnp.where(lane % 2 == 0, nxt, prv)
    return x * cos + partner * sin_signed


def _qkprep_kernel(*refs, rope):
    if rope:
        q_ref, k_ref, qw_ref, kw_ref, gm_ref, cos_ref, sin_ref, qo_ref, ko_ref = refs
    else:
        q_ref, k_ref, qw_ref, kw_ref, gm_ref, qo_ref, ko_ref = refs
    gm = gm_ref[...]
    qn = _head_norm(q_ref[...], qw_ref[...], gm)
    kn = _head_norm(k_ref[...], kw_ref[...], gm[:KV_WIDTH, :KV_WIDTH])
    if rope:
        cos = cos_ref[...]
        sin = sin_ref[...]
        qn = _rope(qn, cos, sin)
        kn = _rope(kn, cos[:, :KV_WIDTH], sin[:, :KV_WIDTH])
    qo_ref[...] = (qn * (HEAD_DIM ** -0.5)).astype(BF16)
    ko_ref[...] = kn


def qk_prepare(proj, grp, qw, kw, rope_tabs):
    tm = TOK_TILE
    st = grp.seq_tiles
    gidx = jnp.arange(B_WIDTH) // HEAD_DIM
    gmat = jnp.where(gidx[:, None] == gidx[None, :], 1.0 / HEAD_DIM, 0.0).astype(BF16)
    qw_t = jnp.tile(qw, B_HEADS).reshape(1, B_WIDTH)
    kw_t = jnp.tile(kw, B_KV_HEADS).reshape(1, KV_WIDTH)
    in_specs = [pl.BlockSpec((tm, B_WIDTH), lambda i: (grp.tile0 + i, Q_COL // B_WIDTH)),
                pl.BlockSpec((tm, KV_WIDTH), lambda i: (grp.tile0 + i, K_COL // KV_WIDTH)),
                pl.BlockSpec((1, B_WIDTH), lambda i: (0, 0)),
                pl.BlockSpec((1, KV_WIDTH), lambda i: (0, 0)),
                pl.BlockSpec((B_WIDTH, B_WIDTH), lambda i: (0, 0))]
    args = [proj, proj, qw_t, kw_t, gmat]
    if rope_tabs is not None:
        in_specs += [pl.BlockSpec((tm, B_WIDTH), lambda i: (i % st, 0)),
                     pl.BlockSpec((tm, B_WIDTH), lambda i: (i % st, 0))]
        args += list(rope_tabs)
    return pl.pallas_call(
        functools.partial(_qkprep_kernel, rope=rope_tabs is not None),
        grid=(grp.tiles,),
        in_specs=in_specs,
        out_specs=[pl.BlockSpec((tm, B_WIDTH), lambda i: (i, 0)),
                   pl.BlockSpec((tm, KV_WIDTH), lambda i: (i, 0))],
        out_shape=[jax.ShapeDtypeStruct((grp.tokens, B_WIDTH), BF16),
                   jax.ShapeDtypeStruct((grp.tokens, KV_WIDTH), F32)],
        compiler_params=_cparams(("arbitrary",)),
        name="qk_prepare",
    )(*args)


def rope_tables(seq):
    pos = jnp.arange(seq)
    row = (pos // GRID_W).astype(F32)
    col = (pos % GRID_W).astype(F32)
    n_pair = HEAD_DIM // 4
    freqs = ROPE_THETA ** (-jnp.arange(n_pair, dtype=F32) / n_pair)
    ang = jnp.concatenate([row[:, None] * freqs, col[:, None] * freqs], axis=-1)
    cos = jnp.repeat(jnp.cos(ang), 2, axis=-1)
    sin = jnp.repeat(jnp.sin(ang), 2, axis=-1) * jnp.tile(jnp.array([-1.0, 1.0], F32), HEAD_DIM // 2)
    return jnp.tile(cos, (1, B_HEADS)), jnp.tile(sin, (1, B_HEADS))


def _value_with_ones(v2, kvh):
    lane = lax.broadcasted_iota(jnp.int32, v2.shape, 1)
    if kvh == 1:
        v2 = pltpu.roll(v2, HEAD_DIM, axis=1)
    return jnp.where(lane < HEAD_DIM, v2, 1.0).astype(BF16)


def _group_attention(q, kk, vv1, valid, sink_ref, kvh, o_ref):
    tq = q.shape[0]
    heads = [kvh * B_GROUP + gq for gq in range(B_GROUP)]
    qs = jnp.concatenate([q[:, h * HEAD_DIM:(h + 1) * HEAD_DIM] for h in heads], axis=0)
    sink = jnp.concatenate([jnp.broadcast_to(sink_ref[h:h + 1, 0:1], (tq, 1)) for h in heads], axis=0)
    s = _dot_nt(qs, kk)
    if valid is not None:
        s = jnp.where(jnp.concatenate([valid] * B_GROUP, axis=0), s, NEG_INF)
    m = jnp.maximum(jnp.max(s, axis=1, keepdims=True), sink)
    pv = _dot(jnp.exp(s - m).astype(BF16), vv1)
    den = pv[:, HEAD_DIM:HEAD_DIM + 1] + jnp.exp(sink - m)
    o = pv[:, :HEAD_DIM] / den
    for gq, h in enumerate(heads):
        o_ref[:, h * HEAD_DIM:(h + 1) * HEAD_DIM] = o[gq * tq:(gq + 1) * tq]


def _ctx_attn_kernel(q_ref, k_ref, v_ref, sink_ref, o_ref):
    q = q_ref[...]
    k = k_ref[...].astype(BF16)
    v = v_ref[...]
    for kvh in range(B_KV_HEADS):
        ks = slice(kvh * HEAD_DIM, (kvh + 1) * HEAD_DIM)
        _group_attention(q, k[:, ks], _value_with_ones(v, kvh), None, sink_ref, kvh, o_ref)


def context_attention(qn, kn, proj, grp, sink):
    seq = grp.seq
    blk0 = grp.row0 // seq
    sink_t = jnp.broadcast_to(sink.reshape(B_HEADS, 1), (B_HEADS, LANES))
    return pl.pallas_call(
        _ctx_attn_kernel,
        grid=(grp.bsz,),
        in_specs=[pl.BlockSpec((seq, B_WIDTH), lambda b: (b, 0)),
                  pl.BlockSpec((seq, KV_WIDTH), lambda b: (b, 0)),
                  pl.BlockSpec((seq, KV_WIDTH), lambda b: (blk0 + b, V_COL // KV_WIDTH)),
                  pl.BlockSpec((B_HEADS, LANES), lambda b: (0, 0))],
        out_specs=pl.BlockSpec((seq, B_WIDTH), lambda b: (b, 0)),
        out_shape=jax.ShapeDtypeStruct((grp.tokens, B_WIDTH), F32),
        compiler_params=_cparams(("arbitrary",)),
        name="context_attention",
    )(qn, kn, proj, sink_t)


def _lat_attn_kernel(q_ref, kp_ref, kc_ref, kn_ref, vp_ref, vc_ref, vn_ref, kx_ref, vx_ref, sink_ref, o_ref):
    blk = pl.program_id(1)
    nblk = pl.num_programs(1)
    tq = q_ref.shape[0]
    q = q_ref[...]
    kl = jnp.concatenate([kp_ref[...], kc_ref[...], kn_ref[...]], axis=0).astype(BF16)
    vl = jnp.concatenate([vp_ref[...], vc_ref[...], vn_ref[...]], axis=0)
    n_ctx = kx_ref.shape[2]
    span = 3 * tq
    i = lax.broadcasted_iota(jnp.int32, (tq, span + n_ctx), 0)
    j = lax.broadcasted_iota(jnp.int32, (tq, span + n_ctx), 1)
    dist = j - tq - i
    valid = (dist >= -WINDOW) & (dist <= WINDOW)
    valid = valid & ((j >= tq) | (blk > 0)) & ((j < 2 * tq) | (blk < nblk - 1))
    valid = valid | (j >= span)
    for kvh in range(B_KV_HEADS):
        ks = slice(kvh * HEAD_DIM, (kvh + 1) * HEAD_DIM)
        kk = jnp.concatenate([kl[:, ks], kx_ref[0, kvh].astype(BF16)], axis=0)
        vv1 = jnp.concatenate([_value_with_ones(vl, kvh), vx_ref[0, kvh].astype(BF16)], axis=0)
        _group_attention(q, kk, vv1, valid, sink_ref, kvh, o_ref)


def latent_attention(qr, kr, proj, grp, k_ctx, v_ctx, sink):
    tq = WINDOW
    nblk = grp.seq // tq
    blk0 = grp.row0 // tq
    n_ctx = k_ctx.shape[2]
    sink_t = jnp.broadcast_to(sink.reshape(B_HEADS, 1), (B_HEADS, LANES))
    v_ctx1 = jnp.concatenate([v_ctx, jnp.ones_like(v_ctx)], axis=-1)

    def kv_specs(off, col):
        return [pl.BlockSpec((tq, KV_WIDTH), lambda b, i: (off + b * nblk + jnp.maximum(i - 1, 0), col)),
                pl.BlockSpec((tq, KV_WIDTH), lambda b, i: (off + b * nblk + i, col)),
                pl.BlockSpec((tq, KV_WIDTH), lambda b, i: (off + b * nblk + jnp.minimum(i + 1, nblk - 1), col))]

    return pl.pallas_call(
        _lat_attn_kernel,
        grid=(grp.bsz, nblk),
        in_specs=[pl.BlockSpec((tq, B_WIDTH), lambda b, i: (b * nblk + i, 0))]
                 + kv_specs(0, 0) + kv_specs(blk0, V_COL // KV_WIDTH)
                 + [pl.BlockSpec((1, B_KV_HEADS, n_ctx, HEAD_DIM), lambda b, i: (b, 0, 0, 0)),
                    pl.BlockSpec((1, B_KV_HEADS, n_ctx, 2 * HEAD_DIM), lambda b, i: (b, 0, 0, 0)),
                    pl.BlockSpec((B_HEADS, LANES), lambda b, i: (0, 0))],
        out_specs=pl.BlockSpec((tq, B_WIDTH), lambda b, i: (b * nblk + i, 0)),
        out_shape=jax.ShapeDtypeStruct((grp.tokens, B_WIDTH), F32),
        compiler_params=_cparams(("arbitrary", "arbitrary")),
        name="latent_attention",
    )(qr, kr, kr, kr, proj, proj, proj, k_ctx, v_ctx1, sink_t)


def s5_operators(a_re, a_im, log_dt, b_re, b_im, c_re, c_im):
    t = S5_CHUNK
    hi = lax.Precision.HIGHEST
    ks, ws, wsw, vs, a1s, a2s = [], [], [], [], [], []
    for d in range(2):
        are, aim = a_re[d].astype(F32), a_im[d].astype(F32)
        dt = jnp.exp(log_dt[d].astype(F32))[:, None]
        den = are * are + aim * aim
        steps = jnp.arange(t + 1, dtype=F32)[:, None, None]
        mag = jnp.exp(steps * (dt * are))
        pw_re = mag * jnp.cos(steps * (dt * aim))
        pw_im = mag * jnp.sin(steps * (dt * aim))
        ab_re, ab_im = pw_re[1], pw_im[1]
        f_re = ((ab_re - 1.0) * are + ab_im * aim) / den
        f_im = (ab_im * are - (ab_re - 1.0) * aim) / den
        bre, bim = b_re[d].astype(F32), b_im[d].astype(F32)
        bb_re = f_re[..., None] * bre - f_im[..., None] * bim
        bb_im = f_re[..., None] * bim + f_im[..., None] * bre
        cre, cim = c_re[d].astype(F32), c_im[d].astype(F32)
        pgr = pw_re.transpose(1, 0, 2)[:, :, None, :]
        pgi = pw_im.transpose(1, 0, 2)[:, :, None, :]
        cp_re = cre[:, None] * pgr - cim[:, None] * pgi
        cp_im = cre[:, None] * pgi + cim[:, None] * pgr
        m = (jnp.einsum('gkcp,gpd->gkcd', cp_re[:, :t], bb_re, precision=hi)
             - jnp.einsum('gkcp,gpd->gkcd', cp_im[:, :t], bb_im, precision=hi))
        s_i = jnp.arange(t)[:, None]
        t_i = jnp.arange(t)[None, :]
        lag = (t_i - s_i) if d == 0 else (s_i - t_i)
        blk = jnp.where((lag >= 0)[None, :, :, None, None], m[:, jnp.clip(lag, 0, t - 1)], 0.0)
        ks.append(blk.transpose(0, 1, 4, 2, 3).reshape(-1, S5_TC, S5_TC))
        pidx = (t - 1 - jnp.arange(t)) if d == 0 else jnp.arange(t)
        pr = pw_re[pidx].transpose(1, 0, 2)[:, :, None, :]
        pi = pw_im[pidx].transpose(1, 0, 2)[:, :, None, :]
        bbr = bb_re.transpose(0, 2, 1)[:, None]
        bbi = bb_im.transpose(0, 2, 1)[:, None]
        w_re = pr * bbr - pi * bbi
        w_im = pr * bbi + pi * bbr
        ws.append(jnp.concatenate([w_re, w_im], axis=-1).reshape(-1, S5_TC, 2 * S5_STATE))
        wsw.append(jnp.concatenate([w_im, w_re], axis=-1).reshape(-1, S5_TC, 2 * S5_STATE))
        kidx = (jnp.arange(t) + 1) if d == 0 else (t - jnp.arange(t))
        v = jnp.concatenate([cp_re[:, kidx], -cp_im[:, kidx]], axis=-1)
        vs.append(v.transpose(0, 3, 1, 2).reshape(-1, 2 * S5_STATE, S5_TC))
        a1s.append(jnp.concatenate([pw_re[t], pw_re[t]], axis=-1))
        a2s.append(jnp.concatenate([-pw_im[t], pw_im[t]], axis=-1))
    return (ks[0] + ks[1], jnp.concatenate(ws + wsw, axis=-1), jnp.concatenate(vs, axis=1),
            jnp.concatenate(a1s, axis=-1), jnp.concatenate(a2s, axis=-1))


def _to_chunks_kernel(x_ref, nw_ref, sc_ref, sh_ref, o_ref, hbuf):
    h = _norm_mod(x_ref[...], nw_ref[...], sc_ref[0], sh_ref[0])
    gpl = LANES // S5_GROUP
    for c in range(hbuf.shape[0]):
        hbuf[c] = h[:, c * LANES:(c + 1) * LANES]
    for c in range(hbuf.shape[0]):
        for t in range(S5_CHUNK):
            rows = hbuf[c, pl.ds(t, TOK_TILE // S5_CHUNK, stride=S5_CHUNK), :]
            for g in range(gpl):
                o_ref[c * gpl + g, :, t * S5_GROUP:(t + 1) * S5_GROUP] = rows[:, g * S5_GROUP:(g + 1) * S5_GROUP]


def s5_to_chunks(x, grp, nw, mod, rowmap):
    d = x.shape[1]
    ng = d // S5_GROUP
    tm = TOK_TILE
    cpt = tm // S5_CHUNK
    st = grp.seq_tiles

    def mspec(k):
        return pl.BlockSpec((1, 1, d), lambda i: (rowmap(grp.tile0 + i), 0, k))

    return pl.pallas_call(
        _to_chunks_kernel,
        grid=(grp.tiles,),
        in_specs=[pl.BlockSpec((tm, d), lambda i: (grp.tile0 + i, 0)),
                  pl.BlockSpec((1, d), lambda i: (0, 0)), mspec(1), mspec(0)],
        out_specs=pl.BlockSpec((ng, cpt, S5_TC), lambda i: (0, i % st, i // st)),
        out_shape=jax.ShapeDtypeStruct((ng, grp.seq // S5_CHUNK, grp.bsz * S5_TC), F32),
        scratch_shapes=[pltpu.VMEM((d // LANES, tm, LANES), F32)],
        compiler_params=_cparams(("arbitrary",)),
        name="s5_to_chunks",
    )(x, nw, mod, mod)


def _s5_states_kernel(uf_ref, ub_ref, w_ref, a1_ref, a2_ref, s0_ref, stf_ref, stb_ref, fin_ref, dbuf, carry, *, bsz):
    r = pl.program_id(1)
    gb, tr, _ = uf_ref.shape
    nc = tr // bsz
    sw = STATE_W

    @pl.when(r == 0)
    def _():
        for g in range(gb):
            s0 = s0_ref[g]
            for d in range(2):
                s = s0[:, d * sw:(d + 1) * sw]
                carry[g, 2 * d] = s
                carry[g, 2 * d + 1] = pltpu.roll(s, sw // 2, axis=1)

    for g in range(gb):
        w = w_ref[g]
        dbuf[g, 0] = _dot(uf_ref[g].astype(BF16), w[:, :2 * sw])
        dbuf[g, 1] = _dot(ub_ref[g].astype(BF16), w[:, 2 * sw:])

    for g in range(gb):
        a1 = a1_ref[g]
        a2 = a2_ref[g]
        for d, st_ref in enumerate((stf_ref, stb_ref)):
            a1d = a1[:, d * sw:(d + 1) * sw]
            a2d = a2[:, d * sw:(d + 1) * sw]
            s = carry[g, 2 * d]
            x = carry[g, 2 * d + 1]
            for c in range(nc):
                rows = (c if d == 0 else nc - 1 - c) * bsz
                st_ref[rows:rows + bsz, g * sw:(g + 1) * sw] = s
                dd = dbuf[g, d, rows:rows + bsz, :]
                s, x = a1d * s + a2d * x + dd[:, :sw], a1d * x - a2d * s + dd[:, sw:]
            carry[g, 2 * d] = s
            carry[g, 2 * d + 1] = x

    @pl.when(r == pl.num_programs(1) - 1)
    def _():
        for g in range(gb):
            fin_ref[g, :, :sw] = carry[g, 0]
            fin_ref[g, :, sw:] = carry[g, 2]


def s5_chunk_states_scan(u, w_bf16, a1, a2, s0g, bsz):
    g, r, tc = u.shape
    n = w_bf16.shape[2]
    sw = STATE_W
    gb = 8
    tr = min(r, 256)
    nblk = r // tr
    return pl.pallas_call(
        functools.partial(_s5_states_kernel, bsz=bsz),
        grid=(g // gb, nblk),
        in_specs=[pl.BlockSpec((gb, tr, tc), lambda i, j: (i, j, 0)),
                  pl.BlockSpec((gb, tr, tc), lambda i, j: (i, nblk - 1 - j, 0)),
                  pl.BlockSpec((gb, tc, n), lambda i, j: (i, 0, 0)),
                  pl.BlockSpec((gb, 1, 2 * sw), lambda i, j: (i, 0, 0)),
                  pl.BlockSpec((gb, 1, 2 * sw), lambda i, j: (i, 0, 0)),
                  pl.BlockSpec((gb, bsz, 2 * sw), lambda i, j: (i, 0, 0))],
        out_specs=[pl.BlockSpec((tr, gb * sw), lambda i, j: (j, i)),
                   pl.BlockSpec((tr, gb * sw), lambda i, j: (nblk - 1 - j, i)),
                   pl.BlockSpec((gb, bsz, 2 * sw), lambda i, j: (i, 0, 0))],
        out_shape=[jax.ShapeDtypeStruct((r, g * sw), F32),
                   jax.ShapeDtypeStruct((r, g * sw), F32),
                   jax.ShapeDtypeStruct((g, bsz, 2 * sw), F32)],
        scratch_shapes=[pltpu.VMEM((gb, 2, tr, 2 * sw), F32), pltpu.VMEM((gb, 4, bsz, sw), F32)],
        compiler_params=_cparams(("arbitrary", "arbitrary")),
        name="s5_chunk_states_scan",
    )(u, u, w_bf16, a1.reshape(g, 1, 2 * sw), a2.reshape(g, 1, 2 * sw), s0g)


def _s5_out_kernel(u_ref, sf_ref, sb_ref, k_ref, v_ref, o_ref):
    gb = u_ref.shape[0]
    sw = STATE_W
    for g in range(gb):
        s = jnp.concatenate([sf_ref[:, g * sw:(g + 1) * sw], sb_ref[:, g * sw:(g + 1) * sw]], axis=1)
        o_ref[g] = _dot(u_ref[g].astype(BF16), k_ref[g]) + _dot(s.astype(BF16), v_ref[g])


def s5_outputs(u, st_f, st_b, k_bf16, v_bf16):
    g, r, tc = u.shape
    sw = STATE_W
    gb = 8
    tr = min(r, 512)
    return pl.pallas_call(
        _s5_out_kernel,
        grid=(g // gb, r // tr),
        in_specs=[pl.BlockSpec((gb, tr, tc), lambda i, j: (i, j, 0)),
                  pl.BlockSpec((tr, gb * sw), lambda i, j: (j, i)),
                  pl.BlockSpec((tr, gb * sw), lambda i, j: (j, i)),
                  pl.BlockSpec((gb, tc, tc), lambda i, j: (i, 0, 0)),
                  pl.BlockSpec((gb, 2 * sw, tc), lambda i, j: (i, 0, 0))],
        out_specs=pl.BlockSpec((gb, tr, tc), lambda i, j: (i, j, 0)),
        out_shape=jax.ShapeDtypeStruct((g, r, tc), F32),
        compiler_params=_cparams(("arbitrary", "arbitrary")),
        name="s5_outputs",
    )(u, st_f, st_b, k_bf16, v_bf16)


def s5_mixer(x, grp, nw, mod, rowmap, ops, s0):
    k_tot, w_tot, v_tot, a1, a2 = ops
    ng = k_tot.shape[0]
    bsz = grp.bsz
    nj = grp.seq // S5_CHUNK
    sw = STATE_W
    u4 = s5_to_chunks(x, grp, nw, mod, rowmap)
    u = u4.reshape(ng, nj * bsz, S5_TC)
    w = jnp.concatenate([w_tot[:, :, :sw], w_tot[:, :, 2 * sw:3 * sw], w_tot[:, :, sw:2 * sw], w_tot[:, :, 3 * sw:]], axis=-1)
    s0g = s0.transpose(3, 0, 1, 2, 4).reshape(ng, bsz, 2 * sw)
    st_f, st_b, final = s5_chunk_states_scan(u, w.astype(BF16), a1, a2, s0g, bsz)
    y = s5_outputs(u, st_f, st_b, k_tot.astype(BF16), v_tot.astype(BF16))
    final = final.reshape(ng, bsz, 2, 2, S5_STATE).transpose(1, 2, 3, 0, 4)
    return y.reshape(ng, nj, bsz * S5_TC), final


def _glu_kernel(yc_ref, yl_ref, x_ref, nw_ref, sc_ref, sh_ref, g1_ref, dsk_ref, wa_ref, wb_ref, o_ref, ybuf,
                *, first_tiles):
    gpl = LANES // S5_GROUP

    def from_chunks(y_ref):
        cpt = y_ref.shape[1]
        for c in range(ybuf.shape[0]):
            for t in range(S5_CHUNK):
                for g in range(gpl):
                    ybuf[c, t * cpt:(t + 1) * cpt, g * S5_GROUP:(g + 1) * S5_GROUP] = (
                        y_ref[c * gpl + g, :, t * S5_GROUP:(t + 1) * S5_GROUP])

    in_first = pl.program_id(0) < first_tiles

    @pl.when(in_first)
    def _():
        from_chunks(yc_ref)

    @pl.when(jnp.logical_not(in_first))
    def _():
        from_chunks(yl_ref)

    cpt = TOK_TILE // S5_CHUNK
    y = jnp.concatenate(
        [jnp.concatenate([ybuf[c, pl.ds(j, S5_CHUNK, stride=cpt), :] for c in range(ybuf.shape[0])], axis=1)
         for j in range(cpt)], axis=0)
    x = x_ref[...]
    y = y + dsk_ref[...] * _norm_mod(x, nw_ref[...], sc_ref[0], sh_ref[0])
    yb = jax.nn.gelu(y, approximate=True).astype(BF16)
    a = _dot(yb, wa_ref[...])
    b = _dot(yb, wb_ref[...])
    o_ref[...] = x + g1_ref[0] * (a * _sigmoid(b))


def glu_residual(y_chunks, x, nw, mod, rowmap, groups, dskip, wa_bf16, wb_bf16):
    n_tok, d = x.shape
    tm = TOK_TILE
    ng = d // S5_GROUP
    cpt = tm // S5_CHUNK
    first, second = groups
    blk = (ng, cpt, S5_TC)
    return pl.pallas_call(
        functools.partial(_glu_kernel, first_tiles=first.tiles),
        grid=(n_tok // tm,),
        in_specs=[pl.BlockSpec(blk, lambda i: (0, jnp.minimum(i, first.tiles - 1) % first.seq_tiles,
                                               jnp.minimum(i, first.tiles - 1) // first.seq_tiles)),
                  pl.BlockSpec(blk, lambda i: (0, jnp.maximum(i - first.tiles, 0) % second.seq_tiles,
                                               jnp.maximum(i - first.tiles, 0) // second.seq_tiles)),
                  pl.BlockSpec((tm, d), lambda i: (i, 0)),
                  pl.BlockSpec((1, d), lambda i: (0, 0)),
                  _mod_spec(rowmap, 1, d), _mod_spec(rowmap, 0, d), _mod_spec(rowmap, 2, d),
                  pl.BlockSpec((1, d), lambda i: (0, 0)),
                  pl.BlockSpec((d, d), lambda i: (0, 0)),
                  pl.BlockSpec((d, d), lambda i: (0, 0))],
        out_specs=pl.BlockSpec((tm, d), lambda i: (i, 0)),
        out_shape=jax.ShapeDtypeStruct((n_tok, d), F32),
        scratch_shapes=[pltpu.VMEM((d // LANES, tm, LANES), F32)],
        compiler_params=_cparams(("arbitrary",)),
        name="glu_residual",
    )(y_chunks[0], y_chunks[1], x, nw, mod, mod, mod, dskip, wa_bf16, wb_bf16)


def _router_kernel(x_ref, nw_ref, sc_ref, sh_ref, wr_ref, h_ref, aff_ref):
    d = x_ref.shape[1]
    tm = x_ref.shape[0]
    h = _norm_mod(x_ref[...], nw_ref[...], sc_ref[0], sh_ref[0])
    h_ref[:, :d] = h
    tok = pl.program_id(0) * tm + lax.broadcasted_iota(jnp.int32, (tm, LANES), 0)
    h_ref[:, d:] = tok.astype(F32)
    logits = _dot_nt(wr_ref[...], h.astype(BF16))
    ex = jnp.exp(logits - jnp.max(logits, axis=0, keepdims=True))
    p = ex / jnp.sum(ex, axis=0, keepdims=True)
    for k in range(aff_ref.shape[0]):
        aff_ref[k] = p[:, k * LANES:(k + 1) * LANES]


def moe_router(x, nw, mod, rowmap, wr_t_bf16):
    n_tok, d = x.shape
    tm = TOK_TILE
    ne = wr_t_bf16.shape[0]
    return pl.pallas_call(
        _router_kernel,
        grid=(n_tok // tm,),
        in_specs=[pl.BlockSpec((tm, d), lambda i: (i, 0)),
                  pl.BlockSpec((1, d), lambda i: (0, 0)),
                  _mod_spec(rowmap, 4, d), _mod_spec(rowmap, 3, d),
                  pl.BlockSpec((ne, d), lambda i: (0, 0))],
        out_specs=[pl.BlockSpec((tm, d + LANES), lambda i: (i, 0)),
                   pl.BlockSpec((tm // LANES, ne, LANES), lambda i: (i, 0, 0))],
        out_shape=[jax.ShapeDtypeStruct((n_tok, d + LANES), F32),
                   jax.ShapeDtypeStruct((n_tok // LANES, ne, LANES), F32)],
        compiler_params=_cparams(("arbitrary",)),
        name="moe_router",
    )(x, nw, mod, mod, wr_t_bf16)


def _select_kernel(aff_ref, ut_ref, v8_ref, cidx_ref, meta_ref, inc_ref, *, cap, first, slot0):
    nt, ne, _ = aff_ref.shape
    aff = aff_ref[...]

    def count(mask):
        c = jnp.sum(jnp.where(mask, 1.0, 0.0), axis=0)
        return jnp.sum(c, axis=1, keepdims=True)

    def as_float(bits):
        return lax.bitcast_convert_type(bits, F32)

    def radix(k, bits):
        cand = bits | (jnp.int32(1) << (30 - k))
        return jnp.where(count(aff >= as_float(cand)[None]) >= cap, cand, bits)

    thr_bits = lax.fori_loop(0, 31, radix, jnp.zeros((ne, 1), jnp.int32))
    thr = as_float(thr_bits)[None]
    nxt = as_float(thr_bits + 1)[None]
    above = aff >= nxt
    bucket = (aff >= thr) & jnp.logical_not(above)
    need = cap - count(above)
    width = nxt - thr
    pos = jnp.where(bucket & (width > 0.0), (aff - thr) / width, 0.0)

    def refine(k, t):
        cand = t + lax.convert_element_type(jnp.int32(1) << (29 - k), F32) * (2.0 ** -30)
        return jnp.where(count(bucket & (pos >= cand[None])) >= need, cand, t)

    t = lax.fori_loop(0, 30, refine, jnp.zeros((ne, 1), F32))
    upper = bucket & (pos >= (t + 2.0 ** -30)[None])
    tied = bucket & (pos >= t[None]) & jnp.logical_not(upper)
    ut = ut_ref[...]

    def excl_rank(mask):
        m = jnp.where(mask, 1.0, 0.0)
        inc_ref[...] = _dot(m.reshape(nt * ne, LANES).astype(BF16), ut).reshape(nt, ne, LANES)

        def body(tt, carry):
            inc = inc_ref[tt]
            inc_ref[tt] = inc + carry
            return carry + inc[:, LANES - 1:LANES]

        lax.fori_loop(0, nt, body, jnp.zeros((ne, 1), F32))
        return inc_ref[...] - m

    sel = above | upper | (tied & (excl_rank(tied) < (need - count(upper))[None]))
    rank = excl_rank(sel)
    start = rank[:, :, 0:1]
    nsel = rank[:, :, LANES - 1:LANES] + jnp.where(sel[:, :, LANES - 1:LANES], 1.0, 0.0) - start
    lane3 = lax.broadcasted_iota(jnp.int32, (nt, ne, LANES), 2)
    meta_ref[...] = jnp.where(lane3 == 0, start + float(slot0), jnp.where(lane3 == 1, nsel, 0.0)).astype(jnp.int32)

    inc_ref[...] = jnp.where(sel, rank - start, -1.0)
    sub = lax.broadcasted_iota(jnp.int32, (LANES, LANES), 0).astype(F32)
    v8 = v8_ref[...]

    def tile_body(tt, carry):
        rho = inc_ref[tt]
        base = lax.convert_element_type(first + tt * LANES, F32)
        for e in range(ne):
            onehot = jnp.where(sub == rho[e:e + 1, :], 1.0, 0.0).astype(BF16)
            packed = _dot_nt(v8, onehot)
            cidx_ref[tt, e:e + 1, :] = (packed[0:1] + base).astype(jnp.int32)
        return carry

    lax.fori_loop(0, nt, tile_body, 0)


def moe_select(aff_t, cap, first, slot0):
    nt, ne, _ = aff_t.shape
    ut = jnp.triu(jnp.ones((LANES, LANES), BF16))
    v8 = jnp.zeros((8, LANES), BF16).at[0].set(jnp.arange(LANES).astype(BF16))
    blk = pl.BlockSpec((nt, ne, LANES), lambda i: (0, 0, 0))
    return pl.pallas_call(
        functools.partial(_select_kernel, cap=cap, first=first, slot0=slot0),
        grid=(1,),
        in_specs=[blk, pl.BlockSpec((LANES, LANES), lambda i: (0, 0)), pl.BlockSpec((8, LANES), lambda i: (0, 0))],
        out_specs=[blk, blk],
        out_shape=[jax.ShapeDtypeStruct((nt, ne, LANES), jnp.int32),
                   jax.ShapeDtypeStruct((nt, ne, LANES), jnp.int32)],
        scratch_shapes=[pltpu.VMEM((nt, ne, LANES), F32)],
        compiler_params=_cparams(("arbitrary",)),
        name="moe_select",
    )(aff_t, ut, v8)


def _lists_kernel(starts_ref, counts_ref, cidx_ref, idx_ref):
    nt, ne, _ = cidx_ref.shape
    idx_ref[...] = jnp.zeros_like(idx_ref)
    lane = lax.broadcasted_iota(jnp.int32, (1, LANES), 1)

    def tile(t, carry):
        for e in range(ne):
            s = starts_ref[e, t]
            c = counts_ref[e, t]
            j0 = s >> 7
            o = s & (LANES - 1)
            rolled = pltpu.roll(cidx_ref[t, e:e + 1, :], o, axis=1)
            end = o + c
            row0 = idx_ref[e, pl.ds(j0, 1), :]
            idx_ref[e, pl.ds(j0, 1), :] = jnp.where((lane >= o) & (lane < end), rolled, row0)
            row1 = idx_ref[e, pl.ds(j0 + 1, 1), :]
            idx_ref[e, pl.ds(j0 + 1, 1), :] = jnp.where(lane < end - LANES, rolled, row1)
        return carry

    lax.fori_loop(0, nt, tile, 0)


def moe_build_lists(starts, counts, cidx, rows):
    nt, ne, _ = cidx.shape
    rt = rows // LANES + 2
    out = pl.pallas_call(
        _lists_kernel,
        grid_spec=pltpu.PrefetchScalarGridSpec(
            num_scalar_prefetch=2,
            grid=(1,),
            in_specs=[pl.BlockSpec((nt, ne, LANES), lambda i, s, c: (0, 0, 0))],
            out_specs=pl.BlockSpec((ne, rt, LANES), lambda i, s, c: (0, 0, 0))),
        out_shape=jax.ShapeDtypeStruct((ne, rt, LANES), jnp.int32),
        compiler_params=_cparams(("arbitrary",)),
        name="moe_build_lists",
    )(starts, counts, cidx)
    return out.reshape(ne * rt * LANES)


def _expert_kernel(idx_ref, h_hbm, wr_ref, wg_ref, wu_ref, wd_ref, y_ref, xbuf, wgb, wub, wdb, sem, *, rows_pad):
    e = pl.program_id(0)
    ch = pl.program_id(1)
    nch = pl.num_programs(1)
    tr = xbuf.shape[1]
    d, f = wgb.shape
    step = e * nch + ch
    slot = step % 2

    @pl.when(ch == 0)
    def _():
        wgb[...] = wg_ref[0, 0].astype(BF16)
        wub[...] = wu_ref[0, 0].astype(BF16)
        wdb[...] = wd_ref[0, 0].astype(BF16)

    def row_copy(base, r, slot_i):
        return pltpu.make_async_copy(h_hbm.at[pl.ds(idx_ref[base + r], 1)], xbuf.at[slot_i, pl.ds(r, 1)], sem.at[slot_i])

    @pl.when(step == 0)
    def _():
        def group(j, carry):
            for u in range(8):
                row_copy(0, 8 * j + u, 0).start()
            return carry

        lax.fori_loop(0, tr // 8, group, 0)

    has_next = step + 1 < pl.num_programs(0) * nch
    last = ch == nch - 1
    e_n = jnp.where(has_next, jnp.where(last, e + 1, e), e)
    ch_n = jnp.where(has_next, jnp.where(last, 0, ch + 1), ch)
    base_n = e_n * rows_pad + ch_n * tr

    pltpu.make_async_copy(h_hbm.at[pl.ds(0, tr)], xbuf.at[slot], sem.at[slot]).wait()

    x = xbuf[slot, :, :d].astype(BF16)
    logits = _dot(x, wr_ref[...])
    lane = lax.broadcasted_iota(jnp.int32, logits.shape, 1)
    ne = pl.num_programs(0)
    logits = jnp.where(lane < ne, logits, NEG_INF)
    ex = jnp.exp(logits - jnp.max(logits, axis=1, keepdims=True))
    gate = jnp.sum(jnp.where(lane == e, ex, 0.0), axis=1, keepdims=True) / jnp.sum(ex, axis=1, keepdims=True)

    fb = f // FFN_SPLIT
    rb = tr // FFN_SPLIT
    y = None
    for nb in range(FFN_SPLIT):
        for r in range(nb * rb, (nb + 1) * rb):
            row_copy(base_n, r, 1 - slot).start()
        cols = slice(nb * fb, (nb + 1) * fb)
        hmid = (_silu(_dot(x, wgb[:, cols])) * _dot(x, wub[:, cols])).astype(BF16)
        part = _dot(hmid, wdb[cols, :])
        y = part if y is None else y + part
    y_ref[0, :, :d] = y * gate
    y_ref[0, :, d:] = xbuf[slot, :, d:]

    @pl.when(jnp.logical_not(has_next))
    def _():
        pltpu.make_async_copy(h_hbm.at[pl.ds(0, tr)], xbuf.at[1 - slot], sem.at[1 - slot]).wait()


def moe_experts(idx, rows, h_ext, wr_pad_bf16, w_gate, w_up, w_down, layer):
    _, ne, d, f = w_gate.shape
    rows_pad = idx.shape[0] // ne
    dx = h_ext.shape[1]
    tr = 512
    return pl.pallas_call(
        functools.partial(_expert_kernel, rows_pad=rows_pad),
        grid_spec=pltpu.PrefetchScalarGridSpec(
            num_scalar_prefetch=1,
            grid=(ne, rows // tr),
            in_specs=[pl.BlockSpec(memory_space=pl.ANY),
                      pl.BlockSpec((d, LANES), lambda e, c, idx: (0, 0)),
                      pl.BlockSpec((1, 1, d, f), lambda e, c, idx: (layer, e, 0, 0)),
                      pl.BlockSpec((1, 1, d, f), lambda e, c, idx: (layer, e, 0, 0)),
                      pl.BlockSpec((1, 1, f, d), lambda e, c, idx: (layer, e, 0, 0))],
            out_specs=pl.BlockSpec((1, tr, dx), lambda e, c, idx: (e, c, 0)),
            scratch_shapes=[pltpu.VMEM((2, tr, dx), F32),
                            pltpu.VMEM((d, f), BF16), pltpu.VMEM((d, f), BF16), pltpu.VMEM((f, d), BF16),
                            pltpu.SemaphoreType.DMA((2,))]),
        out_shape=jax.ShapeDtypeStruct((ne, rows, dx), F32),
        compiler_params=_cparams(("arbitrary", "arbitrary")),
        name="moe_experts",
    )(idx, h_ext, wr_pad_bf16, w_gate, w_up, w_down)


def _combine_kernel(starts_ref, y_hbm, x_ref, g2_ref, *rest, first_tiles):
    if first_tiles is None:
        o_ref, acc, stage, sem = rest
    else:
        oc_ref, ol_ref, acc, stage, sem = rest
    tb = pl.program_id(0)
    tm, d = x_ref.shape
    ne = y_hbm.shape[0]
    slot = tb % 2

    def chunk_copy(e, src_row, dst_row, slot_i):
        return pltpu.make_async_copy(y_hbm.at[e, pl.ds(src_row, 8)], stage.at[slot_i, pl.ds(dst_row, 8)], sem.at[slot_i])

    def spans(tile):
        out = []
        for e in range(ne):
            s0 = starts_ref[e, tile]
            s1 = starts_ref[e, tile + 1]
            a = (s0 >> 3) << 3
            out.append((a, jnp.where(s1 > s0, (s1 - a + 7) >> 3, 0)))
        return out

    def fetch(tile, slot_i):
        off = jnp.int32(0)
        for e, (a, nchunk) in enumerate(spans(tile)):
            def issue(c, carry, e=e, a=a, off=off):
                chunk_copy(e, pl.multiple_of(a + 8 * c, 8), pl.multiple_of(off + 8 * c, 8), slot_i).start()
                return carry

            lax.fori_loop(0, nchunk, issue, 0)
            off = off + 8 * nchunk

    @pl.when(tb == 0)
    def _():
        stage[...] = jnp.zeros(stage.shape, F32)
        fetch(tb, slot)

    @pl.when(tb + 1 < pl.num_programs(0))
    def _():
        fetch(tb + 1, 1 - slot)

    off = jnp.int32(0)
    for _, nchunk in spans(tb):
        off = off + 8 * nchunk

    def drain(c, carry):
        chunk_copy(0, 0, 0, slot).wait()
        return carry

    lax.fori_loop(0, off >> 3, drain, 0)

    acc[...] = jnp.zeros_like(acc)
    want = (lax.broadcasted_iota(jnp.int32, (tm, KCH), 0) + tb * tm).astype(F32)

    def fold(kc, carry):
        rows = stage[slot, pl.ds(pl.multiple_of(kc * KCH, KCH), KCH), :]
        tok = rows[:, d:].T[0:1, :]
        fresh = lax.broadcasted_iota(jnp.int32, (tm, KCH), 1) + kc * KCH < off
        onehot = jnp.where((want == tok) & fresh, 1.0, 0.0).astype(BF16)
        y = rows[:, :d]
        hi = y.astype(BF16)
        lo = (y - hi.astype(F32)).astype(BF16)
        acc[...] += _dot(onehot, hi) + _dot(onehot, lo)
        return carry

    lax.fori_loop(0, (off + KCH - 1) >> 8, fold, 0)
    res = x_ref[...] + g2_ref[0] * acc[...]
    if first_tiles is None:
        o_ref[...] = res
    else:
        @pl.when(tb < first_tiles)
        def _():
            oc_ref[...] = res

        @pl.when(tb >= first_tiles)
        def _():
            ol_ref[...] = res


def moe_combine(starts, y, x, mod, rowmap, split=None):
    n_tok, d = x.shape
    ne, _, dx = y.shape
    tm = TOK_TILE
    stage_rows = -(-(ne * tm + ne * 16) // KCH) * KCH
    if split is None:
        out_specs = pl.BlockSpec((tm, d), lambda i, s: (i, 0))
        out_shape = jax.ShapeDtypeStruct((n_tok, d), F32)
        first_tiles = None
    else:
        first_tiles = split[0]
        out_specs = [pl.BlockSpec((tm, d), lambda i, s: (jnp.minimum(i, first_tiles - 1), 0)),
                     pl.BlockSpec((tm, d), lambda i, s: (jnp.maximum(i - first_tiles, 0), 0))]
        out_shape = [jax.ShapeDtypeStruct((split[0] * tm, d), F32), jax.ShapeDtypeStruct((split[1] * tm, d), F32)]
    return pl.pallas_call(
        functools.partial(_combine_kernel, first_tiles=first_tiles),
        grid_spec=pltpu.PrefetchScalarGridSpec(
            num_scalar_prefetch=1,
            grid=(n_tok // tm,),
            in_specs=[pl.BlockSpec(memory_space=pl.ANY),
                      pl.BlockSpec((tm, d), lambda i, s: (i, 0)),
                      pl.BlockSpec((1, 1, d), lambda i, s: (rowmap(i), 0, 5))],
            out_specs=out_specs,
            scratch_shapes=[pltpu.VMEM((tm, d), F32), pltpu.VMEM((2, stage_rows, dx), F32),
                            pltpu.SemaphoreType.DMA((2,))]),
        out_shape=out_shape,
        compiler_params=_cparams(("arbitrary",)),
        name="moe_combine",
    )(starts, y, x, mod)


def moe_layer(x, nw, mod, rowmap, groups, wr, w_gate, w_up, w_down, layer, split=False):
    n_tok, d = x.shape
    ne = wr.shape[1]
    h_ext, aff_t = moe_router(x, nw, mod, rowmap, wr.T.astype(BF16))
    cidx_parts, meta_parts = [], []
    rows = 0
    for grp in groups:
        cap = EC_FACTOR * grp.tokens // ne
        t0 = grp.row0 // LANES
        cidx, meta = moe_select(aff_t[t0:t0 + grp.tokens // LANES], cap, grp.row0, rows)
        cidx_parts.append(cidx)
        meta_parts.append(meta[:, :, :2])
        rows += cap
    cidx = jnp.concatenate(cidx_parts, axis=0)
    meta = jnp.concatenate(meta_parts, axis=0)
    starts = meta[:, :, 0].T
    counts = meta[:, :, 1].T
    idx = moe_build_lists(starts, counts, cidx, rows)
    per = TOK_TILE // LANES
    starts_blk = jnp.concatenate([starts[:, ::per], jnp.full((ne, 1), rows, jnp.int32)], axis=1)
    wr_pad = jnp.zeros((d, LANES), BF16).at[:, :ne].set(wr.astype(BF16))
    y = moe_experts(idx, rows, h_ext, wr_pad, w_gate, w_up, w_down, layer)
    return moe_combine(starts_blk, y, x, mod, rowmap, (groups[0].tiles, groups[1].tiles) if split else None)


def kernel(x_prompt, x_sample, cache_k, cache_v, state_hgrn, state_s5, c, c_ctx, norm_w, ada_w, ada_b, w_in_ab, w_out_ab, hgrn_lb_logits, hgrn_norm_w, q_norm_w, k_norm_w, attn_sink, s5_a_re, s5_a_im, s5_log_dt, s5_b_re, s5_b_im, s5_c_re, s5_c_im, s5_d, glu_w_a, glu_w_b, router_w, exp_w_gate, exp_w_up, exp_w_down):
    b_ctx, l_ctx, d = x_prompt.shape
    b_lat, l_lat, _ = x_sample.shape
    depth = norm_w.shape[0]
    ctx = Group(0, b_ctx, l_ctx)
    lat = Group(ctx.tokens, b_lat, l_lat)
    groups = (ctx, lat)

    def rowmap(i):
        return jnp.where(i < ctx.tiles, i // ctx.seq_tiles, b_ctx + (i - ctx.tiles) // lat.seq_tiles)

    cond = jnp.concatenate([c_ctx[None, :], c, jnp.zeros((8 - 1 - b_lat, d), F32)], axis=0)
    mod_small = ada_modulation(cond, ada_w, ada_b)
    seq_rows = jnp.concatenate([jnp.zeros((b_ctx,), jnp.int32), 1 + jnp.arange(b_lat, dtype=jnp.int32)])
    mods = mod_small[:, seq_rows][:, :, None, :]

    x = jnp.concatenate([x_prompt.reshape(ctx.tokens, d), x_sample.reshape(lat.tokens, d)], axis=0)
    rope = rope_tables(l_lat)
    ks, vs, hs, ss = [], [], [], []
    for l in range(depth):
        mod = mods[l]
        nw1 = norm_w[l, 0].reshape(1, d)
        nw2 = norm_w[l, 1].reshape(1, d)
        if l % 2 == 0:
            e = l // 2
            proj = norm_mod_matmul(x, nw1, mod, rowmap, 1, 0, w_in_ab[e].astype(BF16))
            zero_state = jnp.zeros((b_ctx, 2, A_HEADS, A_DK, A_DK), F32)
            of_c, ob_c, st_c = hgrn2_mixer(proj, ctx, hgrn_lb_logits, zero_state, e)
            of_l, ob_l, _ = hgrn2_mixer(proj, lat, hgrn_lb_logits, jnp.swapaxes(state_hgrn[:, e], -1, -2), e)
            hs.append(jnp.swapaxes(st_c, -1, -2))
            qn_c, kn_c = qk_prepare(proj, ctx, q_norm_w[e], k_norm_w[e], None)
            att_c = context_attention(qn_c, kn_c, proj, ctx, attn_sink[e])
            ks.append(kn_c.reshape(b_ctx, l_ctx, B_KV_HEADS, HEAD_DIM).transpose(0, 2, 1, 3))
            vs.append(proj[:ctx.tokens, V_COL:].reshape(b_ctx, l_ctx, B_KV_HEADS, HEAD_DIM).transpose(0, 2, 1, 3))
            qr_l, kr_l = qk_prepare(proj, lat, q_norm_w[e], k_norm_w[e], rope)
            att_l = latent_attention(qr_l, kr_l, proj, lat, cache_k[:, e], cache_v[:, e], attn_sink[e])
            x = even_out_proj((of_c, of_l), (ob_c, ob_l), (att_c, att_l), proj, x, mod, rowmap, groups,
                              hgrn_norm_w[e].reshape(1, A_DK), w_out_ab[e].astype(BF16))
        else:
            o = l // 2
            ops = s5_operators(s5_a_re[o], s5_a_im[o], s5_log_dt[o], s5_b_re[o], s5_b_im[o], s5_c_re[o], s5_c_im[o])
            zero_s5 = jnp.zeros((b_ctx, 2, 2, d // S5_GROUP, S5_STATE), F32)
            y_c, fin_c = s5_mixer(x, ctx, nw1, mod, rowmap, ops, zero_s5)
            y_l, _ = s5_mixer(x, lat, nw1, mod, rowmap, ops, state_s5[:, o])
            ss.append(fin_c)
            x = glu_residual((y_c, y_l), x, nw1, mod, rowmap, groups, s5_d[o].reshape(1, d),
                             glu_w_a[o].astype(BF16), glu_w_b[o].astype(BF16))
        x = moe_layer(x, nw2, mod, rowmap, groups, router_w[l], exp_w_gate, exp_w_up, exp_w_down, l,
                      split=(l == depth - 1))
    y_prompt = x[0].reshape(b_ctx, l_ctx, d)
    y_sample = x[1].reshape(b_lat, l_lat, d)
    return (y_prompt, y_sample, jnp.stack(ks, axis=1), jnp.stack(vs, axis=1),
            jnp.stack(hs, axis=1), jnp.stack(ss, axis=1))
```

```python
import functools

import jax
import jax.numpy as jnp
from jax import lax
from jax.experimental import pallas as pl
from jax.experimental.pallas import tpu as pltpu

F32 = jnp.float32
BF16 = jnp.bfloat16

A_HEADS = 4
A_DK = 128
A_WIDTH = A_HEADS * A_DK
B_HEADS = 8
B_KV_HEADS = 2
HEAD_DIM = 64
B_GROUP = B_HEADS // B_KV_HEADS
B_WIDTH = B_HEADS * HEAD_DIM
KV_WIDTH = B_KV_HEADS * HEAD_DIM
Q_COL = 5 * A_WIDTH
K_COL = Q_COL + B_WIDTH
V_COL = K_COL + KV_WIDTH
WINDOW = 128
GRID_W = 64
ROPE_THETA = 10000.0
S5_GROUP = 16
S5_STATE = 64
S5_CHUNK = 16
S5_TC = S5_CHUNK * S5_GROUP
N_EXPERTS = 16
EC_FACTOR = 2
EPS = 1e-6
NEG_INF = -1e30

HGRN_C = 128
HGRN_SB = 16
TOK_TILE = 256
LANES = 128
VMEM_LIMIT = 56 * 1024 * 1024
STATE_W = 2 * S5_STATE
KCH = 256
CPT = TOK_TILE // S5_CHUNK
FFN_SPLIT = 4


def _cparams(sem):
    return pltpu.CompilerParams(dimension_semantics=sem, vmem_limit_bytes=VMEM_LIMIT)


def _sigmoid(x):
    return 1.0 / (1.0 + jnp.exp(-x))


def _silu(x):
    return x * _sigmoid(x)


def _norm_mod(x, nw, sc, sh):
    ms = jnp.mean(x * x, axis=-1, keepdims=True)
    return (x * lax.rsqrt(ms + EPS) * nw) * (1.0 + sc) + sh


def _dot(a, b):
    return jnp.dot(a, b, preferred_element_type=F32)


def _dot_nt(a, b):
    return lax.dot_general(a, b, (((1,), (1,)), ((), ())), preferred_element_type=F32)


def _split3(x):
    hi = x.astype(BF16)
    r1 = x - hi.astype(F32)
    mid = r1.astype(BF16)
    lo = (r1 - mid.astype(F32)).astype(BF16)
    return hi, mid, lo


class Group:
    def __init__(self, row0, bsz, seq):
        self.row0, self.bsz, self.seq = row0, bsz, seq
        self.tokens = bsz * seq
        self.tile0 = row0 // TOK_TILE
        self.tiles = self.tokens // TOK_TILE
        self.seq_tiles = seq // TOK_TILE


def _ada_kernel(c_ref, w_ref, b_ref, o_ref):
    s = _silu(c_ref[...])
    o_ref[0] = _dot(s.astype(BF16), w_ref[0].astype(BF16)) + b_ref[0]


def ada_modulation(cond, ada_w, ada_b):
    depth, d, n = ada_w.shape
    rows = cond.shape[0]
    tn = 1536
    return pl.pallas_call(
        _ada_kernel,
        grid=(depth, n // tn),
        in_specs=[pl.BlockSpec((rows, d), lambda l, j: (0, 0)),
                  pl.BlockSpec((1, d, tn), lambda l, j: (l, 0, j)),
                  pl.BlockSpec((1, 1, tn), lambda l, j: (l, 0, j))],
        out_specs=pl.BlockSpec((1, rows, tn), lambda l, j: (l, 0, j)),
        out_shape=jax.ShapeDtypeStruct((depth, rows, n), F32),
        compiler_params=_cparams(("arbitrary", "arbitrary")),
        name="ada_modulation",
    )(cond, ada_w, ada_b.reshape(depth, 1, n))


def _mod_spec(rowmap, k, d):
    return pl.BlockSpec((1, 1, d), lambda i: (rowmap(i), 0, k))


def _inproj_kernel(x_ref, nw_ref, sc_ref, sh_ref, w_ref, o_ref):
    h = _norm_mod(x_ref[...], nw_ref[...], sc_ref[0], sh_ref[0])
    o_ref[...] = _dot(h.astype(BF16), w_ref[...])


def norm_mod_matmul(x, nw, mod, rowmap, k_sc, k_sh, w_bf16):
    n_tok, d = x.shape
    n = w_bf16.shape[1]
    tm = TOK_TILE
    return pl.pallas_call(
        _inproj_kernel,
        grid=(n_tok // tm,),
        in_specs=[pl.BlockSpec((tm, d), lambda i: (i, 0)),
                  pl.BlockSpec((1, d), lambda i: (0, 0)),
                  _mod_spec(rowmap, k_sc, d), _mod_spec(rowmap, k_sh, d),
                  pl.BlockSpec((d, n), lambda i: (0, 0))],
        out_specs=pl.BlockSpec((tm, n), lambda i: (i, 0)),
        out_shape=jax.ShapeDtypeStruct((n_tok, n), F32),
        compiler_params=_cparams(("arbitrary",)),
        name="norm_mod_matmul",
    )(x, nw, mod, mod, w_bf16)


def _two_group_specs(groups, shape_of):
    first, second = groups
    return [pl.BlockSpec(shape_of, lambda i: (jnp.minimum(i, first.tiles - 1), 0)),
            pl.BlockSpec(shape_of, lambda i: (jnp.maximum(i - first.tiles, 0), 0))]


def _outproj_kernel(ofc_ref, ofl_ref, obc_ref, obl_ref, atc_ref, atl_ref, ga_ref, x_ref, g1_ref, hw_ref, w_ref,
                    o_ref, *, first_tiles):
    in_first = pl.program_id(0) < first_tiles
    o = jnp.where(in_first, ofc_ref[...] + obc_ref[...], ofl_ref[...] + obl_ref[...])
    o_att = jnp.where(in_first, atc_ref[...], atl_ref[...])
    gate = _silu(ga_ref[...])
    hw = hw_ref[...]
    parts = []
    for h in range(A_HEADS):
        sl = slice(h * A_DK, (h + 1) * A_DK)
        oh = o[:, sl]
        ms = jnp.mean(oh * oh, axis=-1, keepdims=True)
        parts.append(((oh * lax.rsqrt(ms + EPS) * hw) * gate[:, sl]).astype(BF16))
    parts.append(o_att.astype(BF16))
    y = _dot(jnp.concatenate(parts, axis=1), w_ref[...])
    o_ref[...] = x_ref[...] + g1_ref[0] * y


def even_out_proj(o_f, o_b, o_att, proj, x, mod, rowmap, groups, hw, w_bf16):
    n_tok, d = x.shape
    tm = TOK_TILE
    aw = A_WIDTH
    return pl.pallas_call(
        functools.partial(_outproj_kernel, first_tiles=groups[0].tiles),
        grid=(n_tok // tm,),
        in_specs=_two_group_specs(groups, (tm, aw)) + _two_group_specs(groups, (tm, aw))
                 + _two_group_specs(groups, (tm, B_WIDTH))
                 + [pl.BlockSpec((tm, aw), lambda i: (i, 4)),
                    pl.BlockSpec((tm, d), lambda i: (i, 0)),
                    _mod_spec(rowmap, 2, d),
                    pl.BlockSpec((1, A_DK), lambda i: (0, 0)),
                    pl.BlockSpec((aw + B_WIDTH, d), lambda i: (0, 0))],
        out_specs=pl.BlockSpec((tm, d), lambda i: (i, 0)),
        out_shape=jax.ShapeDtypeStruct((n_tok, d), F32),
        compiler_params=_cparams(("arbitrary",)),
        name="even_out_proj",
    )(o_f[0], o_f[1], o_b[0], o_b[1], o_att[0], o_att[1], proj, x, mod, hw, w_bf16)


def _hgrn_chunk(q, k, v, g, st, msel, rev):
    c = q.shape[0]
    nb = c // HGRN_SB
    row = lax.broadcasted_iota(jnp.int32, (c, c), 0)
    col = lax.broadcasted_iota(jnp.int32, (c, c), 1)
    tri = jnp.where((col >= row) if rev else (col <= row), 1.0, 0.0).astype(BF16)
    gh, gm, gl = _split3(g)
    b = _dot(tri, gh) + _dot(tri, gm) + _dot(tri, gl)
    b_edge = b[0:1] if rev else b[c - 1:c]
    qs = q * jnp.exp(b)
    kdec = k * jnp.exp(b_edge - b)

    lk = jnp.log(jnp.maximum(k, 0.0))
    half = HGRN_SB // 2
    zero_half = jnp.zeros((half, A_DK), F32)
    slabs = []
    for i in range(nb):
        sl = slice(i * HGRN_SB, (i + 1) * HGRN_SB)
        bi, qi, lki = b[sl], q[sl], lk[sl]
        ci = bi - lki
        pieces = []
        for s in range(HGRN_SB):
            s_half = s // half
            halves = []
            for hh in range(2):
                rows = slice(hh * half, (hh + 1) * half)
                if (hh > s_half) if rev else (hh < s_half):
                    halves.append(zero_half)
                    continue
                d = bi[rows] - ci[s:s + 1]
                if hh == s_half:
                    d = jnp.minimum(d, lki[s:s + 1])
                halves.append(qi[rows] * jnp.exp(d))
            pieces.append(jnp.concatenate(halves, axis=0).astype(BF16))
        slabs.append(jnp.concatenate(pieces, axis=1))
    a_loc = _dot(jnp.concatenate(slabs, axis=0), msel)

    lane = lax.broadcasted_iota(jnp.int32, (HGRN_SB, LANES), 1)
    rloc = lax.broadcasted_iota(jnp.int32, (HGRN_SB, LANES), 0)
    dmask = ((lane >= rloc) & (lane < HGRN_SB)) if rev else (lane <= rloc)
    krow = lax.broadcasted_iota(jnp.int32, (c, A_DK), 0)
    att_rows = []
    for i in range(nb):
        sl = slice(i * HGRN_SB, (i + 1) * HGRN_SB)
        a_d = jnp.where(dmask, a_loc[sl], 0.0)
        if i > 0:
            a_d = pltpu.roll(a_d, i * HGRN_SB, axis=1)
        a_i = a_d[:, :c]
        has_off = (i < nb - 1) if rev else (i > 0)
        if has_off:
            edge = (i + 1) * HGRN_SB if rev else i * HGRN_SB
            r = b[edge:edge + 1] if rev else b[edge - 1:edge]
            qp = q[sl] * jnp.exp(b[sl] - r)
            live = (krow >= edge) if rev else (krow < edge)
            kp = jnp.where(live, k * jnp.exp(jnp.minimum(r - b, 0.0)), 0.0)
            a_i = a_i + _dot_nt(qp.astype(BF16), kp.astype(BF16))
        att_rows.append(a_i)
    att = jnp.concatenate(att_rows, axis=0)

    vb = v.astype(BF16)
    o = _dot(att.astype(BF16), vb) + _dot_nt(qs.astype(BF16), st.astype(BF16))
    st_new = st * jnp.exp(b_edge) + _dot(v.T.astype(BF16), kdec.astype(BF16))
    return o, st_new


def _hgrn_kernel(qf_ref, vf_ref, ff_ref, qb_ref, vb_ref, fb_ref, lbl_ref, msel_ref, s0_ref,
                 of_ref, ob_ref, sout_ref, st_ref, *, layer):
    c_idx = pl.program_id(1)

    @pl.when(c_idx == 0)
    def _():
        st_ref[...] = s0_ref[0]

    lg = lbl_ref[...]
    ex = jnp.exp(lg - jnp.max(lg, axis=0, keepdims=True))
    pr = ex / jnp.sum(ex, axis=0, keepdims=True)
    lb = jnp.zeros_like(pr[0])
    for e in range(1, layer + 1):
        lb = lb + pr[e]
    msel = msel_ref[...]

    for d, (q_ref, v_ref, f_ref, o_ref) in enumerate(((qf_ref, vf_ref, ff_ref, of_ref),
                                                       (qb_ref, vb_ref, fb_ref, ob_ref))):
        q_all = _silu(q_ref[...])
        v_all = v_ref[...]
        lbd = lb[d:d + 1]
        forget = lbd + (1.0 - lbd) * _sigmoid(f_ref[...])
        k_all = 1.0 - forget
        g_all = jnp.log(forget)
        for h in range(A_HEADS):
            sl = slice(h * A_DK, (h + 1) * A_DK)
            o, st_new = _hgrn_chunk(q_all[:, sl], k_all[:, sl], v_all[:, sl], g_all[:, sl],
                                    st_ref[d, h], msel, rev=(d == 1))
            o_ref[:, sl] = o
            st_ref[d, h] = st_new

    @pl.when(c_idx == pl.num_programs(1) - 1)
    def _():
        sout_ref[0] = st_ref[...]


def hgrn2_mixer(proj, grp, lb_logits, s0t, layer):
    c = HGRN_C
    nc = grp.seq // c
    blk0 = grp.row0 // c
    aw = A_WIDTH
    msel = jnp.repeat(jnp.eye(HGRN_SB, LANES, dtype=BF16), A_DK, axis=0)

    def fwd(col):
        return pl.BlockSpec((c, aw), lambda b, i: (blk0 + b * nc + i, col))

    def bwd(col):
        return pl.BlockSpec((c, aw), lambda b, i: (blk0 + b * nc + nc - 1 - i, col))

    st_spec = pl.BlockSpec((1, 2, A_HEADS, A_DK, A_DK), lambda b, i: (b, 0, 0, 0, 0))
    return pl.pallas_call(
        functools.partial(_hgrn_kernel, layer=layer),
        grid=(grp.bsz, nc),
        in_specs=[fwd(0), fwd(3), fwd(1), bwd(0), bwd(3), bwd(2),
                  pl.BlockSpec(lb_logits.shape, lambda b, i: (0, 0, 0)),
                  pl.BlockSpec(msel.shape, lambda b, i: (0, 0)),
                  st_spec],
        out_specs=[pl.BlockSpec((c, aw), lambda b, i: (b * nc + i, 0)),
                   pl.BlockSpec((c, aw), lambda b, i: (b * nc + nc - 1 - i, 0)),
                   st_spec],
        out_shape=[jax.ShapeDtypeStruct((grp.tokens, aw), F32),
                   jax.ShapeDtypeStruct((grp.tokens, aw), F32),
                   jax.ShapeDtypeStruct((grp.bsz, 2, A_HEADS, A_DK, A_DK), F32)],
        scratch_shapes=[pltpu.VMEM((2, A_HEADS, A_DK, A_DK), F32)],
        compiler_params=_cparams(("arbitrary", "arbitrary")),
        name="hgrn2_mixer",
    )(proj, proj, proj, proj, proj, proj, lb_logits, msel, s0t)


def _head_norm(x, w, gmat):
    hi, mid, lo = _split3(x * x)
    ms = _dot(hi, gmat) + _dot(mid, gmat) + _dot(lo, gmat)
    return x * lax.rsqrt(ms + EPS) * w


def _rope(x, cos, sin_signed):
    width = x.shape[1]
    lane = lax.broadcasted_iota(jnp.int32, x.shape, 1)
    nxt = pltpu.roll(x, width - 1, axis=1)
    prv = pltpu.roll(x, 1, axis=1)
    partner = jnp.where(lane % 2 == 0, nxt, prv)
    return x * cos + partner * sin_signed


def _qkprep_kernel(*refs, rope):
    if rope:
        q_ref, k_ref, qw_ref, kw_ref, gm_ref, cos_ref, sin_ref, qo_ref, ko_ref = refs
    else:
        q_ref, k_ref, qw_ref, kw_ref, gm_ref, qo_ref, ko_ref = refs
    gm = gm_ref[...]
    qn = _head_norm(q_ref[...], qw_ref[...], gm)
    kn = _head_norm(k_ref[...], kw_ref[...], gm[:KV_WIDTH, :KV_WIDTH])
    if rope:
        cos = cos_ref[...]
        sin = sin_ref[...]
        qn = _rope(qn, cos, sin)
        kn = _rope(kn, cos[:, :KV_WIDTH], sin[:, :KV_WIDTH])
    qo_ref[...] = (qn * (HEAD_DIM ** -0.5)).astype(BF16)
    ko_ref[...] = kn


def qk_prepare(proj, grp, qw, kw, rope_tabs):
    tm = TOK_TILE
    st = grp.seq_tiles
    gidx = jnp.arange(B_WIDTH) // HEAD_DIM
    gmat = jnp.where(gidx[:, None] == gidx[None, :], 1.0 / HEAD_DIM, 0.0).astype(BF16)
    qw_t = jnp.tile(qw, B_HEADS).reshape(1, B_WIDTH)
    kw_t = jnp.tile(kw, B_KV_HEADS).reshape(1, KV_WIDTH)
    in_specs = [pl.BlockSpec((tm, B_WIDTH), lambda i: (grp.tile0 + i, Q_COL // B_WIDTH)),
                pl.BlockSpec((tm, KV_WIDTH), lambda i: (grp.tile0 + i, K_COL // KV_WIDTH)),
                pl.BlockSpec((1, B_WIDTH), lambda i: (0, 0)),
                pl.BlockSpec((1, KV_WIDTH), lambda i: (0, 0)),
                pl.BlockSpec((B_WIDTH, B_WIDTH), lambda i: (0, 0))]
    args = [proj, proj, qw_t, kw_t, gmat]
    if rope_tabs is not None:
        in_specs += [pl.BlockSpec((tm, B_WIDTH), lambda i: (i % st, 0)),
                     pl.BlockSpec((tm, B_WIDTH), lambda i: (i % st, 0))]
        args += list(rope_tabs)
    return pl.pallas_call(
        functools.partial(_qkprep_kernel, rope=rope_tabs is not None),
        grid=(grp.tiles,),
        in_specs=in_specs,
        out_specs=[pl.BlockSpec((tm, B_WIDTH), lambda i: (i, 0)),
                   pl.BlockSpec((tm, KV_WIDTH), lambda i: (i, 0))],
        out_shape=[jax.ShapeDtypeStruct((grp.tokens, B_WIDTH), BF16),
                   jax.ShapeDtypeStruct((grp.tokens, KV_WIDTH), F32)],
        compiler_params=_cparams(("arbitrary",)),
        name="qk_prepare",
    )(*args)


def rope_tables(seq):
    pos = jnp.arange(seq)
    row = (pos // GRID_W).astype(F32)
    col = (pos % GRID_W).astype(F32)
    n_pair = HEAD_DIM // 4
    freqs = ROPE_THETA ** (-jnp.arange(n_pair, dtype=F32) / n_pair)
    ang = jnp.concatenate([row[:, None] * freqs, col[:, None] * freqs], axis=-1)
    cos = jnp.repeat(jnp.cos(ang), 2, axis=-1)
    sin = jnp.repeat(jnp.sin(ang), 2, axis=-1) * jnp.tile(jnp.array([-1.0, 1.0], F32), HEAD_DIM // 2)
    return jnp.tile(cos, (1, B_HEADS)), jnp.tile(sin, (1, B_HEADS))


def _value_with_ones(v2, kvh):
    lane = lax.broadcasted_iota(jnp.int32, v2.shape, 1)
    if kvh == 1:
        v2 = pltpu.roll(v2, HEAD_DIM, axis=1)
    return jnp.where(lane < HEAD_DIM, v2, 1.0).astype(BF16)


def _group_attention(q, kk, vv1, valid, sink_ref, kvh, o_ref):
    tq = q.shape[0]
    heads = [kvh * B_GROUP + gq for gq in range(B_GROUP)]
    qs = jnp.concatenate([q[:, h * HEAD_DIM:(h + 1) * HEAD_DIM] for h in heads], axis=0)
    sink = jnp.concatenate([jnp.broadcast_to(sink_ref[h:h + 1, 0:1], (tq, 1)) for h in heads], axis=0)
    s = _dot_nt(qs, kk)
    if valid is not None:
        s = jnp.where(jnp.concatenate([valid] * B_GROUP, axis=0), s, NEG_INF)
    m = jnp.maximum(jnp.max(s, axis=1, keepdims=True), sink)
    pv = _dot(jnp.exp(s - m).astype(BF16), vv1)
    den = pv[:, HEAD_DIM:HEAD_DIM + 1] + jnp.exp(sink - m)
    o = pv[:, :HEAD_DIM] / den
    for gq, h in enumerate(heads):
        o_ref[:, h * HEAD_DIM:(h + 1) * HEAD_DIM] = o[gq * tq:(gq + 1) * tq]


def _ctx_attn_kernel(q_ref, k_ref, v_ref, sink_ref, o_ref):
    q = q_ref[...]
    k = k_ref[...].astype(BF16)
    v = v_ref[...]
    for kvh in range(B_KV_HEADS):
        ks = slice(kvh * HEAD_DIM, (kvh + 1) * HEAD_DIM)
        _group_attention(q, k[:, ks], _value_with_ones(v, kvh), None, sink_ref, kvh, o_ref)


def context_attention(qn, kn, proj, grp, sink):
    seq = grp.seq
    blk0 = grp.row0 // seq
    sink_t = jnp.broadcast_to(sink.reshape(B_HEADS, 1), (B_HEADS, LANES))
    return pl.pallas_call(
        _ctx_attn_kernel,
        grid=(grp.bsz,),
        in_specs=[pl.BlockSpec((seq, B_WIDTH), lambda b: (b, 0)),
                  pl.BlockSpec((seq, KV_WIDTH), lambda b: (b, 0)),
                  pl.BlockSpec((seq, KV_WIDTH), lambda b: (blk0 + b, V_COL // KV_WIDTH)),
                  pl.BlockSpec((B_HEADS, LANES), lambda b: (0, 0))],
        out_specs=pl.BlockSpec((seq, B_WIDTH), lambda b: (b, 0)),
        out_shape=jax.ShapeDtypeStruct((grp.tokens, B_WIDTH), F32),
        compiler_params=_cparams(("arbitrary",)),
        name="context_attention",
    )(qn, kn, proj, sink_t)


def _lat_attn_kernel(q_ref, kp_ref, kc_ref, kn_ref, vp_ref, vc_ref, vn_ref, kx_ref, vx_ref, sink_ref, o_ref):
    blk = pl.program_id(1)
    nblk = pl.num_programs(1)
    tq = q_ref.shape[0]
    q = q_ref[...]
    kl = jnp.concatenate([kp_ref[...], kc_ref[...], kn_ref[...]], axis=0).astype(BF16)
    vl = jnp.concatenate([vp_ref[...], vc_ref[...], vn_ref[...]], axis=0)
    n_ctx = kx_ref.shape[2]
    span = 3 * tq
    i = lax.broadcasted_iota(jnp.int32, (tq, span + n_ctx), 0)
    j = lax.broadcasted_iota(jnp.int32, (tq, span + n_ctx), 1)
    dist = j - tq - i
    valid = (dist >= -WINDOW) & (dist <= WINDOW)
    valid = valid & ((j >= tq) | (blk > 0)) & ((j < 2 * tq) | (blk < nblk - 1))
    valid = valid | (j >= span)
    for kvh in range(B_KV_HEADS):
        ks = slice(kvh * HEAD_DIM, (kvh + 1) * HEAD_DIM)
        kk = jnp.concatenate([kl[:, ks], kx_ref[0, kvh].astype(BF16)], axis=0)
        vv1 = jnp.concatenate([_value_with_ones(vl, kvh), vx_ref[0, kvh].astype(BF16)], axis=0)
        _group_attention(q, kk, vv1, valid, sink_ref, kvh, o_ref)


def latent_attention(qr, kr, proj, grp, k_ctx, v_ctx, sink):
    tq = WINDOW
    nblk = grp.seq // tq
    blk0 = grp.row0 // tq
    n_ctx = k_ctx.shape[2]
    sink_t = jnp.broadcast_to(sink.reshape(B_HEADS, 1), (B_HEADS, LANES))
    v_ctx1 = jnp.concatenate([v_ctx, jnp.ones_like(v_ctx)], axis=-1)

    def kv_specs(off, col):
        return [pl.BlockSpec((tq, KV_WIDTH), lambda b, i: (off + b * nblk + jnp.maximum(i - 1, 0), col)),
                pl.BlockSpec((tq, KV_WIDTH), lambda b, i: (off + b * nblk + i, col)),
                pl.BlockSpec((tq, KV_WIDTH), lambda b, i: (off + b * nblk + jnp.minimum(i + 1, nblk - 1), col))]

    return pl.pallas_call(
        _lat_attn_kernel,
        grid=(grp.bsz, nblk),
        in_specs=[pl.BlockSpec((tq, B_WIDTH), lambda b, i: (b * nblk + i, 0))]
                 + kv_specs(0, 0) + kv_specs(blk0, V_COL // KV_WIDTH)
                 + [pl.BlockSpec((1, B_KV_HEADS, n_ctx, HEAD_DIM), lambda b, i: (b, 0, 0, 0)),
                    pl.BlockSpec((1, B_KV_HEADS, n_ctx, 2 * HEAD_DIM), lambda b, i: (b, 0, 0, 0)),
                    pl.BlockSpec((B_HEADS, LANES), lambda b, i: (0, 0))],
        out_specs=pl.BlockSpec((tq, B_WIDTH), lambda b, i: (b * nblk + i, 0)),
        out_shape=jax.ShapeDtypeStruct((grp.tokens, B_WIDTH), F32),
        compiler_params=_cparams(("arbitrary", "arbitrary")),
        name="latent_attention",
    )(qr, kr, kr, kr, proj, proj, proj, k_ctx, v_ctx1, sink_t)


def s5_operators(a_re, a_im, log_dt, b_re, b_im, c_re, c_im):
    t = S5_CHUNK
    hi = lax.Precision.HIGHEST
    ks, ws, wsw, vs, a1s, a2s = [], [], [], [], [], []
    for d in range(2):
        are, aim = a_re[d].astype(F32), a_im[d].astype(F32)
        dt = jnp.exp(log_dt[d].astype(F32))[:, None]
        den = are * are + aim * aim
        steps = jnp.arange(t + 1, dtype=F32)[:, None, None]
        mag = jnp.exp(steps * (dt * are))
        pw_re = mag * jnp.cos(steps * (dt * aim))
        pw_im = mag * jnp.sin(steps * (dt * aim))
        ab_re, ab_im = pw_re[1], pw_im[1]
        f_re = ((ab_re - 1.0) * are + ab_im * aim) / den
        f_im = (ab_im * are - (ab_re - 1.0) * aim) / den
        bre, bim = b_re[d].astype(F32), b_im[d].astype(F32)
        bb_re = f_re[..., None] * bre - f_im[..., None] * bim
        bb_im = f_re[..., None] * bim + f_im[..., None] * bre
        cre, cim = c_re[d].astype(F32), c_im[d].astype(F32)
        pgr = pw_re.transpose(1, 0, 2)[:, :, None, :]
        pgi = pw_im.transpose(1, 0, 2)[:, :, None, :]
        cp_re = cre[:, None] * pgr - cim[:, None] * pgi
        cp_im = cre[:, None] * pgi + cim[:, None] * pgr
        m = (jnp.einsum('gkcp,gpd->gkcd', cp_re[:, :t], bb_re, precision=hi)
             - jnp.einsum('gkcp,gpd->gkcd', cp_im[:, :t], bb_im, precision=hi))
        s_i = jnp.arange(t)[:, None]
        t_i = jnp.arange(t)[None, :]
        lag = (t_i - s_i) if d == 0 else (s_i - t_i)
        blk = jnp.where((lag >= 0)[None, :, :, None, None], m[:, jnp.clip(lag, 0, t - 1)], 0.0)
        ks.append(blk.transpose(0, 1, 4, 2, 3).reshape(-1, S5_TC, S5_TC))
        pidx = (t - 1 - jnp.arange(t)) if d == 0 else jnp.arange(t)
        pr = pw_re[pidx].transpose(1, 0, 2)[:, :, None, :]
        pi = pw_im[pidx].transpose(1, 0, 2)[:, :, None, :]
        bbr = bb_re.transpose(0, 2, 1)[:, None]
        bbi = bb_im.transpose(0, 2, 1)[:, None]
        w_re = pr * bbr - pi * bbi
        w_im = pr * bbi + pi * bbr
        ws.append(jnp.concatenate([w_re, w_im], axis=-1).reshape(-1, S5_TC, 2 * S5_STATE))
        wsw.append(jnp.concatenate([w_im, w_re], axis=-1).reshape(-1, S5_TC, 2 * S5_STATE))
        kidx = (jnp.arange(t) + 1) if d == 0 else (t - jnp.arange(t))
        v = jnp.concatenate([cp_re[:, kidx], -cp_im[:, kidx]], axis=-1)
        vs.append(v.transpose(0, 3, 1, 2).reshape(-1, 2 * S5_STATE, S5_TC))
        a1s.append(jnp.concatenate([pw_re[t], pw_re[t]], axis=-1))
        a2s.append(jnp.concatenate([-pw_im[t], pw_im[t]], axis=-1))
    return (ks[0] + ks[1], jnp.concatenate(ws + wsw, axis=-1), jnp.concatenate(vs, axis=1),
            jnp.concatenate(a1s, axis=-1), jnp.concatenate(a2s, axis=-1))


def _to_chunks_kernel(x_ref, nw_ref, sc_ref, sh_ref, o_ref, hbuf):
    h = _norm_mod(x_ref[...], nw_ref[...], sc_ref[0], sh_ref[0])
    gpl = LANES // S5_GROUP
    for c in range(hbuf.shape[0]):
        hbuf[c] = h[:, c * LANES:(c + 1) * LANES]
    for c in range(hbuf.shape[0]):
        for t in range(S5_CHUNK):
            rows = hbuf[c, pl.ds(t, TOK_TILE // S5_CHUNK, stride=S5_CHUNK), :]
            for g in range(gpl):
                o_ref[c * gpl + g, :, t * S5_GROUP:(t + 1) * S5_GROUP] = rows[:, g * S5_GROUP:(g + 1) * S5_GROUP]


def s5_to_chunks(x, grp, nw, mod, rowmap):
    d = x.shape[1]
    ng = d // S5_GROUP
    tm = TOK_TILE
    st = grp.seq_tiles

    def mspec(k):
        return pl.BlockSpec((1, 1, d), lambda i: (rowmap(grp.tile0 + i), 0, k))

    return pl.pallas_call(
        _to_chunks_kernel,
        grid=(grp.tiles,),
        in_specs=[pl.BlockSpec((tm, d), lambda i: (grp.tile0 + i, 0)),
                  pl.BlockSpec((1, d), lambda i: (0, 0)), mspec(1), mspec(0)],
        out_specs=pl.BlockSpec((ng, CPT, S5_TC), lambda i: (0, (i % st) * grp.bsz + i // st, 0)),
        out_shape=jax.ShapeDtypeStruct((ng, grp.tokens // S5_CHUNK, S5_TC), F32),
        scratch_shapes=[pltpu.VMEM((d // LANES, tm, LANES), F32)],
        compiler_params=_cparams(("arbitrary",)),
        name="s5_to_chunks",
    )(x, nw, mod, mod)


def _s5_states_kernel(uf_ref, ub_ref, w_ref, a1_ref, a2_ref, s0_ref, stf_ref, stb_ref, fin_ref, dbuf, carry, *, bsz):
    r = pl.program_id(1)
    gb, tr, _ = uf_ref.shape
    sw = STATE_W
    span = CPT * bsz
    nspan = tr // span

    @pl.when(r == 0)
    def _():
        for g in range(gb):
            s0 = s0_ref[g]
            for d in range(2):
                s = s0[:, d * sw:(d + 1) * sw]
                carry[g, 2 * d] = s
                carry[g, 2 * d + 1] = pltpu.roll(s, sw // 2, axis=1)

    for g in range(gb):
        w = w_ref[g]
        for d, u_ref in enumerate((uf_ref, ub_ref)):
            dd = _dot(u_ref[g].astype(BF16), w[:, 2 * d * sw:2 * (d + 1) * sw])
            dbuf[g, d, 0] = dd[:, :sw]
            dbuf[g, d, 1] = dd[:, sw:]

    for g in range(gb):
        a1 = a1_ref[g]
        a2 = a2_ref[g]
        for d, st_ref in enumerate((stf_ref, stb_ref)):
            a1d = a1[:, d * sw:(d + 1) * sw]
            a2d = a2[:, d * sw:(d + 1) * sw]
            s = carry[g, 2 * d]
            x = carry[g, 2 * d + 1]
            for c in range(nspan * CPT):
                cc = c if d == 0 else nspan * CPT - 1 - c
                first = (cc // CPT) * span + cc % CPT
                for b in range(bsz):
                    st_ref[first + b * CPT:first + b * CPT + 1, g * sw:(g + 1) * sw] = s[b:b + 1]
                own = dbuf[g, d, 0, pl.ds(first, bsz, stride=CPT), :]
                swp = dbuf[g, d, 1, pl.ds(first, bsz, stride=CPT), :]
                s, x = a1d * s + a2d * x + own, a1d * x - a2d * s + swp
            carry[g, 2 * d] = s
            carry[g, 2 * d + 1] = x

    @pl.when(r == pl.num_programs(1) - 1)
    def _():
        for g in range(gb):
            fin_ref[g, :, :sw] = carry[g, 0]
            fin_ref[g, :, sw:] = carry[g, 2]


def s5_chunk_states_scan(u, w_bf16, a1, a2, s0g, bsz):
    g, r, tc = u.shape
    n = w_bf16.shape[2]
    sw = STATE_W
    gb = 8
    tr = min(r, 256)
    nblk = r // tr
    return pl.pallas_call(
        functools.partial(_s5_states_kernel, bsz=bsz),
        grid=(g // gb, nblk),
        in_specs=[pl.BlockSpec((gb, tr, tc), lambda i, j: (i, j, 0)),
                  pl.BlockSpec((gb, tr, tc), lambda i, j: (i, nblk - 1 - j, 0)),
                  pl.BlockSpec((gb, tc, n), lambda i, j: (i, 0, 0)),
                  pl.BlockSpec((gb, 1, 2 * sw), lambda i, j: (i, 0, 0)),
                  pl.BlockSpec((gb, 1, 2 * sw), lambda i, j: (i, 0, 0)),
                  pl.BlockSpec((gb, bsz, 2 * sw), lambda i, j: (i, 0, 0))],
        out_specs=[pl.BlockSpec((tr, gb * sw), lambda i, j: (j, i)),
                   pl.BlockSpec((tr, gb * sw), lambda i, j: (nblk - 1 - j, i)),
                   pl.BlockSpec((gb, bsz, 2 * sw), lambda i, j: (i, 0, 0))],
        out_shape=[jax.ShapeDtypeStruct((r, g * sw), F32),
                   jax.ShapeDtypeStruct((r, g * sw), F32),
                   jax.ShapeDtypeStruct((g, bsz, 2 * sw), F32)],
        scratch_shapes=[pltpu.VMEM((gb, 2, 2, tr, sw), F32), pltpu.VMEM((gb, 4, bsz, sw), F32)],
        compiler_params=_cparams(("arbitrary", "arbitrary")),
        name="s5_chunk_states_scan",
    )(u, u, w_bf16, a1.reshape(g, 1, 2 * sw), a2.reshape(g, 1, 2 * sw), s0g)


def _s5_out_kernel(u_ref, sf_ref, sb_ref, k_ref, v_ref, o_ref):
    gb = u_ref.shape[0]
    sw = STATE_W
    for g in range(gb):
        s = jnp.concatenate([sf_ref[:, g * sw:(g + 1) * sw], sb_ref[:, g * sw:(g + 1) * sw]], axis=1)
        o_ref[g] = _dot(u_ref[g].astype(BF16), k_ref[g]) + _dot(s.astype(BF16), v_ref[g])


def s5_outputs(u, st_f, st_b, k_bf16, v_bf16):
    g, r, tc = u.shape
    sw = STATE_W
    gb = 8
    tr = min(r, 512)
    return pl.pallas_call(
        _s5_out_kernel,
        grid=(g // gb, r // tr),
        in_specs=[pl.BlockSpec((gb, tr, tc), lambda i, j: (i, j, 0)),
                  pl.BlockSpec((tr, gb * sw), lambda i, j: (j, i)),
                  pl.BlockSpec((tr, gb * sw), lambda i, j: (j, i)),
                  pl.BlockSpec((gb, tc, tc), lambda i, j: (i, 0, 0)),
                  pl.BlockSpec((gb, 2 * sw, tc), lambda i, j: (i, 0, 0))],
        out_specs=pl.BlockSpec((gb, tr, tc), lambda i, j: (i, j, 0)),
        out_shape=jax.ShapeDtypeStruct((g, r, tc), F32),
        compiler_params=_cparams(("arbitrary", "arbitrary")),
        name="s5_outputs",
    )(u, st_f, st_b, k_bf16, v_bf16)


def s5_mixer(x, grp, nw, mod, rowmap, ops, s0):
    k_tot, w_tot, v_tot, a1, a2 = ops
    ng = k_tot.shape[0]
    bsz = grp.bsz
    sw = STATE_W
    u = s5_to_chunks(x, grp, nw, mod, rowmap)
    w = jnp.concatenate([w_tot[:, :, :sw], w_tot[:, :, 2 * sw:3 * sw], w_tot[:, :, sw:2 * sw], w_tot[:, :, 3 * sw:]], axis=-1)
    s0g = s0.transpose(3, 0, 1, 2, 4).reshape(ng, bsz, 2 * sw)
    st_f, st_b, final = s5_chunk_states_scan(u, w.astype(BF16), a1, a2, s0g, bsz)
    y = s5_outputs(u, st_f, st_b, k_tot.astype(BF16), v_tot.astype(BF16))
    final = final.reshape(ng, bsz, 2, 2, S5_STATE).transpose(1, 2, 3, 0, 4)
    return y, final


def _glu_kernel(yc_ref, yl_ref, x_ref, nw_ref, sc_ref, sh_ref, g1_ref, dsk_ref, wa_ref, wb_ref, o_ref, ybuf,
                *, first_tiles):
    gpl = LANES // S5_GROUP

    def from_chunks(y_ref):
        cpt = y_ref.shape[1]
        for c in range(ybuf.shape[0]):
            for t in range(S5_CHUNK):
                for g in range(gpl):
                    ybuf[c, t * cpt:(t + 1) * cpt, g * S5_GROUP:(g + 1) * S5_GROUP] = (
                        y_ref[c * gpl + g, :, t * S5_GROUP:(t + 1) * S5_GROUP])

    in_first = pl.program_id(0) < first_tiles

    @pl.when(in_first)
    def _():
        from_chunks(yc_ref)

    @pl.when(jnp.logical_not(in_first))
    def _():
        from_chunks(yl_ref)

    cpt = TOK_TILE // S5_CHUNK
    y = jnp.concatenate(
        [jnp.concatenate([ybuf[c, pl.ds(j, S5_CHUNK, stride=cpt), :] for c in range(ybuf.shape[0])], axis=1)
         for j in range(cpt)], axis=0)
    x = x_ref[...]
    y = y + dsk_ref[...] * _norm_mod(x, nw_ref[...], sc_ref[0], sh_ref[0])
    yb = jax.nn.gelu(y, approximate=True).astype(BF16)
    a = _dot(yb, wa_ref[...])
    b = _dot(yb, wb_ref[...])
    o_ref[...] = x + g1_ref[0] * (a * _sigmoid(b))


def glu_residual(y_chunks, x, nw, mod, rowmap, groups, dskip, wa_bf16, wb_bf16):
    n_tok, d = x.shape
    tm = TOK_TILE
    ng = d // S5_GROUP
    first, second = groups
    blk = (ng, CPT, S5_TC)

    def rows_of(grp, i):
        return (i % grp.seq_tiles) * grp.bsz + i // grp.seq_tiles

    return pl.pallas_call(
        functools.partial(_glu_kernel, first_tiles=first.tiles),
        grid=(n_tok // tm,),
        in_specs=[pl.BlockSpec(blk, lambda i: (0, rows_of(first, jnp.minimum(i, first.tiles - 1)), 0)),
                  pl.BlockSpec(blk, lambda i: (0, rows_of(second, jnp.maximum(i - first.tiles, 0)), 0)),
                  pl.BlockSpec((tm, d), lambda i: (i, 0)),
                  pl.BlockSpec((1, d), lambda i: (0, 0)),
                  _mod_spec(rowmap, 1, d), _mod_spec(rowmap, 0, d), _mod_spec(rowmap, 2, d),
                  pl.BlockSpec((1, d), lambda i: (0, 0)),
                  pl.BlockSpec((d, d), lambda i: (0, 0)),
                  pl.BlockSpec((d, d), lambda i: (0, 0))],
        out_specs=pl.BlockSpec((tm, d), lambda i: (i, 0)),
        out_shape=jax.ShapeDtypeStruct((n_tok, d), F32),
        scratch_shapes=[pltpu.VMEM((d // LANES, tm, LANES), F32)],
        compiler_params=_cparams(("arbitrary",)),
        name="glu_residual",
    )(y_chunks[0], y_chunks[1], x, nw, mod, mod, mod, dskip, wa_bf16, wb_bf16)


def _router_kernel(x_ref, nw_ref, sc_ref, sh_ref, wr_ref, h_ref, aff_ref):
    d = x_ref.shape[1]
    tm = x_ref.shape[0]
    h = _norm_mod(x_ref[...], nw_ref[...], sc_ref[0], sh_ref[0])
    h_ref[:, :d] = h
    tok = pl.program_id(0) * tm + lax.broadcasted_iota(jnp.int32, (tm, LANES), 0)
    h_ref[:, d:] = tok.astype(F32)
    logits = _dot_nt(wr_ref[...], h.astype(BF16))
    ex = jnp.exp(logits - jnp.max(logits, axis=0, keepdims=True))
    p = ex / jnp.sum(ex, axis=0, keepdims=True)
    for k in range(aff_ref.shape[0]):
        aff_ref[k] = p[:, k * LANES:(k + 1) * LANES]


def moe_router(x, nw, mod, rowmap, wr_t_bf16):
    n_tok, d = x.shape
    tm = TOK_TILE
    ne = wr_t_bf16.shape[0]
    return pl.pallas_call(
        _router_kernel,
        grid=(n_tok // tm,),
        in_specs=[pl.BlockSpec((tm, d), lambda i: (i, 0)),
                  pl.BlockSpec((1, d), lambda i: (0, 0)),
                  _mod_spec(rowmap, 4, d), _mod_spec(rowmap, 3, d),
                  pl.BlockSpec((ne, d), lambda i: (0, 0))],
        out_specs=[pl.BlockSpec((tm, d + LANES), lambda i: (i, 0)),
                   pl.BlockSpec((tm // LANES, ne, LANES), lambda i: (i, 0, 0))],
        out_shape=[jax.ShapeDtypeStruct((n_tok, d + LANES), F32),
                   jax.ShapeDtypeStruct((n_tok // LANES, ne, LANES), F32)],
        compiler_params=_cparams(("arbitrary",)),
        name="moe_router",
    )(x, nw, mod, mod, wr_t_bf16)


def _select_kernel(aff_ref, ut_ref, v8_ref, cidx_ref, meta_ref, inc_ref, *, cap, first, slot0):
    nt, ne, _ = aff_ref.shape
    aff = aff_ref[...]

    def count(mask):
        c = jnp.sum(jnp.where(mask, 1.0, 0.0), axis=0)
        return jnp.sum(c, axis=1, keepdims=True)

    def as_float(bits):
        return lax.bitcast_convert_type(bits, F32)

    def radix(k, bits):
        cand = bits | (jnp.int32(1) << (30 - k))
        return jnp.where(count(aff >= as_float(cand)[None]) >= cap, cand, bits)

    thr_bits = lax.fori_loop(0, 31, radix, jnp.zeros((ne, 1), jnp.int32))
    thr = as_float(thr_bits)[None]
    nxt = as_float(thr_bits + 1)[None]
    above = aff >= nxt
    bucket = (aff >= thr) & jnp.logical_not(above)
    need = cap - count(above)
    width = nxt - thr
    pos = jnp.where(bucket & (width > 0.0), (aff - thr) / width, 0.0)

    def refine(k, t):
        cand = t + lax.convert_element_type(jnp.int32(1) << (29 - k), F32) * (2.0 ** -30)
        return jnp.where(count(bucket & (pos >= cand[None])) >= need, cand, t)

    t = lax.fori_loop(0, 30, refine, jnp.zeros((ne, 1), F32))
    upper = bucket & (pos >= (t + 2.0 ** -30)[None])
    tied = bucket & (pos >= t[None]) & jnp.logical_not(upper)
    ut = ut_ref[...]

    def excl_rank(mask):
        m = jnp.where(mask, 1.0, 0.0)
        inc_ref[...] = _dot(m.reshape(nt * ne, LANES).astype(BF16), ut).reshape(nt, ne, LANES)

        def body(tt, carry):
            inc = inc_ref[tt]
            inc_ref[tt] = inc + carry
            return carry + inc[:, LANES - 1:LANES]

        lax.fori_loop(0, nt, body, jnp.zeros((ne, 1), F32))
        return inc_ref[...] - m

    sel = above | upper | (tied & (excl_rank(tied) < (need - count(upper))[None]))
    rank = excl_rank(sel)
    start = rank[:, :, 0:1]
    nsel = rank[:, :, LANES - 1:LANES] + jnp.where(sel[:, :, LANES - 1:LANES], 1.0, 0.0) - start
    lane3 = lax.broadcasted_iota(jnp.int32, (nt, ne, LANES), 2)
    meta_ref[...] = jnp.where(lane3 == 0, start + float(slot0), jnp.where(lane3 == 1, nsel, 0.0)).astype(jnp.int32)

    inc_ref[...] = jnp.where(sel, rank - start, -1.0)
    sub = lax.broadcasted_iota(jnp.int32, (LANES, LANES), 0).astype(F32)
    v8 = v8_ref[...]

    def tile_body(tt, carry):
        rho = inc_ref[tt]
        base = lax.convert_element_type(first + tt * LANES, F32)
        for e in range(ne):
            onehot = jnp.where(sub == rho[e:e + 1, :], 1.0, 0.0).astype(BF16)
            packed = _dot_nt(v8, onehot)
            cidx_ref[tt, e:e + 1, :] = (packed[0:1] + base).astype(jnp.int32)
        return carry

    lax.fori_loop(0, nt, tile_body, 0)


def moe_select(aff_t, cap, first, slot0):
    nt, ne, _ = aff_t.shape
    ut = jnp.triu(jnp.ones((LANES, LANES), BF16))
    v8 = jnp.zeros((8, LANES), BF16).at[0].set(jnp.arange(LANES).astype(BF16))
    blk = pl.BlockSpec((nt, ne, LANES), lambda i: (0, 0, 0))
    return pl.pallas_call(
        functools.partial(_select_kernel, cap=cap, first=first, slot0=slot0),
        grid=(1,),
        in_specs=[blk, pl.BlockSpec((LANES, LANES), lambda i: (0, 0)), pl.BlockSpec((8, LANES), lambda i: (0, 0))],
        out_specs=[blk, blk],
        out_shape=[jax.ShapeDtypeStruct((nt, ne, LANES), jnp.int32),
                   jax.ShapeDtypeStruct((nt, ne, LANES), jnp.int32)],
        scratch_shapes=[pltpu.VMEM((nt, ne, LANES), F32)],
        compiler_params=_cparams(("arbitrary",)),
        name="moe_select",
    )(aff_t, ut, v8)


def _lists_kernel(starts_ref, counts_ref, cidx_ref, idx_ref):
    nt, ne, _ = cidx_ref.shape
    idx_ref[...] = jnp.zeros_like(idx_ref)
    lane = lax.broadcasted_iota(jnp.int32, (1, LANES), 1)

    def tile(t, carry):
        for e in range(ne):
            s = starts_ref[e, t]
            c = counts_ref[e, t]
            j0 = s >> 7
            o = s & (LANES - 1)
            rolled = pltpu.roll(cidx_ref[t, e:e + 1, :], o, axis=1)
            end = o + c
            row0 = idx_ref[e, pl.ds(j0, 1), :]
            idx_ref[e, pl.ds(j0, 1), :] = jnp.where((lane >= o) & (lane < end), rolled, row0)
            row1 = idx_ref[e, pl.ds(j0 + 1, 1), :]
            idx_ref[e, pl.ds(j0 + 1, 1), :] = jnp.where(lane < end - LANES, rolled, row1)
        return carry

    lax.fori_loop(0, nt, tile, 0)


def moe_build_lists(starts, counts, cidx, rows):
    nt, ne, _ = cidx.shape
    rt = rows // LANES + 2
    out = pl.pallas_call(
        _lists_kernel,
        grid_spec=pltpu.PrefetchScalarGridSpec(
            num_scalar_prefetch=2,
            grid=(1,),
            in_specs=[pl.BlockSpec((nt, ne, LANES), lambda i, s, c: (0, 0, 0))],
            out_specs=pl.BlockSpec((ne, rt, LANES), lambda i, s, c: (0, 0, 0))),
        out_shape=jax.ShapeDtypeStruct((ne, rt, LANES), jnp.int32),
        compiler_params=_cparams(("arbitrary",)),
        name="moe_build_lists",
    )(starts, counts, cidx)
    return out.reshape(ne * rt * LANES)


def _expert_kernel(idx_ref, h_hbm, wr_ref, wg_ref, wu_ref, wd_ref, y_ref, xbuf, wgb, wub, wdb, sem, *, rows_pad):
    e = pl.program_id(0)
    ch = pl.program_id(1)
    nch = pl.num_programs(1)
    tr = xbuf.shape[1]
    d, f = wgb.shape
    step = e * nch + ch
    slot = step % 2

    @pl.when(ch == 0)
    def _():
        wgb[...] = wg_ref[0, 0].astype(BF16)
        wub[...] = wu_ref[0, 0].astype(BF16)
        wdb[...] = wd_ref[0, 0].astype(BF16)

    def row_copy(base, r, slot_i):
        return pltpu.make_async_copy(h_hbm.at[pl.ds(idx_ref[base + r], 1)], xbuf.at[slot_i, pl.ds(r, 1)], sem.at[slot_i])

    @pl.when(step == 0)
    def _():
        def group(j, carry):
            for u in range(8):
                row_copy(0, 8 * j + u, 0).start()
            return carry

        lax.fori_loop(0, tr // 8, group, 0)

    has_next = step + 1 < pl.num_programs(0) * nch
    last = ch == nch - 1
    e_n = jnp.where(has_next, jnp.where(last, e + 1, e), e)
    ch_n = jnp.where(has_next, jnp.where(last, 0, ch + 1), ch)
    base_n = e_n * rows_pad + ch_n * tr

    pltpu.make_async_copy(h_hbm.at[pl.ds(0, tr)], xbuf.at[slot], sem.at[slot]).wait()

    x = xbuf[slot, :, :d].astype(BF16)
    logits = _dot(x, wr_ref[...])
    lane = lax.broadcasted_iota(jnp.int32, logits.shape, 1)
    ne = pl.num_programs(0)
    logits = jnp.where(lane < ne, logits, NEG_INF)
    ex = jnp.exp(logits - jnp.max(logits, axis=1, keepdims=True))
    gate = jnp.sum(jnp.where(lane == e, ex, 0.0), axis=1, keepdims=True) / jnp.sum(ex, axis=1, keepdims=True)

    fb = f // FFN_SPLIT
    rb = tr // FFN_SPLIT
    y = None
    for nb in range(FFN_SPLIT):
        for r in range(nb * rb, (nb + 1) * rb):
            row_copy(base_n, r, 1 - slot).start()
        cols = slice(nb * fb, (nb + 1) * fb)
        hmid = (_silu(_dot(x, wgb[:, cols])) * _dot(x, wub[:, cols])).astype(BF16)
        part = _dot(hmid, wdb[cols, :])
        y = part if y is None else y + part
    y_ref[0, :, :d] = y * gate
    y_ref[0, :, d:] = xbuf[slot, :, d:]

    @pl.when(jnp.logical_not(has_next))
    def _():
        pltpu.make_async_copy(h_hbm.at[pl.ds(0, tr)], xbuf.at[1 - slot], sem.at[1 - slot]).wait()


def moe_experts(idx, rows, h_ext, wr_pad_bf16, w_gate, w_up, w_down, layer):
    _, ne, d, f = w_gate.shape
    rows_pad = idx.shape[0] // ne
    dx = h_ext.shape[1]
    tr = 512
    return pl.pallas_call(
        functools.partial(_expert_kernel, rows_pad=rows_pad),
        grid_spec=pltpu.PrefetchScalarGridSpec(
            num_scalar_prefetch=1,
            grid=(ne, rows // tr),
            in_specs=[pl.BlockSpec(memory_space=pl.ANY),
                      pl.BlockSpec((d, LANES), lambda e, c, idx: (0, 0)),
                      pl.BlockSpec((1, 1, d, f), lambda e, c, idx: (layer, e, 0, 0)),
                      pl.BlockSpec((1, 1, d, f), lambda e, c, idx: (layer, e, 0, 0)),
                      pl.BlockSpec((1, 1, f, d), lambda e, c, idx: (layer, e, 0, 0))],
            out_specs=pl.BlockSpec((1, tr, dx), lambda e, c, idx: (e, c, 0)),
            scratch_shapes=[pltpu.VMEM((2, tr, dx), F32),
                            pltpu.VMEM((d, f), BF16), pltpu.VMEM((d, f), BF16), pltpu.VMEM((f, d), BF16),
                            pltpu.SemaphoreType.DMA((2,))]),
        out_shape=jax.ShapeDtypeStruct((ne, rows, dx), F32),
        compiler_params=_cparams(("arbitrary", "arbitrary")),
        name="moe_experts",
    )(idx, h_ext, wr_pad_bf16, w_gate, w_up, w_down)


def _combine_kernel(starts_ref, y_hbm, x_ref, g2_ref, *rest, first_tiles):
    if first_tiles is None:
        o_ref, acc, stage, sem = rest
    else:
        oc_ref, ol_ref, acc, stage, sem = rest
    tb = pl.program_id(0)
    tm, d = x_ref.shape
    ne = y_hbm.shape[0]
    slot = tb % 2

    def chunk_copy(e, src_row, dst_row, slot_i):
        return pltpu.make_async_copy(y_hbm.at[e, pl.ds(src_row, 8)], stage.at[slot_i, pl.ds(dst_row, 8)], sem.at[slot_i])

    def spans(tile):
        out = []
        for e in range(ne):
            s0 = starts_ref[e, tile]
            s1 = starts_ref[e, tile + 1]
            a = (s0 >> 3) << 3
            out.append((a, jnp.where(s1 > s0, (s1 - a + 7) >> 3, 0)))
        return out

    def fetch(tile, slot_i):
        off = jnp.int32(0)
        for e, (a, nchunk) in enumerate(spans(tile)):
            def issue(c, carry, e=e, a=a, off=off):
                chunk_copy(e, pl.multiple_of(a + 8 * c, 8), pl.multiple_of(off + 8 * c, 8), slot_i).start()
                return carry

            lax.fori_loop(0, nchunk, issue, 0)
            off = off + 8 * nchunk

    @pl.when(tb == 0)
    def _():
        stage[...] = jnp.zeros(stage.shape, F32)
        fetch(tb, slot)

    @pl.when(tb + 1 < pl.num_programs(0))
    def _():
        fetch(tb + 1, 1 - slot)

    off = jnp.int32(0)
    for _, nchunk in spans(tb):
        off = off + 8 * nchunk

    def drain(c, carry):
        chunk_copy(0, 0, 0, slot).wait()
        return carry

    lax.fori_loop(0, off >> 3, drain, 0)

    acc[...] = jnp.zeros_like(acc)
    want = (lax.broadcasted_iota(jnp.int32, (tm, KCH), 0) + tb * tm).astype(F32)

    def fold(kc, carry):
        rows = stage[slot, pl.ds(pl.multiple_of(kc * KCH, KCH), KCH), :]
        tok = rows[:, d:].T[0:1, :]
        fresh = lax.broadcasted_iota(jnp.int32, (tm, KCH), 1) + kc * KCH < off
        onehot = jnp.where((want == tok) & fresh, 1.0, 0.0).astype(BF16)
        y = rows[:, :d]
        hi = y.astype(BF16)
        lo = (y - hi.astype(F32)).astype(BF16)
        acc[...] += _dot(onehot, hi) + _dot(onehot, lo)
        return carry

    lax.fori_loop(0, (off + KCH - 1) >> 8, fold, 0)
    res = x_ref[...] + g2_ref[0] * acc[...]
    if first_tiles is None:
        o_ref[...] = res
    else:
        @pl.when(tb < first_tiles)
        def _():
            oc_ref[...] = res

        @pl.when(tb >= first_tiles)
        def _():
            ol_ref[...] = res


def moe_combine(starts, y, x, mod, rowmap, split=None):
    n_tok, d = x.shape
    ne, _, dx = y.shape
    tm = TOK_TILE
    stage_rows = -(-(ne * tm + ne * 16) // KCH) * KCH
    if split is None:
        out_specs = pl.BlockSpec((tm, d), lambda i, s: (i, 0))
        out_shape = jax.ShapeDtypeStruct((n_tok, d), F32)
        first_tiles = None
    else:
        first_tiles = split[0]
        out_specs = [pl.BlockSpec((tm, d), lambda i, s: (jnp.minimum(i, first_tiles - 1), 0)),
                     pl.BlockSpec((tm, d), lambda i, s: (jnp.maximum(i - first_tiles, 0), 0))]
        out_shape = [jax.ShapeDtypeStruct((split[0] * tm, d), F32), jax.ShapeDtypeStruct((split[1] * tm, d), F32)]
    return pl.pallas_call(
        functools.partial(_combine_kernel, first_tiles=first_tiles),
        grid_spec=pltpu.PrefetchScalarGridSpec(
            num_scalar_prefetch=1,
            grid=(n_tok // tm,),
            in_specs=[pl.BlockSpec(memory_space=pl.ANY),
                      pl.BlockSpec((tm, d), lambda i, s: (i, 0)),
                      pl.BlockSpec((1, 1, d), lambda i, s: (rowmap(i), 0, 5))],
            out_specs=out_specs,
            scratch_shapes=[pltpu.VMEM((tm, d), F32), pltpu.VMEM((2, stage_rows, dx), F32),
                            pltpu.SemaphoreType.DMA((2,))]),
        out_shape=out_shape,
        compiler_params=_cparams(("arbitrary",)),
        name="moe_combine",
    )(starts, y, x, mod)


def moe_layer(x, nw, mod, rowmap, groups, wr, w_gate, w_up, w_down, layer, split=False):
    n_tok, d = x.shape
    ne = wr.shape[1]
    h_ext, aff_t = moe_router(x, nw, mod, rowmap, wr.T.astype(BF16))
    cidx_parts, meta_parts = [], []
    rows = 0
    for grp in groups:
        cap = EC_FACTOR * grp.tokens // ne
        t0 = grp.row0 // LANES
        cidx, meta = moe_select(aff_t[t0:t0 + grp.tokens // LANES], cap, grp.row0, rows)
        cidx_parts.append(cidx)
        meta_parts.append(meta[:, :, :2])
        rows += cap
    cidx = jnp.concatenate(cidx_parts, axis=0)
    meta = jnp.concatenate(meta_parts, axis=0)
    starts = meta[:, :, 0].T
    counts = meta[:, :, 1].T
    idx = moe_build_lists(starts, counts, cidx, rows)
    per = TOK_TILE // LANES
    starts_blk = jnp.concatenate([starts[:, ::per], jnp.full((ne, 1), rows, jnp.int32)], axis=1)
    wr_pad = jnp.zeros((d, LANES), BF16).at[:, :ne].set(wr.astype(BF16))
    y = moe_experts(idx, rows, h_ext, wr_pad, w_gate, w_up, w_down, layer)
    return moe_combine(starts_blk, y, x, mod, rowmap, (groups[0].tiles, groups[1].tiles) if split else None)


def kernel(x_prompt, x_sample, cache_k, cache_v, state_hgrn, state_s5, c, c_ctx, norm_w, ada_w, ada_b, w_in_ab, w_out_ab, hgrn_lb_logits, hgrn_norm_w, q_norm_w, k_norm_w, attn_sink, s5_a_re, s5_a_im, s5_log_dt, s5_b_re, s5_b_im, s5_c_re, s5_c_im, s5_d, glu_w_a, glu_w_b, router_w, exp_w_gate, exp_w_up, exp_w_down):
    b_ctx, l_ctx, d = x_prompt.shape
    b_lat, l_lat, _ = x_sample.shape
    depth = norm_w.shape[0]
    ctx = Group(0, b_ctx, l_ctx)
    lat = Group(ctx.tokens, b_lat, l_lat)
    groups = (ctx, lat)

    def rowmap(i):
        return jnp.where(i < ctx.tiles, i // ctx.seq_tiles, b_ctx + (i - ctx.tiles) // lat.seq_tiles)

    cond = jnp.concatenate([c_ctx[None, :], c, jnp.zeros((8 - 1 - b_lat, d), F32)], axis=0)
    mod_small = ada_modulation(cond, ada_w, ada_b)
    seq_rows = jnp.concatenate([jnp.zeros((b_ctx,), jnp.int32), 1 + jnp.arange(b_lat, dtype=jnp.int32)])
    mods = mod_small[:, seq_rows][:, :, None, :]

    x = jnp.concatenate([x_prompt.reshape(ctx.tokens, d), x_sample.reshape(lat.tokens, d)], axis=0)
    rope = rope_tables(l_lat)
    ks, vs, hs, ss = [], [], [], []
    for l in range(depth):
        mod = mods[l]
        nw1 = norm_w[l, 0].reshape(1, d)
        nw2 = norm_w[l, 1].reshape(1, d)
        if l % 2 == 0:
            e = l // 2
            proj = norm_mod_matmul(x, nw1, mod, rowmap, 1, 0, w_in_ab[e].astype(BF16))
            zero_state = jnp.zeros((b_ctx, 2, A_HEADS, A_DK, A_DK), F32)
            of_c, ob_c, st_c = hgrn2_mixer(proj, ctx, hgrn_lb_logits, zero_state, e)
            of_l, ob_l, _ = hgrn2_mixer(proj, lat, hgrn_lb_logits, jnp.swapaxes(state_hgrn[:, e], -1, -2), e)
            hs.append(jnp.swapaxes(st_c, -1, -2))
            qn_c, kn_c = qk_prepare(proj, ctx, q_norm_w[e], k_norm_w[e], None)
            att_c = context_attention(qn_c, kn_c, proj, ctx, attn_sink[e])
            ks.append(kn_c.reshape(b_ctx, l_ctx, B_KV_HEADS, HEAD_DIM).transpose(0, 2, 1, 3))
            vs.append(proj[:ctx.tokens, V_COL:].reshape(b_ctx, l_ctx, B_KV_HEADS, HEAD_DIM).transpose(0, 2, 1, 3))
            qr_l, kr_l = qk_prepare(proj, lat, q_norm_w[e], k_norm_w[e], rope)
            att_l = latent_attention(qr_l, kr_l, proj, lat, cache_k[:, e], cache_v[:, e], attn_sink[e])
            x = even_out_proj((of_c, of_l), (ob_c, ob_l), (att_c, att_l), proj, x, mod, rowmap, groups,
                              hgrn_norm_w[e].reshape(1, A_DK), w_out_ab[e].astype(BF16))
        else:
            o = l // 2
            ops = s5_operators(s5_a_re[o], s5_a_im[o], s5_log_dt[o], s5_b_re[o], s5_b_im[o], s5_c_re[o], s5_c_im[o])
            zero_s5 = jnp.zeros((b_ctx, 2, 2, d // S5_GROUP, S5_STATE), F32)
            y_c, fin_c = s5_mixer(x, ctx, nw1, mod, rowmap, ops, zero_s5)
            y_l, _ = s5_mixer(x, lat, nw1, mod, rowmap, ops, state_s5[:, o])
            ss.append(fin_c)
            x = glu_residual((y_c, y_l), x, nw1, mod, rowmap, groups, s5_d[o].reshape(1, d),
                             glu_w_a[o].astype(BF16), glu_w_b[o].astype(BF16))
        x = moe_layer(x, nw2, mod, rowmap, groups, router_w[l], exp_w_gate, exp_w_up, exp_w_down, l,
                      split=(l == depth - 1))
    y_prompt = x[0].reshape(b_ctx, l_ctx, d)
    y_sample = x[1].reshape(b_lat, l_lat, d)
    return (y_prompt, y_sample, jnp.stack(ks, axis=1), jnp.stack(vs, axis=1),
            jnp.stack(hs, axis=1), jnp.stack(ss, axis=1))
```

```python
import functools

import jax
import jax.numpy as jnp
from jax import lax
from jax.experimental import pallas as pl
from jax.experimental.pallas import tpu as pltpu

F32 = jnp.float32
BF16 = jnp.bfloat16

A_HEADS = 4
A_DK = 128
A_WIDTH = A_HEADS * A_DK
B_HEADS = 8
B_KV_HEADS = 2
HEAD_DIM = 64
B_GROUP = B_HEADS // B_KV_HEADS
B_WIDTH = B_HEADS * HEAD_DIM
KV_WIDTH = B_KV_HEADS * HEAD_DIM
Q_COL = 5 * A_WIDTH
K_COL = Q_COL + B_WIDTH
V_COL = K_COL + KV_WIDTH
WINDOW = 128
GRID_W = 64
ROPE_THETA = 10000.0
S5_GROUP = 16
S5_STATE = 64
S5_CHUNK = 16
S5_TC = S5_CHUNK * S5_GROUP
N_EXPERTS = 16
EC_FACTOR = 2
EPS = 1e-6
NEG_INF = -1e30

HGRN_C = 128
HGRN_SB = 16
TOK_TILE = 256
LANES = 128
VMEM_LIMIT = 56 * 1024 * 1024
STATE_W = 2 * S5_STATE
KCH = 256
CPT = TOK_TILE // S5_CHUNK
FFN_SPLIT = 4


def _cparams(sem):
    return pltpu.CompilerParams(dimension_semantics=sem, vmem_limit_bytes=VMEM_LIMIT)


def _sigmoid(x):
    return 1.0 / (1.0 + jnp.exp(-x))


def _silu(x):
    return x * _sigmoid(x)


def _norm_mod(x, nw, sc, sh):
    ms = jnp.mean(x * x, axis=-1, keepdims=True)
    return (x * lax.rsqrt(ms + EPS) * nw) * (1.0 + sc) + sh


def _dot(a, b):
    return jnp.dot(a, b, preferred_element_type=F32)


def _dot_nt(a, b):
    return lax.dot_general(a, b, (((1,), (1,)), ((), ())), preferred_element_type=F32)


def _split3(x):
    hi = x.astype(BF16)
    r1 = x - hi.astype(F32)
    mid = r1.astype(BF16)
    lo = (r1 - mid.astype(F32)).astype(BF16)
    return hi, mid, lo


class Group:
    def __init__(self, row0, bsz, seq):
        self.row0, self.bsz, self.seq = row0, bsz, seq
        self.tokens = bsz * seq
        self.tile0 = row0 // TOK_TILE
        self.tiles = self.tokens // TOK_TILE
        self.seq_tiles = seq // TOK_TILE


def _ada_kernel(c_ref, w_ref, b_ref, o_ref):
    s = _silu(c_ref[...])
    o_ref[0] = _dot(s.astype(BF16), w_ref[0].astype(BF16)) + b_ref[0]


def ada_modulation(cond, ada_w, ada_b):
    depth, d, n = ada_w.shape
    rows = cond.shape[0]
    tn = 1536
    return pl.pallas_call(
        _ada_kernel,
        grid=(depth, n // tn),
        in_specs=[pl.BlockSpec((rows, d), lambda l, j: (0, 0)),
                  pl.BlockSpec((1, d, tn), lambda l, j: (l, 0, j)),
                  pl.BlockSpec((1, 1, tn), lambda l, j: (l, 0, j))],
        out_specs=pl.BlockSpec((1, rows, tn), lambda l, j: (l, 0, j)),
        out_shape=jax.ShapeDtypeStruct((depth, rows, n), F32),
        compiler_params=_cparams(("arbitrary", "arbitrary")),
        name="ada_modulation",
    )(cond, ada_w, ada_b.reshape(depth, 1, n))


def _mod_spec(rowmap, k, d):
    return pl.BlockSpec((1, 1, d), lambda i: (rowmap(i), 0, k))


def _inproj_kernel(x_ref, nw_ref, sc_ref, sh_ref, w_ref, o_ref):
    h = _norm_mod(x_ref[...], nw_ref[...], sc_ref[0], sh_ref[0])
    o_ref[...] = _dot(h.astype(BF16), w_ref[...])


def norm_mod_matmul(x, nw, mod, rowmap, k_sc, k_sh, w_bf16):
    n_tok, d = x.shape
    n = w_bf16.shape[1]
    tm = TOK_TILE
    return pl.pallas_call(
        _inproj_kernel,
        grid=(n_tok // tm,),
        in_specs=[pl.BlockSpec((tm, d), lambda i: (i, 0)),
                  pl.BlockSpec((1, d), lambda i: (0, 0)),
                  _mod_spec(rowmap, k_sc, d), _mod_spec(rowmap, k_sh, d),
                  pl.BlockSpec((d, n), lambda i: (0, 0))],
        out_specs=pl.BlockSpec((tm, n), lambda i: (i, 0)),
        out_shape=jax.ShapeDtypeStruct((n_tok, n), F32),
        compiler_params=_cparams(("arbitrary",)),
        name="norm_mod_matmul",
    )(x, nw, mod, mod, w_bf16)


def _two_group_specs(groups, shape_of):
    first, second = groups
    return [pl.BlockSpec(shape_of, lambda i: (jnp.minimum(i, first.tiles - 1), 0)),
            pl.BlockSpec(shape_of, lambda i: (jnp.maximum(i - first.tiles, 0), 0))]


def _outproj_kernel(ofc_ref, ofl_ref, obc_ref, obl_ref, atc_ref, atl_ref, ga_ref, x_ref, g1_ref, hw_ref, w_ref,
                    o_ref, *, first_tiles):
    in_first = pl.program_id(0) < first_tiles
    o = jnp.where(in_first, ofc_ref[...] + obc_ref[...], ofl_ref[...] + obl_ref[...])
    o_att = jnp.where(in_first, atc_ref[...], atl_ref[...])
    gate = _silu(ga_ref[...])
    hw = hw_ref[...]
    parts = []
    for h in range(A_HEADS):
        sl = slice(h * A_DK, (h + 1) * A_DK)
        oh = o[:, sl]
        ms = jnp.mean(oh * oh, axis=-1, keepdims=True)
        parts.append(((oh * lax.rsqrt(ms + EPS) * hw) * gate[:, sl]).astype(BF16))
    parts.append(o_att.astype(BF16))
    y = _dot(jnp.concatenate(parts, axis=1), w_ref[...])
    o_ref[...] = x_ref[...] + g1_ref[0] * y


def even_out_proj(o_f, o_b, o_att, proj, x, mod, rowmap, groups, hw, w_bf16):
    n_tok, d = x.shape
    tm = TOK_TILE
    aw = A_WIDTH
    return pl.pallas_call(
        functools.partial(_outproj_kernel, first_tiles=groups[0].tiles),
        grid=(n_tok // tm,),
        in_specs=_two_group_specs(groups, (tm, aw)) + _two_group_specs(groups, (tm, aw))
                 + _two_group_specs(groups, (tm, B_WIDTH))
                 + [pl.BlockSpec((tm, aw), lambda i: (i, 4)),
                    pl.BlockSpec((tm, d), lambda i: (i, 0)),
                    _mod_spec(rowmap, 2, d),
                    pl.BlockSpec((1, A_DK), lambda i: (0, 0)),
                    pl.BlockSpec((aw + B_WIDTH, d), lambda i: (0, 0))],
        out_specs=pl.BlockSpec((tm, d), lambda i: (i, 0)),
        out_shape=jax.ShapeDtypeStruct((n_tok, d), F32),
        compiler_params=_cparams(("arbitrary",)),
        name="even_out_proj",
    )(o_f[0], o_f[1], o_b[0], o_b[1], o_att[0], o_att[1], proj, x, mod, hw, w_bf16)


def _hgrn_chunk(q, k, v, g, st, msel, rev):
    c = q.shape[0]
    nb = c // HGRN_SB
    row = lax.broadcasted_iota(jnp.int32, (c, c), 0)
    col = lax.broadcasted_iota(jnp.int32, (c, c), 1)
    tri = jnp.where((col >= row) if rev else (col <= row), 1.0, 0.0).astype(BF16)
    gh, gm, gl = _split3(g)
    b = _dot(tri, gh) + _dot(tri, gm) + _dot(tri, gl)
    b_edge = b[0:1] if rev else b[c - 1:c]
    qs = q * jnp.exp(b)
    kdec = k * jnp.exp(b_edge - b)

    lk = jnp.log(jnp.maximum(k, 0.0))
    half = HGRN_SB // 2
    zero_half = jnp.zeros((half, A_DK), F32)
    slabs = []
    for i in range(nb):
        sl = slice(i * HGRN_SB, (i + 1) * HGRN_SB)
        bi, qi, lki = b[sl], q[sl], lk[sl]
        ci = bi - lki
        pieces = []
        for s in range(HGRN_SB):
            s_half = s // half
            halves = []
            for hh in range(2):
                rows = slice(hh * half, (hh + 1) * half)
                if (hh > s_half) if rev else (hh < s_half):
                    halves.append(zero_half)
                    continue
                d = bi[rows] - ci[s:s + 1]
                if hh == s_half:
                    d = jnp.minimum(d, lki[s:s + 1])
                halves.append(qi[rows] * jnp.exp(d))
            pieces.append(jnp.concatenate(halves, axis=0).astype(BF16))
        slabs.append(jnp.concatenate(pieces, axis=1))
    a_loc = _dot(jnp.concatenate(slabs, axis=0), msel)

    lane = lax.broadcasted_iota(jnp.int32, (HGRN_SB, LANES), 1)
    rloc = lax.broadcasted_iota(jnp.int32, (HGRN_SB, LANES), 0)
    dmask = ((lane >= rloc) & (lane < HGRN_SB)) if rev else (lane <= rloc)
    krow = lax.broadcasted_iota(jnp.int32, (c, A_DK), 0)
    att_rows = []
    for i in range(nb):
        sl = slice(i * HGRN_SB, (i + 1) * HGRN_SB)
        a_d = jnp.where(dmask, a_loc[sl], 0.0)
        if i > 0:
            a_d = pltpu.roll(a_d, i * HGRN_SB, axis=1)
        a_i = a_d[:, :c]
        has_off = (i < nb - 1) if rev else (i > 0)
        if has_off:
            edge = (i + 1) * HGRN_SB if rev else i * HGRN_SB
            r = b[edge:edge + 1] if rev else b[edge - 1:edge]
            qp = q[sl] * jnp.exp(b[sl] - r)
            live = (krow >= edge) if rev else (krow < edge)
            kp = jnp.where(live, k * jnp.exp(jnp.minimum(r - b, 0.0)), 0.0)
            a_i = a_i + _dot_nt(qp.astype(BF16), kp.astype(BF16))
        att_rows.append(a_i)
    att = jnp.concatenate(att_rows, axis=0)

    vb = v.astype(BF16)
    o = _dot(att.astype(BF16), vb) + _dot_nt(qs.astype(BF16), st.astype(BF16))
    st_new = st * jnp.exp(b_edge) + _dot(v.T.astype(BF16), kdec.astype(BF16))
    return o, st_new


def _hgrn_kernel(qf_ref, vf_ref, ff_ref, qb_ref, vb_ref, fb_ref, lbl_ref, msel_ref, s0_ref,
                 of_ref, ob_ref, sout_ref, st_ref, *, layer):
    c_idx = pl.program_id(1)

    @pl.when(c_idx == 0)
    def _():
        st_ref[...] = s0_ref[0]

    lg = lbl_ref[...]
    ex = jnp.exp(lg - jnp.max(lg, axis=0, keepdims=True))
    pr = ex / jnp.sum(ex, axis=0, keepdims=True)
    lb = jnp.zeros_like(pr[0])
    for e in range(1, layer + 1):
        lb = lb + pr[e]
    msel = msel_ref[...]

    for d, (q_ref, v_ref, f_ref, o_ref) in enumerate(((qf_ref, vf_ref, ff_ref, of_ref),
                                                       (qb_ref, vb_ref, fb_ref, ob_ref))):
        q_all = _silu(q_ref[...])
        v_all = v_ref[...]
        lbd = lb[d:d + 1]
        forget = lbd + (1.0 - lbd) * _sigmoid(f_ref[...])
        k_all = 1.0 - forget
        g_all = jnp.log(forget)
        for h in range(A_HEADS):
            sl = slice(h * A_DK, (h + 1) * A_DK)
            o, st_new = _hgrn_chunk(q_all[:, sl], k_all[:, sl], v_all[:, sl], g_all[:, sl],
                                    st_ref[d, h], msel, rev=(d == 1))
            o_ref[:, sl] = o
            st_ref[d, h] = st_new

    @pl.when(c_idx == pl.num_programs(1) - 1)
    def _():
        sout_ref[0] = st_ref[...]


def hgrn2_mixer(proj, grp, lb_logits, s0t, layer):
    c = HGRN_C
    nc = grp.seq // c
    blk0 = grp.row0 // c
    aw = A_WIDTH
    msel = jnp.repeat(jnp.eye(HGRN_SB, LANES, dtype=BF16), A_DK, axis=0)

    def fwd(col):
        return pl.BlockSpec((c, aw), lambda b, i: (blk0 + b * nc + i, col))

    def bwd(col):
        return pl.BlockSpec((c, aw), lambda b, i: (blk0 + b * nc + nc - 1 - i, col))

    st_spec = pl.BlockSpec((1, 2, A_HEADS, A_DK, A_DK), lambda b, i: (b, 0, 0, 0, 0))
    return pl.pallas_call(
        functools.partial(_hgrn_kernel, layer=layer),
        grid=(grp.bsz, nc),
        in_specs=[fwd(0), fwd(3), fwd(1), bwd(0), bwd(3), bwd(2),
                  pl.BlockSpec(lb_logits.shape, lambda b, i: (0, 0, 0)),
                  pl.BlockSpec(msel.shape, lambda b, i: (0, 0)),
                  st_spec],
        out_specs=[pl.BlockSpec((c, aw), lambda b, i: (b * nc + i, 0)),
                   pl.BlockSpec((c, aw), lambda b, i: (b * nc + nc - 1 - i, 0)),
                   st_spec],
        out_shape=[jax.ShapeDtypeStruct((grp.tokens, aw), F32),
                   jax.ShapeDtypeStruct((grp.tokens, aw), F32),
                   jax.ShapeDtypeStruct((grp.bsz, 2, A_HEADS, A_DK, A_DK), F32)],
        scratch_shapes=[pltpu.VMEM((2, A_HEADS, A_DK, A_DK), F32)],
        compiler_params=_cparams(("arbitrary", "arbitrary")),
        name="hgrn2_mixer",
    )(proj, proj, proj, proj, proj, proj, lb_logits, msel, s0t)


def _head_norm(x, w, gmat):
    hi, mid, lo = _split3(x * x)
    ms = _dot(hi, gmat) + _dot(mid, gmat) + _dot(lo, gmat)
    return x * lax.rsqrt(ms + EPS) * w


def _rope(x, cos, sin_signed):
    width = x.shape[1]
    lane = lax.broadcasted_iota(jnp.int32, x.shape, 1)
    nxt = pltpu.roll(x, width - 1, axis=1)
    prv = pltpu.roll(x, 1, axis=1)
    partner = jnp.where(lane % 2 == 0, nxt, prv)
    return x * cos + partner * sin_signed


def _qkprep_kernel(*refs, rope):
    if rope:
        q_ref, k_ref, qw_ref, kw_ref, gm_ref, cos_ref, sin_ref, qo_ref, ko_ref = refs
    else:
        q_ref, k_ref, qw_ref, kw_ref, gm_ref, qo_ref, ko_ref = refs
    gm = gm_ref[...]
    qn = _head_norm(q_ref[...], qw_ref[...], gm)
    kn = _head_norm(k_ref[...], kw_ref[...], gm[:KV_WIDTH, :KV_WIDTH])
    if rope:
        cos = cos_ref[...]
        sin = sin_ref[...]
        qn = _rope(qn, cos, sin)
        kn = _rope(kn, cos[:, :KV_WIDTH], sin[:, :KV_WIDTH])
    qo_ref[...] = (qn * (HEAD_DIM ** -0.5)).astype(BF16)
    ko_ref[...] = kn


def qk_prepare(proj, grp, qw, kw, rope_tabs):
    tm = TOK_TILE
    st = grp.seq_tiles
    gidx = jnp.arange(B_WIDTH) // HEAD_DIM
    gmat = jnp.where(gidx[:, None] == gidx[None, :], 1.0 / HEAD_DIM, 0.0).astype(BF16)
    qw_t = jnp.tile(qw, B_HEADS).reshape(1, B_WIDTH)
    kw_t = jnp.tile(kw, B_KV_HEADS).reshape(1, KV_WIDTH)
    in_specs = [pl.BlockSpec((tm, B_WIDTH), lambda i: (grp.tile0 + i, Q_COL // B_WIDTH)),
                pl.BlockSpec((tm, KV_WIDTH), lambda i: (grp.tile0 + i, K_COL // KV_WIDTH)),
                pl.BlockSpec((1, B_WIDTH), lambda i: (0, 0)),
                pl.BlockSpec((1, KV_WIDTH), lambda i: (0, 0)),
                pl.BlockSpec((B_WIDTH, B_WIDTH), lambda i: (0, 0))]
    args = [proj, proj, qw_t, kw_t, gmat]
    if rope_tabs is not None:
        in_specs += [pl.BlockSpec((tm, B_WIDTH), lambda i: (i % st, 0)),
                     pl.BlockSpec((tm, B_WIDTH), lambda i: (i % st, 0))]
        args += list(rope_tabs)
    return pl.pallas_call(
        functools.partial(_qkprep_kernel, rope=rope_tabs is not None),
        grid=(grp.tiles,),
        in_specs=in_specs,
        out_specs=[pl.BlockSpec((tm, B_WIDTH), lambda i: (i, 0)),
                   pl.BlockSpec((tm, KV_WIDTH), lambda i: (i, 0))],
        out_shape=[jax.ShapeDtypeStruct((grp.tokens, B_WIDTH), BF16),
                   jax.ShapeDtypeStruct((grp.tokens, KV_WIDTH), F32)],
        compiler_params=_cparams(("arbitrary",)),
        name="qk_prepare",
    )(*args)


def rope_tables(seq):
    pos = jnp.arange(seq)
    row = (pos // GRID_W).astype(F32)
    col = (pos % GRID_W).astype(F32)
    n_pair = HEAD_DIM // 4
    freqs = ROPE_THETA ** (-jnp.arange(n_pair, dtype=F32) / n_pair)
    ang = jnp.concatenate([row[:, None] * freqs, col[:, None] * freqs], axis=-1)
    cos = jnp.repeat(jnp.cos(ang), 2, axis=-1)
    sin = jnp.repeat(jnp.sin(ang), 2, axis=-1) * jnp.tile(jnp.array([-1.0, 1.0], F32), HEAD_DIM // 2)
    return jnp.tile(cos, (1, B_HEADS)), jnp.tile(sin, (1, B_HEADS))


def _value_with_ones(v2, kvh):
    lane = lax.broadcasted_iota(jnp.int32, v2.shape, 1)
    if kvh == 1:
        v2 = pltpu.roll(v2, HEAD_DIM, axis=1)
    return jnp.where(lane < HEAD_DIM, v2, 1.0).astype(BF16)


def _group_attention(q, kk, vv1, valid, sink_ref, kvh, o_ref):
    tq = q.shape[0]
    heads = [kvh * B_GROUP + gq for gq in range(B_GROUP)]
    qs = jnp.concatenate([q[:, h * HEAD_DIM:(h + 1) * HEAD_DIM] for h in heads], axis=0)
    sink = jnp.concatenate([jnp.broadcast_to(sink_ref[h:h + 1, 0:1], (tq, 1)) for h in heads], axis=0)
    s = _dot_nt(qs, kk)
    if valid is not None:
        s = jnp.where(jnp.concatenate([valid] * B_GROUP, axis=0), s, NEG_INF)
    m = jnp.maximum(jnp.max(s, axis=1, keepdims=True), sink)
    pv = _dot(jnp.exp(s - m).astype(BF16), vv1)
    den = pv[:, HEAD_DIM:HEAD_DIM + 1] + jnp.exp(sink - m)
    o = pv[:, :HEAD_DIM] / den
    for gq, h in enumerate(heads):
        o_ref[:, h * HEAD_DIM:(h + 1) * HEAD_DIM] = o[gq * tq:(gq + 1) * tq]


def _ctx_attn_kernel(q_ref, k_ref, v_ref, sink_ref, o_ref):
    q = q_ref[...]
    k = k_ref[...].astype(BF16)
    v = v_ref[...]
    for kvh in range(B_KV_HEADS):
        ks = slice(kvh * HEAD_DIM, (kvh + 1) * HEAD_DIM)
        _group_attention(q, k[:, ks], _value_with_ones(v, kvh), None, sink_ref, kvh, o_ref)


def context_attention(qn, kn, proj, grp, sink):
    seq = grp.seq
    blk0 = grp.row0 // seq
    sink_t = jnp.broadcast_to(sink.reshape(B_HEADS, 1), (B_HEADS, LANES))
    return pl.pallas_call(
        _ctx_attn_kernel,
        grid=(grp.bsz,),
        in_specs=[pl.BlockSpec((seq, B_WIDTH), lambda b: (b, 0)),
                  pl.BlockSpec((seq, KV_WIDTH), lambda b: (b, 0)),
                  pl.BlockSpec((seq, KV_WIDTH), lambda b: (blk0 + b, V_COL // KV_WIDTH)),
                  pl.BlockSpec((B_HEADS, LANES), lambda b: (0, 0))],
        out_specs=pl.BlockSpec((seq, B_WIDTH), lambda b: (b, 0)),
        out_shape=jax.ShapeDtypeStruct((grp.tokens, B_WIDTH), F32),
        compiler_params=_cparams(("arbitrary",)),
        name="context_attention",
    )(qn, kn, proj, sink_t)


def _lat_attn_kernel(q_ref, kp_ref, kc_ref, kn_ref, vp_ref, vc_ref, vn_ref, kx_ref, vx_ref, sink_ref, o_ref):
    blk = pl.program_id(1)
    nblk = pl.num_programs(1)
    tq = q_ref.shape[0]
    q = q_ref[...]
    kl = jnp.concatenate([kp_ref[...], kc_ref[...], kn_ref[...]], axis=0).astype(BF16)
    vl = jnp.concatenate([vp_ref[...], vc_ref[...], vn_ref[...]], axis=0)
    n_ctx = kx_ref.shape[2]
    span = 3 * tq
    i = lax.broadcasted_iota(jnp.int32, (tq, span + n_ctx), 0)
    j = lax.broadcasted_iota(jnp.int32, (tq, span + n_ctx), 1)
    dist = j - tq - i
    valid = (dist >= -WINDOW) & (dist <= WINDOW)
    valid = valid & ((j >= tq) | (blk > 0)) & ((j < 2 * tq) | (blk < nblk - 1))
    valid = valid | (j >= span)
    for kvh in range(B_KV_HEADS):
        ks = slice(kvh * HEAD_DIM, (kvh + 1) * HEAD_DIM)
        kk = jnp.concatenate([kl[:, ks], kx_ref[0, kvh].astype(BF16)], axis=0)
        vv1 = jnp.concatenate([_value_with_ones(vl, kvh), vx_ref[0, kvh].astype(BF16)], axis=0)
        _group_attention(q, kk, vv1, valid, sink_ref, kvh, o_ref)


def latent_attention(qr, kr, proj, grp, k_ctx, v_ctx, sink):
    tq = WINDOW
    nblk = grp.seq // tq
    blk0 = grp.row0 // tq
    n_ctx = k_ctx.shape[2]
    sink_t = jnp.broadcast_to(sink.reshape(B_HEADS, 1), (B_HEADS, LANES))
    v_ctx1 = jnp.concatenate([v_ctx, jnp.ones_like(v_ctx)], axis=-1)

    def kv_specs(off, col):
        return [pl.BlockSpec((tq, KV_WIDTH), lambda b, i: (off + b * nblk + jnp.maximum(i - 1, 0), col)),
                pl.BlockSpec((tq, KV_WIDTH), lambda b, i: (off + b * nblk + i, col)),
                pl.BlockSpec((tq, KV_WIDTH), lambda b, i: (off + b * nblk + jnp.minimum(i + 1, nblk - 1), col))]

    return pl.pallas_call(
        _lat_attn_kernel,
        grid=(grp.bsz, nblk),
        in_specs=[pl.BlockSpec((tq, B_WIDTH), lambda b, i: (b * nblk + i, 0))]
                 + kv_specs(0, 0) + kv_specs(blk0, V_COL // KV_WIDTH)
                 + [pl.BlockSpec((1, B_KV_HEADS, n_ctx, HEAD_DIM), lambda b, i: (b, 0, 0, 0)),
                    pl.BlockSpec((1, B_KV_HEADS, n_ctx, 2 * HEAD_DIM), lambda b, i: (b, 0, 0, 0)),
                    pl.BlockSpec((B_HEADS, LANES), lambda b, i: (0, 0))],
        out_specs=pl.BlockSpec((tq, B_WIDTH), lambda b, i: (b * nblk + i, 0)),
        out_shape=jax.ShapeDtypeStruct((grp.tokens, B_WIDTH), F32),
        compiler_params=_cparams(("arbitrary", "arbitrary")),
        name="latent_attention",
    )(qr, kr, kr, kr, proj, proj, proj, k_ctx, v_ctx1, sink_t)


def s5_operators(a_re, a_im, log_dt, b_re, b_im, c_re, c_im):
    t = S5_CHUNK
    hi = lax.Precision.HIGHEST
    ks, ws, wsw, vs, a1s, a2s = [], [], [], [], [], []
    for d in range(2):
        are, aim = a_re[d].astype(F32), a_im[d].astype(F32)
        dt = jnp.exp(log_dt[d].astype(F32))[:, None]
        den = are * are + aim * aim
        steps = jnp.arange(t + 1, dtype=F32)[:, None, None]
        mag = jnp.exp(steps * (dt * are))
        pw_re = mag * jnp.cos(steps * (dt * aim))
        pw_im = mag * jnp.sin(steps * (dt * aim))
        ab_re, ab_im = pw_re[1], pw_im[1]
        f_re = ((ab_re - 1.0) * are + ab_im * aim) / den
        f_im = (ab_im * are - (ab_re - 1.0) * aim) / den
        bre, bim = b_re[d].astype(F32), b_im[d].astype(F32)
        bb_re = f_re[..., None] * bre - f_im[..., None] * bim
        bb_im = f_re[..., None] * bim + f_im[..., None] * bre
        cre, cim = c_re[d].astype(F32), c_im[d].astype(F32)
        pgr = pw_re.transpose(1, 0, 2)[:, :, None, :]
        pgi = pw_im.transpose(1, 0, 2)[:, :, None, :]
        cp_re = cre[:, None] * pgr - cim[:, None] * pgi
        cp_im = cre[:, None] * pgi + cim[:, None] * pgr
        m = (jnp.einsum('gkcp,gpd->gkcd', cp_re[:, :t], bb_re, precision=hi)
             - jnp.einsum('gkcp,gpd->gkcd', cp_im[:, :t], bb_im, precision=hi))
        s_i = jnp.arange(t)[:, None]
        t_i = jnp.arange(t)[None, :]
        lag = (t_i - s_i) if d == 0 else (s_i - t_i)
        blk = jnp.where((lag >= 0)[None, :, :, None, None], m[:, jnp.clip(lag, 0, t - 1)], 0.0)
        ks.append(blk.transpose(0, 1, 4, 2, 3).reshape(-1, S5_TC, S5_TC))
        pidx = (t - 1 - jnp.arange(t)) if d == 0 else jnp.arange(t)
        pr = pw_re[pidx].transpose(1, 0, 2)[:, :, None, :]
        pi = pw_im[pidx].transpose(1, 0, 2)[:, :, None, :]
        bbr = bb_re.transpose(0, 2, 1)[:, None]
        bbi = bb_im.transpose(0, 2, 1)[:, None]
        w_re = pr * bbr - pi * bbi
        w_im = pr * bbi + pi * bbr
        ws.append(jnp.concatenate([w_re, w_im], axis=-1).reshape(-1, S5_TC, 2 * S5_STATE))
        wsw.append(jnp.concatenate([w_im, w_re], axis=-1).reshape(-1, S5_TC, 2 * S5_STATE))
        kidx = (jnp.arange(t) + 1) if d == 0 else (t - jnp.arange(t))
        v = jnp.concatenate([cp_re[:, kidx], -cp_im[:, kidx]], axis=-1)
        vs.append(v.transpose(0, 3, 1, 2).reshape(-1, 2 * S5_STATE, S5_TC))
        a1s.append(jnp.concatenate([pw_re[t], pw_re[t]], axis=-1))
        a2s.append(jnp.concatenate([-pw_im[t], pw_im[t]], axis=-1))
    return (ks[0] + ks[1], jnp.concatenate(ws + wsw, axis=-1), jnp.concatenate(vs, axis=1),
            jnp.concatenate(a1s, axis=-1), jnp.concatenate(a2s, axis=-1))


def _to_chunks_kernel(x_ref, nw_ref, sc_ref, sh_ref, o_ref, hbuf):
    h = _norm_mod(x_ref[...], nw_ref[...], sc_ref[0], sh_ref[0])
    gpl = LANES // S5_GROUP
    for c in range(hbuf.shape[0]):
        hbuf[c] = h[:, c * LANES:(c + 1) * LANES]
    for c in range(hbuf.shape[0]):
        for t in range(S5_CHUNK):
            rows = hbuf[c, pl.ds(t, TOK_TILE // S5_CHUNK, stride=S5_CHUNK), :]
            for g in range(gpl):
                o_ref[c * gpl + g, :, t * S5_GROUP:(t + 1) * S5_GROUP] = rows[:, g * S5_GROUP:(g + 1) * S5_GROUP]


def s5_to_chunks(x, grp, nw, mod, rowmap):
    d = x.shape[1]
    ng = d // S5_GROUP
    tm = TOK_TILE
    st = grp.seq_tiles

    def mspec(k):
        return pl.BlockSpec((1, 1, d), lambda i: (rowmap(grp.tile0 + i), 0, k))

    return pl.pallas_call(
        _to_chunks_kernel,
        grid=(grp.tiles,),
        in_specs=[pl.BlockSpec((tm, d), lambda i: (grp.tile0 + i, 0)),
                  pl.BlockSpec((1, d), lambda i: (0, 0)), mspec(1), mspec(0)],
        out_specs=pl.BlockSpec((ng, CPT, S5_TC), lambda i: (0, (i % st) * grp.bsz + i // st, 0)),
        out_shape=jax.ShapeDtypeStruct((ng, grp.tokens // S5_CHUNK, S5_TC), F32),
        scratch_shapes=[pltpu.VMEM((d // LANES, tm, LANES), F32)],
        compiler_params=_cparams(("arbitrary",)),
        name="s5_to_chunks",
    )(x, nw, mod, mod)


def _s5_states_kernel(uf_ref, ub_ref, w_ref, a1_ref, a2_ref, s0_ref, stf_ref, stb_ref, fin_ref, dbuf, carry, *, bsz):
    r = pl.program_id(1)
    gb, tr, _ = uf_ref.shape
    sw = STATE_W
    span = CPT * bsz
    nspan = tr // span

    @pl.when(r == 0)
    def _():
        for g in range(gb):
            s0 = s0_ref[g]
            for d in range(2):
                s = s0[:, d * sw:(d + 1) * sw]
                carry[g, 2 * d] = s
                carry[g, 2 * d + 1] = pltpu.roll(s, sw // 2, axis=1)

    for g in range(gb):
        w = w_ref[g]
        for d, u_ref in enumerate((uf_ref, ub_ref)):
            dd = _dot(u_ref[g].astype(BF16), w[:, 2 * d * sw:2 * (d + 1) * sw])
            dbuf[g, d, 0] = dd[:, :sw]
            dbuf[g, d, 1] = dd[:, sw:]

    for g in range(gb):
        a1 = a1_ref[g]
        a2 = a2_ref[g]
        for d, st_ref in enumerate((stf_ref, stb_ref)):
            a1d = a1[:, d * sw:(d + 1) * sw]
            a2d = a2[:, d * sw:(d + 1) * sw]
            s = carry[g, 2 * d]
            x = carry[g, 2 * d + 1]
            for c in range(nspan * CPT):
                cc = c if d == 0 else nspan * CPT - 1 - c
                first = (cc // CPT) * span + cc % CPT
                for b in range(bsz):
                    st_ref[first + b * CPT:first + b * CPT + 1, g * sw:(g + 1) * sw] = s[b:b + 1]
                own = dbuf[g, d, 0, pl.ds(first, bsz, stride=CPT), :]
                swp = dbuf[g, d, 1, pl.ds(first, bsz, stride=CPT), :]
                s, x = a1d * s + a2d * x + own, a1d * x - a2d * s + swp
            carry[g, 2 * d] = s
            carry[g, 2 * d + 1] = x

    @pl.when(r == pl.num_programs(1) - 1)
    def _():
        for g in range(gb):
            fin_ref[g, :, :sw] = carry[g, 0]
            fin_ref[g, :, sw:] = carry[g, 2]


def s5_chunk_states_scan(u, w_bf16, a1, a2, s0g, bsz):
    g, r, tc = u.shape
    n = w_bf16.shape[2]
    sw = STATE_W
    gb = 8
    tr = min(r, 256)
    nblk = r // tr
    return pl.pallas_call(
        functools.partial(_s5_states_kernel, bsz=bsz),
        grid=(g // gb, nblk),
        in_specs=[pl.BlockSpec((gb, tr, tc), lambda i, j: (i, j, 0)),
                  pl.BlockSpec((gb, tr, tc), lambda i, j: (i, nblk - 1 - j, 0)),
                  pl.BlockSpec((gb, tc, n), lambda i, j: (i, 0, 0)),
                  pl.BlockSpec((gb, 1, 2 * sw), lambda i, j: (i, 0, 0)),
                  pl.BlockSpec((gb, 1, 2 * sw), lambda i, j: (i, 0, 0)),
                  pl.BlockSpec((gb, bsz, 2 * sw), lambda i, j: (i, 0, 0))],
        out_specs=[pl.BlockSpec((tr, gb * sw), lambda i, j: (j, i)),
                   pl.BlockSpec((tr, gb * sw), lambda i, j: (nblk - 1 - j, i)),
                   pl.BlockSpec((gb, bsz, 2 * sw), lambda i, j: (i, 0, 0))],
        out_shape=[jax.ShapeDtypeStruct((r, g * sw), F32),
                   jax.ShapeDtypeStruct((r, g * sw), F32),
                   jax.ShapeDtypeStruct((g, bsz, 2 * sw), F32)],
        scratch_shapes=[pltpu.VMEM((gb, 2, 2, tr, sw), F32), pltpu.VMEM((gb, 4, bsz, sw), F32)],
        compiler_params=_cparams(("arbitrary", "arbitrary")),
        name="s5_chunk_states_scan",
    )(u, u, w_bf16, a1.reshape(g, 1, 2 * sw), a2.reshape(g, 1, 2 * sw), s0g)


def _s5_out_kernel(u_ref, sf_ref, sb_ref, k_ref, v_ref, o_ref):
    gb = u_ref.shape[0]
    sw = STATE_W
    for g in range(gb):
        s = jnp.concatenate([sf_ref[:, g * sw:(g + 1) * sw], sb_ref[:, g * sw:(g + 1) * sw]], axis=1)
        o_ref[g] = _dot(u_ref[g].astype(BF16), k_ref[g]) + _dot(s.astype(BF16), v_ref[g])


def s5_outputs(u, st_f, st_b, k_bf16, v_bf16):
    g, r, tc = u.shape
    sw = STATE_W
    gb = 8
    tr = min(r, 512)
    return pl.pallas_call(
        _s5_out_kernel,
        grid=(g // gb, r // tr),
        in_specs=[pl.BlockSpec((gb, tr, tc), lambda i, j: (i, j, 0)),
                  pl.BlockSpec((tr, gb * sw), lambda i, j: (j, i)),
                  pl.BlockSpec((tr, gb * sw), lambda i, j: (j, i)),
                  pl.BlockSpec((gb, tc, tc), lambda i, j: (i, 0, 0)),
                  pl.BlockSpec((gb, 2 * sw, tc), lambda i, j: (i, 0, 0))],
        out_specs=pl.BlockSpec((gb, tr, tc), lambda i, j: (i, j, 0)),
        out_shape=jax.ShapeDtypeStruct((g, r, tc), F32),
        compiler_params=_cparams(("arbitrary", "arbitrary")),
        name="s5_outputs",
    )(u, st_f, st_b, k_bf16, v_bf16)


def s5_mixer(x, grp, nw, mod, rowmap, ops, s0):
    k_tot, w_tot, v_tot, a1, a2 = ops
    ng = k_tot.shape[0]
    bsz = grp.bsz
    sw = STATE_W
    u = s5_to_chunks(x, grp, nw, mod, rowmap)
    w = jnp.concatenate([w_tot[:, :, :sw], w_tot[:, :, 2 * sw:3 * sw], w_tot[:, :, sw:2 * sw], w_tot[:, :, 3 * sw:]], axis=-1)
    s0g = s0.transpose(3, 0, 1, 2, 4).reshape(ng, bsz, 2 * sw)
    st_f, st_b, final = s5_chunk_states_scan(u, w.astype(BF16), a1, a2, s0g, bsz)
    y = s5_outputs(u, st_f, st_b, k_tot.astype(BF16), v_tot.astype(BF16))
    final = final.reshape(ng, bsz, 2, 2, S5_STATE).transpose(1, 2, 3, 0, 4)
    return y, final


def _glu_kernel(yc_ref, yl_ref, x_ref, nw_ref, sc_ref, sh_ref, g1_ref, dsk_ref, wa_ref, wb_ref, o_ref, ybuf,
                *, first_tiles):
    gpl = LANES // S5_GROUP

    def from_chunks(y_ref):
        cpt = y_ref.shape[1]
        for c in range(ybuf.shape[0]):
            for t in range(S5_CHUNK):
                for g in range(gpl):
                    ybuf[c, t * cpt:(t + 1) * cpt, g * S5_GROUP:(g + 1) * S5_GROUP] = (
                        y_ref[c * gpl + g, :, t * S5_GROUP:(t + 1) * S5_GROUP])

    in_first = pl.program_id(0) < first_tiles

    @pl.when(in_first)
    def _():
        from_chunks(yc_ref)

    @pl.when(jnp.logical_not(in_first))
    def _():
        from_chunks(yl_ref)

    cpt = TOK_TILE // S5_CHUNK
    y = jnp.concatenate(
        [jnp.concatenate([ybuf[c, pl.ds(j, S5_CHUNK, stride=cpt), :] for c in range(ybuf.shape[0])], axis=1)
         for j in range(cpt)], axis=0)
    x = x_ref[...]
    y = y + dsk_ref[...] * _norm_mod(x, nw_ref[...], sc_ref[0], sh_ref[0])
    yb = jax.nn.gelu(y, approximate=True).astype(BF16)
    a = _dot(yb, wa_ref[...])
    b = _dot(yb, wb_ref[...])
    o_ref[...] = x + g1_ref[0] * (a * _sigmoid(b))


def glu_residual(y_chunks, x, nw, mod, rowmap, groups, dskip, wa_bf16, wb_bf16):
    n_tok, d = x.shape
    tm = TOK_TILE
    ng = d // S5_GROUP
    first, second = groups
    blk = (ng, CPT, S5_TC)

    def rows_of(grp, i):
        return (i % grp.seq_tiles) * grp.bsz + i // grp.seq_tiles

    return pl.pallas_call(
        functools.partial(_glu_kernel, first_tiles=first.tiles),
        grid=(n_tok // tm,),
        in_specs=[pl.BlockSpec(blk, lambda i: (0, rows_of(first, jnp.minimum(i, first.tiles - 1)), 0)),
                  pl.BlockSpec(blk, lambda i: (0, rows_of(second, jnp.maximum(i - first.tiles, 0)), 0)),
                  pl.BlockSpec((tm, d), lambda i: (i, 0)),
                  pl.BlockSpec((1, d), lambda i: (0, 0)),
                  _mod_spec(rowmap, 1, d), _mod_spec(rowmap, 0, d), _mod_spec(rowmap, 2, d),
                  pl.BlockSpec((1, d), lambda i: (0, 0)),
                  pl.BlockSpec((d, d), lambda i: (0, 0)),
                  pl.BlockSpec((d, d), lambda i: (0, 0))],
        out_specs=pl.BlockSpec((tm, d), lambda i: (i, 0)),
        out_shape=jax.ShapeDtypeStruct((n_tok, d), F32),
        scratch_shapes=[pltpu.VMEM((d // LANES, tm, LANES), F32)],
        compiler_params=_cparams(("arbitrary",)),
        name="glu_residual",
    )(y_chunks[0], y_chunks[1], x, nw, mod, mod, mod, dskip, wa_bf16, wb_bf16)


def _router_kernel(x_ref, nw_ref, sc_ref, sh_ref, wr_ref, h_ref, aff_ref):
    d = x_ref.shape[1]
    tm = x_ref.shape[0]
    h = _norm_mod(x_ref[...], nw_ref[...], sc_ref[0], sh_ref[0])
    h_ref[:, :d] = h
    tok = pl.program_id(0) * tm + lax.broadcasted_iota(jnp.int32, (tm, LANES), 0)
    h_ref[:, d:] = tok.astype(F32)
    logits = _dot_nt(wr_ref[...], h.astype(BF16))
    ex = jnp.exp(logits - jnp.max(logits, axis=0, keepdims=True))
    p = ex / jnp.sum(ex, axis=0, keepdims=True)
    for k in range(aff_ref.shape[0]):
        aff_ref[k] = p[:, k * LANES:(k + 1) * LANES]


def moe_router(x, nw, mod, rowmap, wr_t_bf16):
    n_tok, d = x.shape
    tm = TOK_TILE
    ne = wr_t_bf16.shape[0]
    return pl.pallas_call(
        _router_kernel,
        grid=(n_tok // tm,),
        in_specs=[pl.BlockSpec((tm, d), lambda i: (i, 0)),
                  pl.BlockSpec((1, d), lambda i: (0, 0)),
                  _mod_spec(rowmap, 4, d), _mod_spec(rowmap, 3, d),
                  pl.BlockSpec((ne, d), lambda i: (0, 0))],
        out_specs=[pl.BlockSpec((tm, d + LANES), lambda i: (i, 0)),
                   pl.BlockSpec((tm // LANES, ne, LANES), lambda i: (i, 0, 0))],
        out_shape=[jax.ShapeDtypeStruct((n_tok, d + LANES), F32),
                   jax.ShapeDtypeStruct((n_tok // LANES, ne, LANES), F32)],
        compiler_params=_cparams(("arbitrary",)),
        name="moe_router",
    )(x, nw, mod, mod, wr_t_bf16)


def _select_kernel(aff_ref, ut_ref, v8_ref, cidx_ref, meta_ref, inc_ref, *, cap, first, slot0):
    nt, ne, _ = aff_ref.shape
    aff = aff_ref[...]

    def count(mask):
        c = jnp.sum(jnp.where(mask, 1.0, 0.0), axis=0)
        return jnp.sum(c, axis=1, keepdims=True)

    def as_float(bits):
        return lax.bitcast_convert_type(bits, F32)

    def radix(k, bits):
        cand = bits | (jnp.int32(1) << (30 - k))
        return jnp.where(count(aff >= as_float(cand)[None]) >= cap, cand, bits)

    thr_bits = lax.fori_loop(0, 31, radix, jnp.zeros((ne, 1), jnp.int32))
    thr = as_float(thr_bits)[None]
    nxt = as_float(thr_bits + 1)[None]
    above = aff >= nxt
    bucket = (aff >= thr) & jnp.logical_not(above)
    need = cap - count(above)
    width = nxt - thr
    pos = jnp.where(bucket & (width > 0.0), (aff - thr) / width, 0.0)

    def refine(k, t):
        cand = t + lax.convert_element_type(jnp.int32(1) << (29 - k), F32) * (2.0 ** -30)
        return jnp.where(count(bucket & (pos >= cand[None])) >= need, cand, t)

    t = lax.fori_loop(0, 30, refine, jnp.zeros((ne, 1), F32))
    upper = bucket & (pos >= (t + 2.0 ** -30)[None])
    tied = bucket & (pos >= t[None]) & jnp.logical_not(upper)
    ut = ut_ref[...]

    def excl_rank(mask):
        m = jnp.where(mask, 1.0, 0.0)
        inc_ref[...] = _dot(m.reshape(nt * ne, LANES).astype(BF16), ut).reshape(nt, ne, LANES)

        def body(tt, carry):
            inc = inc_ref[tt]
            inc_ref[tt] = inc + carry
            return carry + inc[:, LANES - 1:LANES]

        lax.fori_loop(0, nt, body, jnp.zeros((ne, 1), F32))
        return inc_ref[...] - m

    sel = above | upper | (tied & (excl_rank(tied) < (need - count(upper))[None]))
    rank = excl_rank(sel)
    start = rank[:, :, 0:1]
    nsel = rank[:, :, LANES - 1:LANES] + jnp.where(sel[:, :, LANES - 1:LANES], 1.0, 0.0) - start
    lane3 = lax.broadcasted_iota(jnp.int32, (nt, ne, LANES), 2)
    meta_ref[...] = jnp.where(lane3 == 0, start + float(slot0), jnp.where(lane3 == 1, nsel, 0.0)).astype(jnp.int32)

    inc_ref[...] = jnp.where(sel, rank - start, -1.0)
    sub = lax.broadcasted_iota(jnp.int32, (LANES, LANES), 0).astype(F32)
    v8 = v8_ref[...]

    def tile_body(tt, carry):
        rho = inc_ref[tt]
        base = lax.convert_element_type(first + tt * LANES, F32)
        for e in range(ne):
            onehot = jnp.where(sub == rho[e:e + 1, :], 1.0, 0.0).astype(BF16)
            packed = _dot_nt(v8, onehot)
            cidx_ref[tt, e:e + 1, :] = (packed[0:1] + base).astype(jnp.int32)
        return carry

    lax.fori_loop(0, nt, tile_body, 0)


def moe_select(aff_t, cap, first, slot0):
    nt, ne, _ = aff_t.shape
    ut = jnp.triu(jnp.ones((LANES, LANES), BF16))
    v8 = jnp.zeros((8, LANES), BF16).at[0].set(jnp.arange(LANES).astype(BF16))
    blk = pl.BlockSpec((nt, ne, LANES), lambda i: (0, 0, 0))
    return pl.pallas_call(
        functools.partial(_select_kernel, cap=cap, first=first, slot0=slot0),
        grid=(1,),
        in_specs=[blk, pl.BlockSpec((LANES, LANES), lambda i: (0, 0)), pl.BlockSpec((8, LANES), lambda i: (0, 0))],
        out_specs=[blk, blk],
        out_shape=[jax.ShapeDtypeStruct((nt, ne, LANES), jnp.int32),
                   jax.ShapeDtypeStruct((nt, ne, LANES), jnp.int32)],
        scratch_shapes=[pltpu.VMEM((nt, ne, LANES), F32)],
        compiler_params=_cparams(("arbitrary",)),
        name="moe_select",
    )(aff_t, ut, v8)


def _lists_kernel(starts_ref, counts_ref, cidx_ref, idx_ref):
    nt, ne, _ = cidx_ref.shape
    idx_ref[...] = jnp.zeros_like(idx_ref)
    lane = lax.broadcasted_iota(jnp.int32, (1, LANES), 1)

    def tile(t, carry):
        for e in range(ne):
            s = starts_ref[e, t]
            c = counts_ref[e, t]
            j0 = s >> 7
            o = s & (LANES - 1)
            rolled = pltpu.roll(cidx_ref[t, e:e + 1, :], o, axis=1)
            end = o + c
            row0 = idx_ref[e, pl.ds(j0, 1), :]
            idx_ref[e, pl.ds(j0, 1), :] = jnp.where((lane >= o) & (lane < end), rolled, row0)
            row1 = idx_ref[e, pl.ds(j0 + 1, 1), :]
            idx_ref[e, pl.ds(j0 + 1, 1), :] = jnp.where(lane < end - LANES, rolled, row1)
        return carry

    lax.fori_loop(0, nt, tile, 0)


def moe_build_lists(starts, counts, cidx, rows):
    nt, ne, _ = cidx.shape
    rt = rows // LANES + 2
    out = pl.pallas_call(
        _lists_kernel,
        grid_spec=pltpu.PrefetchScalarGridSpec(
            num_scalar_prefetch=2,
            grid=(1,),
            in_specs=[pl.BlockSpec((nt, ne, LANES), lambda i, s, c: (0, 0, 0))],
            out_specs=pl.BlockSpec((ne, rt, LANES), lambda i, s, c: (0, 0, 0))),
        out_shape=jax.ShapeDtypeStruct((ne, rt, LANES), jnp.int32),
        compiler_params=_cparams(("arbitrary",)),
        name="moe_build_lists",
    )(starts, counts, cidx)
    return out.reshape(ne * rt * LANES)


def _expert_kernel(idx_ref, h_hbm, wr_ref, wg_ref, wu_ref, wd_ref, y_ref, xbuf, wgb, wub, wdb, sem, *, rows_pad):
    e = pl.program_id(0)
    ch = pl.program_id(1)
    nch = pl.num_programs(1)
    tr = xbuf.shape[1]
    d, f = wgb.shape
    step = e * nch + ch
    slot = step % 2

    @pl.when(ch == 0)
    def _():
        wgb[...] = wg_ref[0, 0].astype(BF16)
        wub[...] = wu_ref[0, 0].astype(BF16)
        wdb[...] = wd_ref[0, 0].astype(BF16)

    def row_copy(base, r, slot_i):
        return pltpu.make_async_copy(h_hbm.at[pl.ds(idx_ref[base + r], 1)], xbuf.at[slot_i, pl.ds(r, 1)], sem.at[slot_i])

    @pl.when(step == 0)
    def _():
        def group(j, carry):
            for u in range(8):
                row_copy(0, 8 * j + u, 0).start()
            return carry

        lax.fori_loop(0, tr // 8, group, 0)

    has_next = step + 1 < pl.num_programs(0) * nch
    last = ch == nch - 1
    e_n = jnp.where(has_next, jnp.where(last, e + 1, e), e)
    ch_n = jnp.where(has_next, jnp.where(last, 0, ch + 1), ch)
    base_n = e_n * rows_pad + ch_n * tr

    pltpu.make_async_copy(h_hbm.at[pl.ds(0, tr)], xbuf.at[slot], sem.at[slot]).wait()

    x = xbuf[slot, :, :d].astype(BF16)
    logits = _dot(x, wr_ref[...])
    lane = lax.broadcasted_iota(jnp.int32, logits.shape, 1)
    ne = pl.num_programs(0)
    logits = jnp.where(lane < ne, logits, NEG_INF)
    ex = jnp.exp(logits - jnp.max(logits, axis=1, keepdims=True))
    gate = jnp.sum(jnp.where(lane == e, ex, 0.0), axis=1, keepdims=True) / jnp.sum(ex, axis=1, keepdims=True)

    fb = f // FFN_SPLIT
    rb = tr // FFN_SPLIT
    y = None
    for nb in range(FFN_SPLIT):
        for r in range(nb * rb, (nb + 1) * rb):
            row_copy(base_n, r, 1 - slot).start()
        cols = slice(nb * fb, (nb + 1) * fb)
        hmid = (_silu(_dot(x, wgb[:, cols])) * _dot(x, wub[:, cols])).astype(BF16)
        part = _dot(hmid, wdb[cols, :])
        y = part if y is None else y + part
    y_ref[0, :, :d] = y * gate
    y_ref[0, :, d:] = xbuf[slot, :, d:]

    @pl.when(jnp.logical_not(has_next))
    def _():
        pltpu.make_async_copy(h_hbm.at[pl.ds(0, tr)], xbuf.at[1 - slot], sem.at[1 - slot]).wait()


def moe_experts(idx, rows, h_ext, wr_pad_bf16, w_gate, w_up, w_down, layer):
    _, ne, d, f = w_gate.shape
    rows_pad = idx.shape[0] // ne
    dx = h_ext.shape[1]
    tr = 640
    return pl.pallas_call(
        functools.partial(_expert_kernel, rows_pad=rows_pad),
        grid_spec=pltpu.PrefetchScalarGridSpec(
            num_scalar_prefetch=1,
            grid=(ne, rows // tr),
            in_specs=[pl.BlockSpec(memory_space=pl.ANY),
                      pl.BlockSpec((d, LANES), lambda e, c, idx: (0, 0)),
                      pl.BlockSpec((1, 1, d, f), lambda e, c, idx: (layer, e, 0, 0)),
                      pl.BlockSpec((1, 1, d, f), lambda e, c, idx: (layer, e, 0, 0)),
                      pl.BlockSpec((1, 1, f, d), lambda e, c, idx: (layer, e, 0, 0))],
            out_specs=pl.BlockSpec((1, tr, dx), lambda e, c, idx: (e, c, 0)),
            scratch_shapes=[pltpu.VMEM((2, tr, dx), F32),
                            pltpu.VMEM((d, f), BF16), pltpu.VMEM((d, f), BF16), pltpu.VMEM((f, d), BF16),
                            pltpu.SemaphoreType.DMA((2,))]),
        out_shape=jax.ShapeDtypeStruct((ne, rows, dx), F32),
        compiler_params=_cparams(("arbitrary", "arbitrary")),
        name="moe_experts",
    )(idx, h_ext, wr_pad_bf16, w_gate, w_up, w_down)


def _combine_kernel(starts_ref, y_hbm, x_ref, g2_ref, *rest, first_tiles):
    if first_tiles is None:
        o_ref, acc, stage, sem = rest
    else:
        oc_ref, ol_ref, acc, stage, sem = rest
    tb = pl.program_id(0)
    tm, d = x_ref.shape
    ne = y_hbm.shape[0]
    slot = tb % 2

    def chunk_copy(e, src_row, dst_row, slot_i):
        return pltpu.make_async_copy(y_hbm.at[e, pl.ds(src_row, 8)], stage.at[slot_i, pl.ds(dst_row, 8)], sem.at[slot_i])

    def spans(tile):
        out = []
        for e in range(ne):
            s0 = starts_ref[e, tile]
            s1 = starts_ref[e, tile + 1]
            a = (s0 >> 3) << 3
            out.append((a, jnp.where(s1 > s0, (s1 - a + 7) >> 3, 0)))
        return out

    def fetch(tile, slot_i):
        off = jnp.int32(0)
        for e, (a, nchunk) in enumerate(spans(tile)):
            def issue(c, carry, e=e, a=a, off=off):
                chunk_copy(e, pl.multiple_of(a + 8 * c, 8), pl.multiple_of(off + 8 * c, 8), slot_i).start()
                return carry

            lax.fori_loop(0, nchunk, issue, 0)
            off = off + 8 * nchunk

    @pl.when(tb == 0)
    def _():
        stage[...] = jnp.zeros(stage.shape, F32)
        fetch(tb, slot)

    @pl.when(tb + 1 < pl.num_programs(0))
    def _():
        fetch(tb + 1, 1 - slot)

    off = jnp.int32(0)
    for _, nchunk in spans(tb):
        off = off + 8 * nchunk

    def drain(c, carry):
        chunk_copy(0, 0, 0, slot).wait()
        return carry

    lax.fori_loop(0, off >> 3, drain, 0)

    acc[...] = jnp.zeros_like(acc)
    want = (lax.broadcasted_iota(jnp.int32, (tm, KCH), 0) + tb * tm).astype(F32)

    def fold(kc, carry):
        rows = stage[slot, pl.ds(pl.multiple_of(kc * KCH, KCH), KCH), :]
        tok = rows[:, d:].T[0:1, :]
        fresh = lax.broadcasted_iota(jnp.int32, (tm, KCH), 1) + kc * KCH < off
        onehot = jnp.where((want == tok) & fresh, 1.0, 0.0).astype(BF16)
        y = rows[:, :d]
        hi = y.astype(BF16)
        lo = (y - hi.astype(F32)).astype(BF16)
        acc[...] += _dot(onehot, hi) + _dot(onehot, lo)
        return carry

    lax.fori_loop(0, (off + KCH - 1) >> 8, fold, 0)
    res = x_ref[...] + g2_ref[0] * acc[...]
    if first_tiles is None:
        o_ref[...] = res
    else:
        @pl.when(tb < first_tiles)
        def _():
            oc_ref[...] = res

        @pl.when(tb >= first_tiles)
        def _():
            ol_ref[...] = res


def moe_combine(starts, y, x, mod, rowmap, split=None):
    n_tok, d = x.shape
    ne, _, dx = y.shape
    tm = TOK_TILE
    stage_rows = -(-(ne * tm + ne * 16) // KCH) * KCH
    if split is None:
        out_specs = pl.BlockSpec((tm, d), lambda i, s: (i, 0))
        out_shape = jax.ShapeDtypeStruct((n_tok, d), F32)
        first_tiles = None
    else:
        first_tiles = split[0]
        out_specs = [pl.BlockSpec((tm, d), lambda i, s: (jnp.minimum(i, first_tiles - 1), 0)),
                     pl.BlockSpec((tm, d), lambda i, s: (jnp.maximum(i - first_tiles, 0), 0))]
        out_shape = [jax.ShapeDtypeStruct((split[0] * tm, d), F32), jax.ShapeDtypeStruct((split[1] * tm, d), F32)]
    return pl.pallas_call(
        functools.partial(_combine_kernel, first_tiles=first_tiles),
        grid_spec=pltpu.PrefetchScalarGridSpec(
            num_scalar_prefetch=1,
            grid=(n_tok // tm,),
            in_specs=[pl.BlockSpec(memory_space=pl.ANY),
                      pl.BlockSpec((tm, d), lambda i, s: (i, 0)),
                      pl.BlockSpec((1, 1, d), lambda i, s: (rowmap(i), 0, 5))],
            out_specs=out_specs,
            scratch_shapes=[pltpu.VMEM((tm, d), F32), pltpu.VMEM((2, stage_rows, dx), F32),
                            pltpu.SemaphoreType.DMA((2,))]),
        out_shape=out_shape,
        compiler_params=_cparams(("arbitrary",)),
        name="moe_combine",
    )(starts, y, x, mod)


def moe_layer(x, nw, mod, rowmap, groups, wr, w_gate, w_up, w_down, layer, split=False):
    n_tok, d = x.shape
    ne = wr.shape[1]
    h_ext, aff_t = moe_router(x, nw, mod, rowmap, wr.T.astype(BF16))
    cidx_parts, meta_parts = [], []
    rows = 0
    for grp in groups:
        cap = EC_FACTOR * grp.tokens // ne
        t0 = grp.row0 // LANES
        cidx, meta = moe_select(aff_t[t0:t0 + grp.tokens // LANES], cap, grp.row0, rows)
        cidx_parts.append(cidx)
        meta_parts.append(meta[:, :, :2])
        rows += cap
    cidx = jnp.concatenate(cidx_parts, axis=0)
    meta = jnp.concatenate(meta_parts, axis=0)
    starts = meta[:, :, 0].T
    counts = meta[:, :, 1].T
    idx = moe_build_lists(starts, counts, cidx, rows)
    per = TOK_TILE // LANES
    starts_blk = jnp.concatenate([starts[:, ::per], jnp.full((ne, 1), rows, jnp.int32)], axis=1)
    wr_pad = jnp.zeros((d, LANES), BF16).at[:, :ne].set(wr.astype(BF16))
    y = moe_experts(idx, rows, h_ext, wr_pad, w_gate, w_up, w_down, layer)
    return moe_combine(starts_blk, y, x, mod, rowmap, (groups[0].tiles, groups[1].tiles) if split else None)


def kernel(x_prompt, x_sample, cache_k, cache_v, state_hgrn, state_s5, c, c_ctx, norm_w, ada_w, ada_b, w_in_ab, w_out_ab, hgrn_lb_logits, hgrn_norm_w, q_norm_w, k_norm_w, attn_sink, s5_a_re, s5_a_im, s5_log_dt, s5_b_re, s5_b_im, s5_c_re, s5_c_im, s5_d, glu_w_a, glu_w_b, router_w, exp_w_gate, exp_w_up, exp_w_down):
    b_ctx, l_ctx, d = x_prompt.shape
    b_lat, l_lat, _ = x_sample.shape
    depth = norm_w.shape[0]
    ctx = Group(0, b_ctx, l_ctx)
    lat = Group(ctx.tokens, b_lat, l_lat)
    groups = (ctx, lat)

    def rowmap(i):
        return jnp.where(i < ctx.tiles, i // ctx.seq_tiles, b_ctx + (i - ctx.tiles) // lat.seq_tiles)

    cond = jnp.concatenate([c_ctx[None, :], c, jnp.zeros((8 - 1 - b_lat, d), F32)], axis=0)
    mod_small = ada_modulation(cond, ada_w, ada_b)
    seq_rows = jnp.concatenate([jnp.zeros((b_ctx,), jnp.int32), 1 + jnp.arange(b_lat, dtype=jnp.int32)])
    mods = mod_small[:, seq_rows][:, :, None, :]

    x = jnp.concatenate([x_prompt.reshape(ctx.tokens, d), x_sample.reshape(lat.tokens, d)], axis=0)
    rope = rope_tables(l_lat)
    ks, vs, hs, ss = [], [], [], []
    for l in range(depth):
        mod = mods[l]
        nw1 = norm_w[l, 0].reshape(1, d)
        nw2 = norm_w[l, 1].reshape(1, d)
        if l % 2 == 0:
            e = l // 2
            proj = norm_mod_matmul(x, nw1, mod, rowmap, 1, 0, w_in_ab[e].astype(BF16))
            zero_state = jnp.zeros((b_ctx, 2, A_HEADS, A_DK, A_DK), F32)
            of_c, ob_c, st_c = hgrn2_mixer(proj, ctx, hgrn_lb_logits, zero_state, e)
            of_l, ob_l, _ = hgrn2_mixer(proj, lat, hgrn_lb_logits, jnp.swapaxes(state_hgrn[:, e], -1, -2), e)
            hs.append(jnp.swapaxes(st_c, -1, -2))
            qn_c, kn_c = qk_prepare(proj, ctx, q_norm_w[e], k_norm_w[e], None)
            att_c = context_attention(qn_c, kn_c, proj, ctx, attn_sink[e])
            ks.append(kn_c.reshape(b_ctx, l_ctx, B_KV_HEADS, HEAD_DIM).transpose(0, 2, 1, 3))
            vs.append(proj[:ctx.tokens, V_COL:].reshape(b_ctx, l_ctx, B_KV_HEADS, HEAD_DIM).transpose(0, 2, 1, 3))
            qr_l, kr_l = qk_prepare(proj, lat, q_norm_w[e], k_norm_w[e], rope)
            att_l = latent_attention(qr_l, kr_l, proj, lat, cache_k[:, e], cache_v[:, e], attn_sink[e])
            x = even_out_proj((of_c, of_l), (ob_c, ob_l), (att_c, att_l), proj, x, mod, rowmap, groups,
                              hgrn_norm_w[e].reshape(1, A_DK), w_out_ab[e].astype(BF16))
        else:
            o = l // 2
            ops = s5_operators(s5_a_re[o], s5_a_im[o], s5_log_dt[o], s5_b_re[o], s5_b_im[o], s5_c_re[o], s5_c_im[o])
            zero_s5 = jnp.zeros((b_ctx, 2, 2, d // S5_GROUP, S5_STATE), F32)
            y_c, fin_c = s5_mixer(x, ctx, nw1, mod, rowmap, ops, zero_s5)
            y_l, _ = s5_mixer(x, lat, nw1, mod, rowmap, ops, state_s5[:, o])
            ss.append(fin_c)
            x = glu_residual((y_c, y_l), x, nw1, mod, rowmap, groups, s5_d[o].reshape(1, d),
                             glu_w_a[o].astype(BF16), glu_w_b[o].astype(BF16))
        x = moe_layer(x, nw2, mod, rowmap, groups, router_w[l], exp_w_gate, exp_w_up, exp_w_down, l,
                      split=(l == depth - 1))
    y_prompt = x[0].reshape(b_ctx, l_ctx, d)
    y_sample = x[1].reshape(b_lat, l_lat, d)
    return (y_prompt, y_sample, jnp.stack(ks, axis=1), jnp.stack(vs, axis=1),
            jnp.stack(hs, axis=1), jnp.stack(ss, axis=1))
```

```python
import functools

import jax
import jax.numpy as jnp
from jax import lax
from jax.experimental import pallas as pl
from jax.experimental.pallas import tpu as pltpu

F32 = jnp.float32
BF16 = jnp.bfloat16

A_HEADS = 4
A_DK = 128
A_WIDTH = A_HEADS * A_DK
B_HEADS = 8
B_KV_HEADS = 2
HEAD_DIM = 64
B_GROUP = B_HEADS // B_KV_HEADS
B_WIDTH = B_HEADS * HEAD_DIM
KV_WIDTH = B_KV_HEADS * HEAD_DIM
Q_COL = 5 * A_WIDTH
K_COL = Q_COL + B_WIDTH
V_COL = K_COL + KV_WIDTH
WINDOW = 128
GRID_W = 64
ROPE_THETA = 10000.0
S5_GROUP = 16
S5_STATE = 64
S5_CHUNK = 16
S5_TC = S5_CHUNK * S5_GROUP
N_EXPERTS = 16
EC_FACTOR = 2
EPS = 1e-6
NEG_INF = -1e30

HGRN_C = 128
HGRN_SB = 16
TOK_TILE = 256
LANES = 128
VMEM_LIMIT = 56 * 1024 * 1024
STATE_W = 2 * S5_STATE
KCH = 256
CPT = TOK_TILE // S5_CHUNK
FFN_SPLIT = 4


def _cparams(sem):
    return pltpu.CompilerParams(dimension_semantics=sem, vmem_limit_bytes=VMEM_LIMIT)


def _sigmoid(x):
    return 1.0 / (1.0 + jnp.exp(-x))


def _silu(x):
    return x * _sigmoid(x)


def _norm_mod(x, nw, sc, sh):
    ms = jnp.mean(x * x, axis=-1, keepdims=True)
    return (x * lax.rsqrt(ms + EPS) * nw) * (1.0 + sc) + sh


def _dot(a, b):
    return jnp.dot(a, b, preferred_element_type=F32)


def _dot_nt(a, b):
    return lax.dot_general(a, b, (((1,), (1,)), ((), ())), preferred_element_type=F32)


def _split3(x):
    hi = x.astype(BF16)
    r1 = x - hi.astype(F32)
    mid = r1.astype(BF16)
    lo = (r1 - mid.astype(F32)).astype(BF16)
    return hi, mid, lo


class Group:
    def __init__(self, row0, bsz, seq):
        self.row0, self.bsz, self.seq = row0, bsz, seq
        self.tokens = bsz * seq
        self.tile0 = row0 // TOK_TILE
        self.tiles = self.tokens // TOK_TILE
        self.seq_tiles = seq // TOK_TILE


def _ada_kernel(c_ref, w_ref, b_ref, o_ref):
    s = _silu(c_ref[...])
    o_ref[0] = _dot(s.astype(BF16), w_ref[0].astype(BF16)) + b_ref[0]


def ada_modulation(cond, ada_w, ada_b):
    depth, d, n = ada_w.shape
    rows = cond.shape[0]
    tn = 1536
    return pl.pallas_call(
        _ada_kernel,
        grid=(depth, n // tn),
        in_specs=[pl.BlockSpec((rows, d), lambda l, j: (0, 0)),
                  pl.BlockSpec((1, d, tn), lambda l, j: (l, 0, j)),
                  pl.BlockSpec((1, 1, tn), lambda l, j: (l, 0, j))],
        out_specs=pl.BlockSpec((1, rows, tn), lambda l, j: (l, 0, j)),
        out_shape=jax.ShapeDtypeStruct((depth, rows, n), F32),
        compiler_params=_cparams(("arbitrary", "arbitrary")),
        name="ada_modulation",
    )(cond, ada_w, ada_b.reshape(depth, 1, n))


def _mod_spec(rowmap, k, d):
    return pl.BlockSpec((1, 1, d), lambda i: (rowmap(i), 0, k))


def _inproj_kernel(x_ref, nw_ref, sc_ref, sh_ref, w_ref, o_ref):
    h = _norm_mod(x_ref[...], nw_ref[...], sc_ref[0], sh_ref[0])
    o_ref[...] = _dot(h.astype(BF16), w_ref[...])


def norm_mod_matmul(x, nw, mod, rowmap, k_sc, k_sh, w_bf16):
    n_tok, d = x.shape
    n = w_bf16.shape[1]
    tm = TOK_TILE
    return pl.pallas_call(
        _inproj_kernel,
        grid=(n_tok // tm,),
        in_specs=[pl.BlockSpec((tm, d), lambda i: (i, 0)),
                  pl.BlockSpec((1, d), lambda i: (0, 0)),
                  _mod_spec(rowmap, k_sc, d), _mod_spec(rowmap, k_sh, d),
                  pl.BlockSpec((d, n), lambda i: (0, 0))],
        out_specs=pl.BlockSpec((tm, n), lambda i: (i, 0)),
        out_shape=jax.ShapeDtypeStruct((n_tok, n), F32),
        compiler_params=_cparams(("arbitrary",)),
        name="norm_mod_matmul",
    )(x, nw, mod, mod, w_bf16)


def _two_group_specs(groups, shape_of):
    first, second = groups
    return [pl.BlockSpec(shape_of, lambda i: (jnp.minimum(i, first.tiles - 1), 0)),
            pl.BlockSpec(shape_of, lambda i: (jnp.maximum(i - first.tiles, 0), 0))]


def _outproj_kernel(ofc_ref, ofl_ref, obc_ref, obl_ref, atc_ref, atl_ref, ga_ref, x_ref, g1_ref, hw_ref, w_ref,
                    o_ref, *, first_tiles):
    in_first = pl.program_id(0) < first_tiles
    o = jnp.where(in_first, ofc_ref[...] + obc_ref[...], ofl_ref[...] + obl_ref[...])
    o_att = jnp.where(in_first, atc_ref[...], atl_ref[...])
    gate = _silu(ga_ref[...])
    hw = hw_ref[...]
    parts = []
    for h in range(A_HEADS):
        sl = slice(h * A_DK, (h + 1) * A_DK)
        oh = o[:, sl]
        ms = jnp.mean(oh * oh, axis=-1, keepdims=True)
        parts.append(((oh * lax.rsqrt(ms + EPS) * hw) * gate[:, sl]).astype(BF16))
    parts.append(o_att.astype(BF16))
    y = _dot(jnp.concatenate(parts, axis=1), w_ref[...])
    o_ref[...] = x_ref[...] + g1_ref[0] * y


def even_out_proj(o_f, o_b, o_att, proj, x, mod, rowmap, groups, hw, w_bf16):
    n_tok, d = x.shape
    tm = TOK_TILE
    aw = A_WIDTH
    return pl.pallas_call(
        functools.partial(_outproj_kernel, first_tiles=groups[0].tiles),
        grid=(n_tok // tm,),
        in_specs=_two_group_specs(groups, (tm, aw)) + _two_group_specs(groups, (tm, aw))
                 + _two_group_specs(groups, (tm, B_WIDTH))
                 + [pl.BlockSpec((tm, aw), lambda i: (i, 4)),
                    pl.BlockSpec((tm, d), lambda i: (i, 0)),
                    _mod_spec(rowmap, 2, d),
                    pl.BlockSpec((1, A_DK), lambda i: (0, 0)),
                    pl.BlockSpec((aw + B_WIDTH, d), lambda i: (0, 0))],
        out_specs=pl.BlockSpec((tm, d), lambda i: (i, 0)),
        out_shape=jax.ShapeDtypeStruct((n_tok, d), F32),
        compiler_params=_cparams(("arbitrary",)),
        name="even_out_proj",
    )(o_f[0], o_f[1], o_b[0], o_b[1], o_att[0], o_att[1], proj, x, mod, hw, w_bf16)


def _hgrn_chunk(q, k, v, g, st, msel, rev):
    c = q.shape[0]
    nb = c // HGRN_SB
    row = lax.broadcasted_iota(jnp.int32, (c, c), 0)
    col = lax.broadcasted_iota(jnp.int32, (c, c), 1)
    tri = jnp.where((col >= row) if rev else (col <= row), 1.0, 0.0).astype(BF16)
    gh, gm, gl = _split3(g)
    b = _dot(tri, gh) + _dot(tri, gm) + _dot(tri, gl)
    b_edge = b[0:1] if rev else b[c - 1:c]
    qs = q * jnp.exp(b)
    kdec = k * jnp.exp(b_edge - b)

    lk = jnp.log(jnp.maximum(k, 0.0))
    half = HGRN_SB // 2
    zero_half = jnp.zeros((half, A_DK), F32)
    slabs = []
    for i in range(nb):
        sl = slice(i * HGRN_SB, (i + 1) * HGRN_SB)
        bi, qi, lki = b[sl], q[sl], lk[sl]
        ci = bi - lki
        pieces = []
        for s in range(HGRN_SB):
            s_half = s // half
            halves = []
            for hh in range(2):
                rows = slice(hh * half, (hh + 1) * half)
                if (hh > s_half) if rev else (hh < s_half):
                    halves.append(zero_half)
                    continue
                d = bi[rows] - ci[s:s + 1]
                if hh == s_half:
                    d = jnp.minimum(d, lki[s:s + 1])
                halves.append(qi[rows] * jnp.exp(d))
            pieces.append(jnp.concatenate(halves, axis=0).astype(BF16))
        slabs.append(jnp.concatenate(pieces, axis=1))
    a_loc = _dot(jnp.concatenate(slabs, axis=0), msel)

    lane = lax.broadcasted_iota(jnp.int32, (HGRN_SB, LANES), 1)
    rloc = lax.broadcasted_iota(jnp.int32, (HGRN_SB, LANES), 0)
    dmask = ((lane >= rloc) & (lane < HGRN_SB)) if rev else (lane <= rloc)
    krow = lax.broadcasted_iota(jnp.int32, (c, A_DK), 0)
    att_rows = []
    for i in range(nb):
        sl = slice(i * HGRN_SB, (i + 1) * HGRN_SB)
        a_d = jnp.where(dmask, a_loc[sl], 0.0)
        if i > 0:
            a_d = pltpu.roll(a_d, i * HGRN_SB, axis=1)
        a_i = a_d[:, :c]
        has_off = (i < nb - 1) if rev else (i > 0)
        if has_off:
            edge = (i + 1) * HGRN_SB if rev else i * HGRN_SB
            r = b[edge:edge + 1] if rev else b[edge - 1:edge]
            qp = q[sl] * jnp.exp(b[sl] - r)
            live = (krow >= edge) if rev else (krow < edge)
            kp = jnp.where(live, k * jnp.exp(jnp.minimum(r - b, 0.0)), 0.0)
            a_i = a_i + _dot_nt(qp.astype(BF16), kp.astype(BF16))
        att_rows.append(a_i)
    att = jnp.concatenate(att_rows, axis=0)

    vb = v.astype(BF16)
    o = _dot(att.astype(BF16), vb) + _dot_nt(qs.astype(BF16), st.astype(BF16))
    st_new = st * jnp.exp(b_edge) + _dot(v.T.astype(BF16), kdec.astype(BF16))
    return o, st_new


def _hgrn_kernel(qf_ref, vf_ref, ff_ref, qb_ref, vb_ref, fb_ref, lbl_ref, msel_ref, s0_ref,
                 of_ref, ob_ref, sout_ref, st_ref, *, layer):
    c_idx = pl.program_id(1)

    @pl.when(c_idx == 0)
    def _():
        st_ref[...] = s0_ref[0]

    lg = lbl_ref[...]
    ex = jnp.exp(lg - jnp.max(lg, axis=0, keepdims=True))
    pr = ex / jnp.sum(ex, axis=0, keepdims=True)
    lb = jnp.zeros_like(pr[0])
    for e in range(1, layer + 1):
        lb = lb + pr[e]
    msel = msel_ref[...]

    for d, (q_ref, v_ref, f_ref, o_ref) in enumerate(((qf_ref, vf_ref, ff_ref, of_ref),
                                                       (qb_ref, vb_ref, fb_ref, ob_ref))):
        q_all = _silu(q_ref[...])
        v_all = v_ref[...]
        lbd = lb[d:d + 1]
        forget = lbd + (1.0 - lbd) * _sigmoid(f_ref[...])
        k_all = 1.0 - forget
        g_all = jnp.log(forget)
        for h in range(A_HEADS):
            sl = slice(h * A_DK, (h + 1) * A_DK)
            o, st_new = _hgrn_chunk(q_all[:, sl], k_all[:, sl], v_all[:, sl], g_all[:, sl],
                                    st_ref[d, h], msel, rev=(d == 1))
            o_ref[:, sl] = o
            st_ref[d, h] = st_new

    @pl.when(c_idx == pl.num_programs(1) - 1)
    def _():
        sout_ref[0] = st_ref[...]


def hgrn2_mixer(proj, grp, lb_logits, s0t, layer):
    c = HGRN_C
    nc = grp.seq // c
    blk0 = grp.row0 // c
    aw = A_WIDTH
    msel = jnp.repeat(jnp.eye(HGRN_SB, LANES, dtype=BF16), A_DK, axis=0)

    def fwd(col):
        return pl.BlockSpec((c, aw), lambda b, i: (blk0 + b * nc + i, col))

    def bwd(col):
        return pl.BlockSpec((c, aw), lambda b, i: (blk0 + b * nc + nc - 1 - i, col))

    st_spec = pl.BlockSpec((1, 2, A_HEADS, A_DK, A_DK), lambda b, i: (b, 0, 0, 0, 0))
    return pl.pallas_call(
        functools.partial(_hgrn_kernel, layer=layer),
        grid=(grp.bsz, nc),
        in_specs=[fwd(0), fwd(3), fwd(1), bwd(0), bwd(3), bwd(2),
                  pl.BlockSpec(lb_logits.shape, lambda b, i: (0, 0, 0)),
                  pl.BlockSpec(msel.shape, lambda b, i: (0, 0)),
                  st_spec],
        out_specs=[pl.BlockSpec((c, aw), lambda b, i: (b * nc + i, 0)),
                   pl.BlockSpec((c, aw), lambda b, i: (b * nc + nc - 1 - i, 0)),
                   st_spec],
        out_shape=[jax.ShapeDtypeStruct((grp.tokens, aw), F32),
                   jax.ShapeDtypeStruct((grp.tokens, aw), F32),
                   jax.ShapeDtypeStruct((grp.bsz, 2, A_HEADS, A_DK, A_DK), F32)],
        scratch_shapes=[pltpu.VMEM((2, A_HEADS, A_DK, A_DK), F32)],
        compiler_params=_cparams(("arbitrary", "arbitrary")),
        name="hgrn2_mixer",
    )(proj, proj, proj, proj, proj, proj, lb_logits, msel, s0t)


def _head_norm(x, w, gmat):
    hi, mid, lo = _split3(x * x)
    ms = _dot(hi, gmat) + _dot(mid, gmat) + _dot(lo, gmat)
    return x * lax.rsqrt(ms + EPS) * w


def _rope(x, cos, sin_signed):
    width = x.shape[1]
    lane = lax.broadcasted_iota(jnp.int32, x.shape, 1)
    nxt = pltpu.roll(x, width - 1, axis=1)
    prv = pltpu.roll(x, 1, axis=1)
    partner = jnp.where(lane % 2 == 0, nxt, prv)
    return x * cos + partner * sin_signed


def _qkprep_kernel(*refs, rope):
    if rope:
        q_ref, k_ref, qw_ref, kw_ref, gm_ref, cos_ref, sin_ref, qo_ref, ko_ref = refs
    else:
        q_ref, k_ref, qw_ref, kw_ref, gm_ref, qo_ref, ko_ref = refs
    gm = gm_ref[...]
    qn = _head_norm(q_ref[...], qw_ref[...], gm)
    kn = _head_norm(k_ref[...], kw_ref[...], gm[:KV_WIDTH, :KV_WIDTH])
    if rope:
        cos = cos_ref[...]
        sin = sin_ref[...]
        qn = _rope(qn, cos, sin)
        kn = _rope(kn, cos[:, :KV_WIDTH], sin[:, :KV_WIDTH])
    qo_ref[...] = (qn * (HEAD_DIM ** -0.5)).astype(BF16)
    ko_ref[...] = kn


def qk_prepare(proj, grp, qw, kw, rope_tabs):
    tm = TOK_TILE
    st = grp.seq_tiles
    gidx = jnp.arange(B_WIDTH) // HEAD_DIM
    gmat = jnp.where(gidx[:, None] == gidx[None, :], 1.0 / HEAD_DIM, 0.0).astype(BF16)
    qw_t = jnp.tile(qw, B_HEADS).reshape(1, B_WIDTH)
    kw_t = jnp.tile(kw, B_KV_HEADS).reshape(1, KV_WIDTH)
    in_specs = [pl.BlockSpec((tm, B_WIDTH), lambda i: (grp.tile0 + i, Q_COL // B_WIDTH)),
                pl.BlockSpec((tm, KV_WIDTH), lambda i: (grp.tile0 + i, K_COL // KV_WIDTH)),
                pl.BlockSpec((1, B_WIDTH), lambda i: (0, 0)),
                pl.BlockSpec((1, KV_WIDTH), lambda i: (0, 0)),
                pl.BlockSpec((B_WIDTH, B_WIDTH), lambda i: (0, 0))]
    args = [proj, proj, qw_t, kw_t, gmat]
    if rope_tabs is not None:
        in_specs += [pl.BlockSpec((tm, B_WIDTH), lambda i: (i % st, 0)),
                     pl.BlockSpec((tm, B_WIDTH), lambda i: (i % st, 0))]
        args += list(rope_tabs)
    return pl.pallas_call(
        functools.partial(_qkprep_kernel, rope=rope_tabs is not None),
        grid=(grp.tiles,),
        in_specs=in_specs,
        out_specs=[pl.BlockSpec((tm, B_WIDTH), lambda i: (i, 0)),
                   pl.BlockSpec((tm, KV_WIDTH), lambda i: (i, 0))],
        out_shape=[jax.ShapeDtypeStruct((grp.tokens, B_WIDTH), BF16),
                   jax.ShapeDtypeStruct((grp.tokens, KV_WIDTH), F32)],
        compiler_params=_cparams(("arbitrary",)),
        name="qk_prepare",
    )(*args)


def rope_tables(seq):
    pos = jnp.arange(seq)
    row = (pos // GRID_W).astype(F32)
    col = (pos % GRID_W).astype(F32)
    n_pair = HEAD_DIM // 4
    freqs = ROPE_THETA ** (-jnp.arange(n_pair, dtype=F32) / n_pair)
    ang = jnp.concatenate([row[:, None] * freqs, col[:, None] * freqs], axis=-1)
    cos = jnp.repeat(jnp.cos(ang), 2, axis=-1)
    sin = jnp.repeat(jnp.sin(ang), 2, axis=-1) * jnp.tile(jnp.array([-1.0, 1.0], F32), HEAD_DIM // 2)
    return jnp.tile(cos, (1, B_HEADS)), jnp.tile(sin, (1, B_HEADS))


def _value_with_ones(v2, kvh):
    lane = lax.broadcasted_iota(jnp.int32, v2.shape, 1)
    if kvh == 1:
        v2 = pltpu.roll(v2, HEAD_DIM, axis=1)
    return jnp.where(lane < HEAD_DIM, v2, 1.0).astype(BF16)


def _group_attention(q, kk, vv1, valid, sink_ref, kvh, o_ref):
    tq = q.shape[0]
    heads = [kvh * B_GROUP + gq for gq in range(B_GROUP)]
    qs = jnp.concatenate([q[:, h * HEAD_DIM:(h + 1) * HEAD_DIM] for h in heads], axis=0)
    sink = jnp.concatenate([jnp.broadcast_to(sink_ref[h:h + 1, 0:1], (tq, 1)) for h in heads], axis=0)
    s = _dot_nt(qs, kk)
    if valid is not None:
        s = jnp.where(jnp.concatenate([valid] * B_GROUP, axis=0), s, NEG_INF)
    m = jnp.maximum(jnp.max(s, axis=1, keepdims=True), sink)
    pv = _dot(jnp.exp(s - m).astype(BF16), vv1)
    den = pv[:, HEAD_DIM:HEAD_DIM + 1] + jnp.exp(sink - m)
    o = pv[:, :HEAD_DIM] / den
    for gq, h in enumerate(heads):
        o_ref[:, h * HEAD_DIM:(h + 1) * HEAD_DIM] = o[gq * tq:(gq + 1) * tq]


def _ctx_attn_kernel(q_ref, k_ref, v_ref, sink_ref, o_ref):
    q = q_ref[...]
    k = k_ref[...].astype(BF16)
    v = v_ref[...]
    for kvh in range(B_KV_HEADS):
        ks = slice(kvh * HEAD_DIM, (kvh + 1) * HEAD_DIM)
        _group_attention(q, k[:, ks], _value_with_ones(v, kvh), None, sink_ref, kvh, o_ref)


def context_attention(qn, kn, proj, grp, sink):
    seq = grp.seq
    blk0 = grp.row0 // seq
    sink_t = jnp.broadcast_to(sink.reshape(B_HEADS, 1), (B_HEADS, LANES))
    return pl.pallas_call(
        _ctx_attn_kernel,
        grid=(grp.bsz,),
        in_specs=[pl.BlockSpec((seq, B_WIDTH), lambda b: (b, 0)),
                  pl.BlockSpec((seq, KV_WIDTH), lambda b: (b, 0)),
                  pl.BlockSpec((seq, KV_WIDTH), lambda b: (blk0 + b, V_COL // KV_WIDTH)),
                  pl.BlockSpec((B_HEADS, LANES), lambda b: (0, 0))],
        out_specs=pl.BlockSpec((seq, B_WIDTH), lambda b: (b, 0)),
        out_shape=jax.ShapeDtypeStruct((grp.tokens, B_WIDTH), F32),
        compiler_params=_cparams(("arbitrary",)),
        name="context_attention",
    )(qn, kn, proj, sink_t)


def _lat_attn_kernel(q_ref, kp_ref, kc_ref, kn_ref, vp_ref, vc_ref, vn_ref, kx_ref, vx_ref, sink_ref, o_ref):
    blk = pl.program_id(1)
    nblk = pl.num_programs(1)
    tq = q_ref.shape[0]
    q = q_ref[...]
    kl = jnp.concatenate([kp_ref[...], kc_ref[...], kn_ref[...]], axis=0).astype(BF16)
    vl = jnp.concatenate([vp_ref[...], vc_ref[...], vn_ref[...]], axis=0)
    n_ctx = kx_ref.shape[2]
    span = 3 * tq
    i = lax.broadcasted_iota(jnp.int32, (tq, span + n_ctx), 0)
    j = lax.broadcasted_iota(jnp.int32, (tq, span + n_ctx), 1)
    dist = j - tq - i
    valid = (dist >= -WINDOW) & (dist <= WINDOW)
    valid = valid & ((j >= tq) | (blk > 0)) & ((j < 2 * tq) | (blk < nblk - 1))
    valid = valid | (j >= span)
    for kvh in range(B_KV_HEADS):
        ks = slice(kvh * HEAD_DIM, (kvh + 1) * HEAD_DIM)
        kk = jnp.concatenate([kl[:, ks], kx_ref[0, kvh].astype(BF16)], axis=0)
        vv1 = jnp.concatenate([_value_with_ones(vl, kvh), vx_ref[0, kvh].astype(BF16)], axis=0)
        _group_attention(q, kk, vv1, valid, sink_ref, kvh, o_ref)


def latent_attention(qr, kr, proj, grp, k_ctx, v_ctx, sink):
    tq = WINDOW
    nblk = grp.seq // tq
    blk0 = grp.row0 // tq
    n_ctx = k_ctx.shape[2]
    sink_t = jnp.broadcast_to(sink.reshape(B_HEADS, 1), (B_HEADS, LANES))
    v_ctx1 = jnp.concatenate([v_ctx, jnp.ones_like(v_ctx)], axis=-1)

    def kv_specs(off, col):
        return [pl.BlockSpec((tq, KV_WIDTH), lambda b, i: (off + b * nblk + jnp.maximum(i - 1, 0), col)),
                pl.BlockSpec((tq, KV_WIDTH), lambda b, i: (off + b * nblk + i, col)),
                pl.BlockSpec((tq, KV_WIDTH), lambda b, i: (off + b * nblk + jnp.minimum(i + 1, nblk - 1), col))]

    return pl.pallas_call(
        _lat_attn_kernel,
        grid=(grp.bsz, nblk),
        in_specs=[pl.BlockSpec((tq, B_WIDTH), lambda b, i: (b * nblk + i, 0))]
                 + kv_specs(0, 0) + kv_specs(blk0, V_COL // KV_WIDTH)
                 + [pl.BlockSpec((1, B_KV_HEADS, n_ctx, HEAD_DIM), lambda b, i: (b, 0, 0, 0)),
                    pl.BlockSpec((1, B_KV_HEADS, n_ctx, 2 * HEAD_DIM), lambda b, i: (b, 0, 0, 0)),
                    pl.BlockSpec((B_HEADS, LANES), lambda b, i: (0, 0))],
        out_specs=pl.BlockSpec((tq, B_WIDTH), lambda b, i: (b * nblk + i, 0)),
        out_shape=jax.ShapeDtypeStruct((grp.tokens, B_WIDTH), F32),
        compiler_params=_cparams(("arbitrary", "arbitrary")),
        name="latent_attention",
    )(qr, kr, kr, kr, proj, proj, proj, k_ctx, v_ctx1, sink_t)


def s5_operators(a_re, a_im, log_dt, b_re, b_im, c_re, c_im):
    t = S5_CHUNK
    hi = lax.Precision.HIGHEST
    ks, ws, wsw, vs, a1s, a2s = [], [], [], [], [], []
    for d in range(2):
        are, aim = a_re[d].astype(F32), a_im[d].astype(F32)
        dt = jnp.exp(log_dt[d].astype(F32))[:, None]
        den = are * are + aim * aim
        steps = jnp.arange(t + 1, dtype=F32)[:, None, None]
        mag = jnp.exp(steps * (dt * are))
        pw_re = mag * jnp.cos(steps * (dt * aim))
        pw_im = mag * jnp.sin(steps * (dt * aim))
        ab_re, ab_im = pw_re[1], pw_im[1]
        f_re = ((ab_re - 1.0) * are + ab_im * aim) / den
        f_im = (ab_im * are - (ab_re - 1.0) * aim) / den
        bre, bim = b_re[d].astype(F32), b_im[d].astype(F32)
        bb_re = f_re[..., None] * bre - f_im[..., None] * bim
        bb_im = f_re[..., None] * bim + f_im[..., None] * bre
        cre, cim = c_re[d].astype(F32), c_im[d].astype(F32)
        pgr = pw_re.transpose(1, 0, 2)[:, :, None, :]
        pgi = pw_im.transpose(1, 0, 2)[:, :, None, :]
        cp_re = cre[:, None] * pgr - cim[:, None] * pgi
        cp_im = cre[:, None] * pgi + cim[:, None] * pgr
        m = (jnp.einsum('gkcp,gpd->gkcd', cp_re[:, :t], bb_re, precision=hi)
             - jnp.einsum('gkcp,gpd->gkcd', cp_im[:, :t], bb_im, precision=hi))
        s_i = jnp.arange(t)[:, None]
        t_i = jnp.arange(t)[None, :]
        lag = (t_i - s_i) if d == 0 else (s_i - t_i)
        blk = jnp.where((lag >= 0)[None, :, :, None, None], m[:, jnp.clip(lag, 0, t - 1)], 0.0)
        ks.append(blk.transpose(0, 1, 4, 2, 3).reshape(-1, S5_TC, S5_TC))
        pidx = (t - 1 - jnp.arange(t)) if d == 0 else jnp.arange(t)
        pr = pw_re[pidx].transpose(1, 0, 2)[:, :, None, :]
        pi = pw_im[pidx].transpose(1, 0, 2)[:, :, None, :]
        bbr = bb_re.transpose(0, 2, 1)[:, None]
        bbi = bb_im.transpose(0, 2, 1)[:, None]
        w_re = pr * bbr - pi * bbi
        w_im = pr * bbi + pi * bbr
        ws.append(jnp.concatenate([w_re, w_im], axis=-1).reshape(-1, S5_TC, 2 * S5_STATE))
        wsw.append(jnp.concatenate([w_im, w_re], axis=-1).reshape(-1, S5_TC, 2 * S5_STATE))
        kidx = (jnp.arange(t) + 1) if d == 0 else (t - jnp.arange(t))
        v = jnp.concatenate([cp_re[:, kidx], -cp_im[:, kidx]], axis=-1)
        vs.append(v.transpose(0, 3, 1, 2).reshape(-1, 2 * S5_STATE, S5_TC))
        a1s.append(jnp.concatenate([pw_re[t], pw_re[t]], axis=-1))
        a2s.append(jnp.concatenate([-pw_im[t], pw_im[t]], axis=-1))
    return (ks[0] + ks[1], jnp.concatenate(ws + wsw, axis=-1), jnp.concatenate(vs, axis=1),
            jnp.concatenate(a1s, axis=-1), jnp.concatenate(a2s, axis=-1))


def _to_chunks_kernel(x_ref, nw_ref, sc_ref, sh_ref, o_ref, hbuf):
    h = _norm_mod(x_ref[...], nw_ref[...], sc_ref[0], sh_ref[0])
    gpl = LANES // S5_GROUP
    for c in range(hbuf.shape[0]):
        hbuf[c] = h[:, c * LANES:(c + 1) * LANES]
    for c in range(hbuf.shape[0]):
        for t in range(S5_CHUNK):
            rows = hbuf[c, pl.ds(t, TOK_TILE // S5_CHUNK, stride=S5_CHUNK), :]
            for g in range(gpl):
                o_ref[c * gpl + g, :, t * S5_GROUP:(t + 1) * S5_GROUP] = rows[:, g * S5_GROUP:(g + 1) * S5_GROUP]


def s5_to_chunks(x, grp, nw, mod, rowmap):
    d = x.shape[1]
    ng = d // S5_GROUP
    tm = TOK_TILE
    st = grp.seq_tiles

    def mspec(k):
        return pl.BlockSpec((1, 1, d), lambda i: (rowmap(grp.tile0 + i), 0, k))

    return pl.pallas_call(
        _to_chunks_kernel,
        grid=(grp.tiles,),
        in_specs=[pl.BlockSpec((tm, d), lambda i: (grp.tile0 + i, 0)),
                  pl.BlockSpec((1, d), lambda i: (0, 0)), mspec(1), mspec(0)],
        out_specs=pl.BlockSpec((ng, CPT, S5_TC), lambda i: (0, (i % st) * grp.bsz + i // st, 0)),
        out_shape=jax.ShapeDtypeStruct((ng, grp.tokens // S5_CHUNK, S5_TC), F32),
        scratch_shapes=[pltpu.VMEM((d // LANES, tm, LANES), F32)],
        compiler_params=_cparams(("arbitrary",)),
        name="s5_to_chunks",
    )(x, nw, mod, mod)


def _s5_states_kernel(uf_ref, ub_ref, w_ref, a1_ref, a2_ref, s0_ref, stf_ref, stb_ref, fin_ref, dbuf, carry, *, bsz):
    r = pl.program_id(1)
    gb, tr, _ = uf_ref.shape
    sw = STATE_W
    span = CPT * bsz
    nspan = tr // span

    @pl.when(r == 0)
    def _():
        for g in range(gb):
            s0 = s0_ref[g]
            for d in range(2):
                s = s0[:, d * sw:(d + 1) * sw]
                carry[g, 2 * d] = s
                carry[g, 2 * d + 1] = pltpu.roll(s, sw // 2, axis=1)

    for g in range(gb):
        w = w_ref[g]
        for d, u_ref in enumerate((uf_ref, ub_ref)):
            dd = _dot(u_ref[g].astype(BF16), w[:, 2 * d * sw:2 * (d + 1) * sw])
            dbuf[g, d, 0] = dd[:, :sw]
            dbuf[g, d, 1] = dd[:, sw:]

    for g in range(gb):
        a1 = a1_ref[g]
        a2 = a2_ref[g]
        for d, st_ref in enumerate((stf_ref, stb_ref)):
            a1d = a1[:, d * sw:(d + 1) * sw]
            a2d = a2[:, d * sw:(d + 1) * sw]
            s = carry[g, 2 * d]
            x = carry[g, 2 * d + 1]
            for c in range(nspan * CPT):
                cc = c if d == 0 else nspan * CPT - 1 - c
                first = (cc // CPT) * span + cc % CPT
                for b in range(bsz):
                    st_ref[first + b * CPT:first + b * CPT + 1, g * sw:(g + 1) * sw] = s[b:b + 1]
                own = dbuf[g, d, 0, pl.ds(first, bsz, stride=CPT), :]
                swp = dbuf[g, d, 1, pl.ds(first, bsz, stride=CPT), :]
                s, x = a1d * s + a2d * x + own, a1d * x - a2d * s + swp
            carry[g, 2 * d] = s
            carry[g, 2 * d + 1] = x

    @pl.when(r == pl.num_programs(1) - 1)
    def _():
        for g in range(gb):
            fin_ref[g, :, :sw] = carry[g, 0]
            fin_ref[g, :, sw:] = carry[g, 2]


def s5_chunk_states_scan(u, w_bf16, a1, a2, s0g, bsz):
    g, r, tc = u.shape
    n = w_bf16.shape[2]
    sw = STATE_W
    gb = 8
    tr = min(r, 256)
    nblk = r // tr
    return pl.pallas_call(
        functools.partial(_s5_states_kernel, bsz=bsz),
        grid=(g // gb, nblk),
        in_specs=[pl.BlockSpec((gb, tr, tc), lambda i, j: (i, j, 0)),
                  pl.BlockSpec((gb, tr, tc), lambda i, j: (i, nblk - 1 - j, 0)),
                  pl.BlockSpec((gb, tc, n), lambda i, j: (i, 0, 0)),
                  pl.BlockSpec((gb, 1, 2 * sw), lambda i, j: (i, 0, 0)),
                  pl.BlockSpec((gb, 1, 2 * sw), lambda i, j: (i, 0, 0)),
                  pl.BlockSpec((gb, bsz, 2 * sw), lambda i, j: (i, 0, 0))],
        out_specs=[pl.BlockSpec((tr, gb * sw), lambda i, j: (j, i)),
                   pl.BlockSpec((tr, gb * sw), lambda i, j: (nblk - 1 - j, i)),
                   pl.BlockSpec((gb, bsz, 2 * sw), lambda i, j: (i, 0, 0))],
        out_shape=[jax.ShapeDtypeStruct((r, g * sw), F32),
                   jax.ShapeDtypeStruct((r, g * sw), F32),
                   jax.ShapeDtypeStruct((g, bsz, 2 * sw), F32)],
        scratch_shapes=[pltpu.VMEM((gb, 2, 2, tr, sw), F32), pltpu.VMEM((gb, 4, bsz, sw), F32)],
        compiler_params=_cparams(("arbitrary", "arbitrary")),
        name="s5_chunk_states_scan",
    )(u, u, w_bf16, a1.reshape(g, 1, 2 * sw), a2.reshape(g, 1, 2 * sw), s0g)


def _s5_out_kernel(u_ref, sf_ref, sb_ref, k_ref, v_ref, o_ref):
    gb = u_ref.shape[0]
    sw = STATE_W
    for g in range(gb):
        s = jnp.concatenate([sf_ref[:, g * sw:(g + 1) * sw], sb_ref[:, g * sw:(g + 1) * sw]], axis=1)
        o_ref[g] = _dot(u_ref[g].astype(BF16), k_ref[g]) + _dot(s.astype(BF16), v_ref[g])


def s5_outputs(u, st_f, st_b, k_bf16, v_bf16):
    g, r, tc = u.shape
    sw = STATE_W
    gb = 8
    tr = min(r, 512)
    return pl.pallas_call(
        _s5_out_kernel,
        grid=(g // gb, r // tr),
        in_specs=[pl.BlockSpec((gb, tr, tc), lambda i, j: (i, j, 0)),
                  pl.BlockSpec((tr, gb * sw), lambda i, j: (j, i)),
                  pl.BlockSpec((tr, gb * sw), lambda i, j: (j, i)),
                  pl.BlockSpec((gb, tc, tc), lambda i, j: (i, 0, 0)),
                  pl.BlockSpec((gb, 2 * sw, tc), lambda i, j: (i, 0, 0))],
        out_specs=pl.BlockSpec((gb, tr, tc), lambda i, j: (i, j, 0)),
        out_shape=jax.ShapeDtypeStruct((g, r, tc), F32),
        compiler_params=_cparams(("arbitrary", "arbitrary")),
        name="s5_outputs",
    )(u, st_f, st_b, k_bf16, v_bf16)


def s5_mixer(x, grp, nw, mod, rowmap, ops, s0):
    k_tot, w_tot, v_tot, a1, a2 = ops
    ng = k_tot.shape[0]
    bsz = grp.bsz
    sw = STATE_W
    u = s5_to_chunks(x, grp, nw, mod, rowmap)
    w = jnp.concatenate([w_tot[:, :, :sw], w_tot[:, :, 2 * sw:3 * sw], w_tot[:, :, sw:2 * sw], w_tot[:, :, 3 * sw:]], axis=-1)
    s0g = s0.transpose(3, 0, 1, 2, 4).reshape(ng, bsz, 2 * sw)
    st_f, st_b, final = s5_chunk_states_scan(u, w.astype(BF16), a1, a2, s0g, bsz)
    y = s5_outputs(u, st_f, st_b, k_tot.astype(BF16), v_tot.astype(BF16))
    final = final.reshape(ng, bsz, 2, 2, S5_STATE).transpose(1, 2, 3, 0, 4)
    return y, final


def _glu_kernel(yc_ref, yl_ref, x_ref, nw_ref, sc_ref, sh_ref, g1_ref, dsk_ref, wa_ref, wb_ref, o_ref, ybuf,
                *, first_tiles):
    gpl = LANES // S5_GROUP

    def from_chunks(y_ref):
        cpt = y_ref.shape[1]
        for c in range(ybuf.shape[0]):
            for t in range(S5_CHUNK):
                for g in range(gpl):
                    ybuf[c, t * cpt:(t + 1) * cpt, g * S5_GROUP:(g + 1) * S5_GROUP] = (
                        y_ref[c * gpl + g, :, t * S5_GROUP:(t + 1) * S5_GROUP])

    in_first = pl.program_id(0) < first_tiles

    @pl.when(in_first)
    def _():
        from_chunks(yc_ref)

    @pl.when(jnp.logical_not(in_first))
    def _():
        from_chunks(yl_ref)

    cpt = TOK_TILE // S5_CHUNK
    y = jnp.concatenate(
        [jnp.concatenate([ybuf[c, pl.ds(j, S5_CHUNK, stride=cpt), :] for c in range(ybuf.shape[0])], axis=1)
         for j in range(cpt)], axis=0)
    x = x_ref[...]
    y = y + dsk_ref[...] * _norm_mod(x, nw_ref[...], sc_ref[0], sh_ref[0])
    yb = jax.nn.gelu(y, approximate=True).astype(BF16)
    a = _dot(yb, wa_ref[...])
    b = _dot(yb, wb_ref[...])
    o_ref[...] = x + g1_ref[0] * (a * _sigmoid(b))


def glu_residual(y_chunks, x, nw, mod, rowmap, groups, dskip, wa_bf16, wb_bf16):
    n_tok, d = x.shape
    tm = TOK_TILE
    ng = d // S5_GROUP
    first, second = groups
    blk = (ng, CPT, S5_TC)

    def rows_of(grp, i):
        return (i % grp.seq_tiles) * grp.bsz + i // grp.seq_tiles

    return pl.pallas_call(
        functools.partial(_glu_kernel, first_tiles=first.tiles),
        grid=(n_tok // tm,),
        in_specs=[pl.BlockSpec(blk, lambda i: (0, rows_of(first, jnp.minimum(i, first.tiles - 1)), 0)),
                  pl.BlockSpec(blk, lambda i: (0, rows_of(second, jnp.maximum(i - first.tiles, 0)), 0)),
                  pl.BlockSpec((tm, d), lambda i: (i, 0)),
                  pl.BlockSpec((1, d), lambda i: (0, 0)),
                  _mod_spec(rowmap, 1, d), _mod_spec(rowmap, 0, d), _mod_spec(rowmap, 2, d),
                  pl.BlockSpec((1, d), lambda i: (0, 0)),
                  pl.BlockSpec((d, d), lambda i: (0, 0)),
                  pl.BlockSpec((d, d), lambda i: (0, 0))],
        out_specs=pl.BlockSpec((tm, d), lambda i: (i, 0)),
        out_shape=jax.ShapeDtypeStruct((n_tok, d), F32),
        scratch_shapes=[pltpu.VMEM((d // LANES, tm, LANES), F32)],
        compiler_params=_cparams(("arbitrary",)),
        name="glu_residual",
    )(y_chunks[0], y_chunks[1], x, nw, mod, mod, mod, dskip, wa_bf16, wb_bf16)


def _router_kernel(x_ref, nw_ref, sc_ref, sh_ref, wr_ref, h_ref, aff_ref):
    d = x_ref.shape[1]
    tm = x_ref.shape[0]
    h = _norm_mod(x_ref[...], nw_ref[...], sc_ref[0], sh_ref[0])
    h_ref[:, :d] = h
    tok = pl.program_id(0) * tm + lax.broadcasted_iota(jnp.int32, (tm, LANES), 0)
    h_ref[:, d:] = tok.astype(F32)
    logits = _dot_nt(wr_ref[...], h.astype(BF16))
    ex = jnp.exp(logits - jnp.max(logits, axis=0, keepdims=True))
    p = ex / jnp.sum(ex, axis=0, keepdims=True)
    for k in range(aff_ref.shape[0]):
        aff_ref[k] = p[:, k * LANES:(k + 1) * LANES]


def moe_router(x, nw, mod, rowmap, wr_t_bf16):
    n_tok, d = x.shape
    tm = TOK_TILE
    ne = wr_t_bf16.shape[0]
    return pl.pallas_call(
        _router_kernel,
        grid=(n_tok // tm,),
        in_specs=[pl.BlockSpec((tm, d), lambda i: (i, 0)),
                  pl.BlockSpec((1, d), lambda i: (0, 0)),
                  _mod_spec(rowmap, 4, d), _mod_spec(rowmap, 3, d),
                  pl.BlockSpec((ne, d), lambda i: (0, 0))],
        out_specs=[pl.BlockSpec((tm, d + LANES), lambda i: (i, 0)),
                   pl.BlockSpec((tm // LANES, ne, LANES), lambda i: (i, 0, 0))],
        out_shape=[jax.ShapeDtypeStruct((n_tok, d + LANES), F32),
                   jax.ShapeDtypeStruct((n_tok // LANES, ne, LANES), F32)],
        compiler_params=_cparams(("arbitrary",)),
        name="moe_router",
    )(x, nw, mod, mod, wr_t_bf16)


def _select_kernel(aff_ref, ut_ref, v8_ref, cidx_ref, meta_ref, inc_ref, *, cap, first, slot0):
    nt, ne, _ = aff_ref.shape
    aff = aff_ref[...]

    def count(mask):
        c = jnp.sum(jnp.where(mask, 1.0, 0.0), axis=0)
        return jnp.sum(c, axis=1, keepdims=True)

    def as_float(bits):
        return lax.bitcast_convert_type(bits, F32)

    def radix(k, bits):
        cand = bits | (jnp.int32(1) << (30 - k))
        return jnp.where(count(aff >= as_float(cand)[None]) >= cap, cand, bits)

    thr_bits = lax.fori_loop(0, 31, radix, jnp.zeros((ne, 1), jnp.int32))
    thr = as_float(thr_bits)[None]
    nxt = as_float(thr_bits + 1)[None]
    above = aff >= nxt
    bucket = (aff >= thr) & jnp.logical_not(above)
    need = cap - count(above)
    width = nxt - thr
    pos = jnp.where(bucket & (width > 0.0), (aff - thr) / width, 0.0)

    def refine(k, t):
        cand = t + lax.convert_element_type(jnp.int32(1) << (29 - k), F32) * (2.0 ** -30)
        return jnp.where(count(bucket & (pos >= cand[None])) >= need, cand, t)

    t = lax.fori_loop(0, 30, refine, jnp.zeros((ne, 1), F32))
    upper = bucket & (pos >= (t + 2.0 ** -30)[None])
    tied = bucket & (pos >= t[None]) & jnp.logical_not(upper)
    ut = ut_ref[...]

    def excl_rank(mask):
        m = jnp.where(mask, 1.0, 0.0)
        inc_ref[...] = _dot(m.reshape(nt * ne, LANES).astype(BF16), ut).reshape(nt, ne, LANES)

        def body(tt, carry):
            inc = inc_ref[tt]
            inc_ref[tt] = inc + carry
            return carry + inc[:, LANES - 1:LANES]

        lax.fori_loop(0, nt, body, jnp.zeros((ne, 1), F32))
        return inc_ref[...] - m

    sel = above | upper | (tied & (excl_rank(tied) < (need - count(upper))[None]))
    rank = excl_rank(sel)
    start = rank[:, :, 0:1]
    nsel = rank[:, :, LANES - 1:LANES] + jnp.where(sel[:, :, LANES - 1:LANES], 1.0, 0.0) - start
    lane3 = lax.broadcasted_iota(jnp.int32, (nt, ne, LANES), 2)
    meta_ref[...] = jnp.where(lane3 == 0, start + float(slot0), jnp.where(lane3 == 1, nsel, 0.0)).astype(jnp.int32)

    inc_ref[...] = jnp.where(sel, rank - start, -1.0)
    sub = lax.broadcasted_iota(jnp.int32, (LANES, LANES), 0).astype(F32)
    v8 = v8_ref[...]

    def tile_body(tt, carry):
        rho = inc_ref[tt]
        base = lax.convert_element_type(first + tt * LANES, F32)
        for e in range(ne):
            onehot = jnp.where(sub == rho[e:e + 1, :], 1.0, 0.0).astype(BF16)
            packed = _dot_nt(v8, onehot)
            cidx_ref[tt, e:e + 1, :] = (packed[0:1] + base).astype(jnp.int32)
        return carry

    lax.fori_loop(0, nt, tile_body, 0)


def moe_select(aff_t, cap, first, slot0):
    nt, ne, _ = aff_t.shape
    ut = jnp.triu(jnp.ones((LANES, LANES), BF16))
    v8 = jnp.zeros((8, LANES), BF16).at[0].set(jnp.arange(LANES).astype(BF16))
    blk = pl.BlockSpec((nt, ne, LANES), lambda i: (0, 0, 0))
    return pl.pallas_call(
        functools.partial(_select_kernel, cap=cap, first=first, slot0=slot0),
        grid=(1,),
        in_specs=[blk, pl.BlockSpec((LANES, LANES), lambda i: (0, 0)), pl.BlockSpec((8, LANES), lambda i: (0, 0))],
        out_specs=[blk, blk],
        out_shape=[jax.ShapeDtypeStruct((nt, ne, LANES), jnp.int32),
                   jax.ShapeDtypeStruct((nt, ne, LANES), jnp.int32)],
        scratch_shapes=[pltpu.VMEM((nt, ne, LANES), F32)],
        compiler_params=_cparams(("arbitrary",)),
        name="moe_select",
    )(aff_t, ut, v8)


def _lists_kernel(starts_ref, counts_ref, cidx_ref, idx_ref):
    nt, ne, _ = cidx_ref.shape
    idx_ref[...] = jnp.zeros_like(idx_ref)
    lane = lax.broadcasted_iota(jnp.int32, (1, LANES), 1)

    def tile(t, carry):
        for e in range(ne):
            s = starts_ref[e, t]
            c = counts_ref[e, t]
            j0 = s >> 7
            o = s & (LANES - 1)
            rolled = pltpu.roll(cidx_ref[t, e:e + 1, :], o, axis=1)
            end = o + c
            row0 = idx_ref[e, pl.ds(j0, 1), :]
            idx_ref[e, pl.ds(j0, 1), :] = jnp.where((lane >= o) & (lane < end), rolled, row0)
            row1 = idx_ref[e, pl.ds(j0 + 1, 1), :]
            idx_ref[e, pl.ds(j0 + 1, 1), :] = jnp.where(lane < end - LANES, rolled, row1)
        return carry

    lax.fori_loop(0, nt, tile, 0)


def moe_build_lists(starts, counts, cidx, rows):
    nt, ne, _ = cidx.shape
    rt = rows // LANES + 2
    out = pl.pallas_call(
        _lists_kernel,
        grid_spec=pltpu.PrefetchScalarGridSpec(
            num_scalar_prefetch=2,
            grid=(1,),
            in_specs=[pl.BlockSpec((nt, ne, LANES), lambda i, s, c: (0, 0, 0))],
            out_specs=pl.BlockSpec((ne, rt, LANES), lambda i, s, c: (0, 0, 0))),
        out_shape=jax.ShapeDtypeStruct((ne, rt, LANES), jnp.int32),
        compiler_params=_cparams(("arbitrary",)),
        name="moe_build_lists",
    )(starts, counts, cidx)
    return out.reshape(ne * rt * LANES)


def _expert_kernel(idx_ref, h_hbm, wr_ref, wg_ref, wu_ref, wd_ref, y_ref, xbuf, wgb, wub, wdb, sem, *, rows_pad):
    e = pl.program_id(0)
    ch = pl.program_id(1)
    nch = pl.num_programs(1)
    tr = xbuf.shape[1]
    d, f = wgb.shape
    step = e * nch + ch
    slot = step % 2

    @pl.when(ch == 0)
    def _():
        wgb[...] = wg_ref[0, 0].astype(BF16)
        wub[...] = wu_ref[0, 0].astype(BF16)
        wdb[...] = wd_ref[0, 0].astype(BF16)

    def row_copy(base, r, slot_i):
        return pltpu.make_async_copy(h_hbm.at[pl.ds(idx_ref[base + r], 1)], xbuf.at[slot_i, pl.ds(r, 1)], sem.at[slot_i])

    @pl.when(step == 0)
    def _():
        def group(j, carry):
            for u in range(8):
                row_copy(0, 8 * j + u, 0).start()
            return carry

        lax.fori_loop(0, tr // 8, group, 0)

    has_next = step + 1 < pl.num_programs(0) * nch
    last = ch == nch - 1
    e_n = jnp.where(has_next, jnp.where(last, e + 1, e), e)
    ch_n = jnp.where(has_next, jnp.where(last, 0, ch + 1), ch)
    base_n = e_n * rows_pad + ch_n * tr

    pltpu.make_async_copy(h_hbm.at[pl.ds(0, tr)], xbuf.at[slot], sem.at[slot]).wait()

    x = xbuf[slot, :, :d].astype(BF16)
    logits = _dot(x, wr_ref[...])
    lane = lax.broadcasted_iota(jnp.int32, logits.shape, 1)
    ne = pl.num_programs(0)
    logits = jnp.where(lane < ne, logits, NEG_INF)
    ex = jnp.exp(logits - jnp.max(logits, axis=1, keepdims=True))
    gate = jnp.sum(jnp.where(lane == e, ex, 0.0), axis=1, keepdims=True) / jnp.sum(ex, axis=1, keepdims=True)

    fb = f // FFN_SPLIT
    rb = tr // FFN_SPLIT
    y = None
    for nb in range(FFN_SPLIT):
        for r in range(nb * rb, (nb + 1) * rb):
            row_copy(base_n, r, 1 - slot).start(priority=r % 2)
        cols = slice(nb * fb, (nb + 1) * fb)
        hmid = (_silu(_dot(x, wgb[:, cols])) * _dot(x, wub[:, cols])).astype(BF16)
        part = _dot(hmid, wdb[cols, :])
        y = part if y is None else y + part
    y_ref[0, :, :d] = y * gate
    y_ref[0, :, d:] = xbuf[slot, :, d:]

    @pl.when(jnp.logical_not(has_next))
    def _():
        pltpu.make_async_copy(h_hbm.at[pl.ds(0, tr)], xbuf.at[1 - slot], sem.at[1 - slot]).wait()


def moe_experts(idx, rows, h_ext, wr_pad_bf16, w_gate, w_up, w_down, layer):
    _, ne, d, f = w_gate.shape
    rows_pad = idx.shape[0] // ne
    dx = h_ext.shape[1]
    tr = 512
    return pl.pallas_call(
        functools.partial(_expert_kernel, rows_pad=rows_pad),
        grid_spec=pltpu.PrefetchScalarGridSpec(
            num_scalar_prefetch=1,
            grid=(ne, rows // tr),
            in_specs=[pl.BlockSpec(memory_space=pl.ANY),
                      pl.BlockSpec((d, LANES), lambda e, c, idx: (0, 0)),
                      pl.BlockSpec((1, 1, d, f), lambda e, c, idx: (layer, e, 0, 0)),
                      pl.BlockSpec((1, 1, d, f), lambda e, c, idx: (layer, e, 0, 0)),
                      pl.BlockSpec((1, 1, f, d), lambda e, c, idx: (layer, e, 0, 0))],
            out_specs=pl.BlockSpec((1, tr, dx), lambda e, c, idx: (e, c, 0)),
            scratch_shapes=[pltpu.VMEM((2, tr, dx), F32),
                            pltpu.VMEM((d, f), BF16), pltpu.VMEM((d, f), BF16), pltpu.VMEM((f, d), BF16),
                            pltpu.SemaphoreType.DMA((2,))]),
        out_shape=jax.ShapeDtypeStruct((ne, rows, dx), F32),
        compiler_params=_cparams(("arbitrary", "arbitrary")),
        name="moe_experts",
    )(idx, h_ext, wr_pad_bf16, w_gate, w_up, w_down)


def _combine_kernel(starts_ref, y_hbm, x_ref, g2_ref, *rest, first_tiles):
    if first_tiles is None:
        o_ref, acc, stage, sem = rest
    else:
        oc_ref, ol_ref, acc, stage, sem = rest
    tb = pl.program_id(0)
    tm, d = x_ref.shape
    ne = y_hbm.shape[0]
    slot = tb % 2

    def chunk_copy(e, src_row, dst_row, slot_i):
        return pltpu.make_async_copy(y_hbm.at[e, pl.ds(src_row, 8)], stage.at[slot_i, pl.ds(dst_row, 8)], sem.at[slot_i])

    def spans(tile):
        out = []
        for e in range(ne):
            s0 = starts_ref[e, tile]
            s1 = starts_ref[e, tile + 1]
            a = (s0 >> 3) << 3
            out.append((a, jnp.where(s1 > s0, (s1 - a + 7) >> 3, 0)))
        return out

    def fetch(tile, slot_i):
        off = jnp.int32(0)
        for e, (a, nchunk) in enumerate(spans(tile)):
            def issue(c, carry, e=e, a=a, off=off):
                chunk_copy(e, pl.multiple_of(a + 8 * c, 8), pl.multiple_of(off + 8 * c, 8), slot_i).start()
                return carry

            lax.fori_loop(0, nchunk, issue, 0)
            off = off + 8 * nchunk

    @pl.when(tb == 0)
    def _():
        stage[...] = jnp.zeros(stage.shape, F32)
        fetch(tb, slot)

    @pl.when(tb + 1 < pl.num_programs(0))
    def _():
        fetch(tb + 1, 1 - slot)

    off = jnp.int32(0)
    for _, nchunk in spans(tb):
        off = off + 8 * nchunk

    def drain(c, carry):
        chunk_copy(0, 0, 0, slot).wait()
        return carry

    lax.fori_loop(0, off >> 3, drain, 0)

    acc[...] = jnp.zeros_like(acc)
    want = (lax.broadcasted_iota(jnp.int32, (tm, KCH), 0) + tb * tm).astype(F32)

    def fold(kc, carry):
        rows = stage[slot, pl.ds(pl.multiple_of(kc * KCH, KCH), KCH), :]
        tok = rows[:, d:].T[0:1, :]
        fresh = lax.broadcasted_iota(jnp.int32, (tm, KCH), 1) + kc * KCH < off
        onehot = jnp.where((want == tok) & fresh, 1.0, 0.0).astype(BF16)
        y = rows[:, :d]
        hi = y.astype(BF16)
        lo = (y - hi.astype(F32)).astype(BF16)
        acc[...] += _dot(onehot, hi) + _dot(onehot, lo)
        return carry

    lax.fori_loop(0, (off + KCH - 1) >> 8, fold, 0)
    res = x_ref[...] + g2_ref[0] * acc[...]
    if first_tiles is None:
        o_ref[...] = res
    else:
        @pl.when(tb < first_tiles)
        def _():
            oc_ref[...] = res

        @pl.when(tb >= first_tiles)
        def _():
            ol_ref[...] = res


def moe_combine(starts, y, x, mod, rowmap, split=None):
    n_tok, d = x.shape
    ne, _, dx = y.shape
    tm = TOK_TILE
    stage_rows = -(-(ne * tm + ne * 16) // KCH) * KCH
    if split is None:
        out_specs = pl.BlockSpec((tm, d), lambda i, s: (i, 0))
        out_shape = jax.ShapeDtypeStruct((n_tok, d), F32)
        first_tiles = None
    else:
        first_tiles = split[0]
        out_specs = [pl.BlockSpec((tm, d), lambda i, s: (jnp.minimum(i, first_tiles - 1), 0)),
                     pl.BlockSpec((tm, d), lambda i, s: (jnp.maximum(i - first_tiles, 0), 0))]
        out_shape = [jax.ShapeDtypeStruct((split[0] * tm, d), F32), jax.ShapeDtypeStruct((split[1] * tm, d), F32)]
    return pl.pallas_call(
        functools.partial(_combine_kernel, first_tiles=first_tiles),
        grid_spec=pltpu.PrefetchScalarGridSpec(
            num_scalar_prefetch=1,
            grid=(n_tok // tm,),
            in_specs=[pl.BlockSpec(memory_space=pl.ANY),
                      pl.BlockSpec((tm, d), lambda i, s: (i, 0)),
                      pl.BlockSpec((1, 1, d), lambda i, s: (rowmap(i), 0, 5))],
            out_specs=out_specs,
            scratch_shapes=[pltpu.VMEM((tm, d), F32), pltpu.VMEM((2, stage_rows, dx), F32),
                            pltpu.SemaphoreType.DMA((2,))]),
        out_shape=out_shape,
        compiler_params=_cparams(("arbitrary",)),
        name="moe_combine",
    )(starts, y, x, mod)


def moe_layer(x, nw, mod, rowmap, groups, wr, w_gate, w_up, w_down, layer, split=False):
    n_tok, d = x.shape
    ne = wr.shape[1]
    h_ext, aff_t = moe_router(x, nw, mod, rowmap, wr.T.astype(BF16))
    cidx_parts, meta_parts = [], []
    rows = 0
    for grp in groups:
        cap = EC_FACTOR * grp.tokens // ne
        t0 = grp.row0 // LANES
        cidx, meta = moe_select(aff_t[t0:t0 + grp.tokens // LANES], cap, grp.row0, rows)
        cidx_parts.append(cidx)
        meta_parts.append(meta[:, :, :2])
        rows += cap
    cidx = jnp.concatenate(cidx_parts, axis=0)
    meta = jnp.concatenate(meta_parts, axis=0)
    starts = meta[:, :, 0].T
    counts = meta[:, :, 1].T
    idx = moe_build_lists(starts, counts, cidx, rows)
    per = TOK_TILE // LANES
    starts_blk = jnp.concatenate([starts[:, ::per], jnp.full((ne, 1), rows, jnp.int32)], axis=1)
    wr_pad = jnp.zeros((d, LANES), BF16).at[:, :ne].set(wr.astype(BF16))
    y = moe_experts(idx, rows, h_ext, wr_pad, w_gate, w_up, w_down, layer)
    return moe_combine(starts_blk, y, x, mod, rowmap, (groups[0].tiles, groups[1].tiles) if split else None)


def kernel(x_prompt, x_sample, cache_k, cache_v, state_hgrn, state_s5, c, c_ctx, norm_w, ada_w, ada_b, w_in_ab, w_out_ab, hgrn_lb_logits, hgrn_norm_w, q_norm_w, k_norm_w, attn_sink, s5_a_re, s5_a_im, s5_log_dt, s5_b_re, s5_b_im, s5_c_re, s5_c_im, s5_d, glu_w_a, glu_w_b, router_w, exp_w_gate, exp_w_up, exp_w_down):
    b_ctx, l_ctx, d = x_prompt.shape
    b_lat, l_lat, _ = x_sample.shape
    depth = norm_w.shape[0]
    ctx = Group(0, b_ctx, l_ctx)
    lat = Group(ctx.tokens, b_lat, l_lat)
    groups = (ctx, lat)

    def rowmap(i):
        return jnp.where(i < ctx.tiles, i // ctx.seq_tiles, b_ctx + (i - ctx.tiles) // lat.seq_tiles)

    cond = jnp.concatenate([c_ctx[None, :], c, jnp.zeros((8 - 1 - b_lat, d), F32)], axis=0)
    mod_small = ada_modulation(cond, ada_w, ada_b)
    seq_rows = jnp.concatenate([jnp.zeros((b_ctx,), jnp.int32), 1 + jnp.arange(b_lat, dtype=jnp.int32)])
    mods = mod_small[:, seq_rows][:, :, None, :]

    x = jnp.concatenate([x_prompt.reshape(ctx.tokens, d), x_sample.reshape(lat.tokens, d)], axis=0)
    rope = rope_tables(l_lat)
    ks, vs, hs, ss = [], [], [], []
    for l in range(depth):
        mod = mods[l]
        nw1 = norm_w[l, 0].reshape(1, d)
        nw2 = norm_w[l, 1].reshape(1, d)
        if l % 2 == 0:
            e = l // 2
            proj = norm_mod_matmul(x, nw1, mod, rowmap, 1, 0, w_in_ab[e].astype(BF16))
            zero_state = jnp.zeros((b_ctx, 2, A_HEADS, A_DK, A_DK), F32)
            of_c, ob_c, st_c = hgrn2_mixer(proj, ctx, hgrn_lb_logits, zero_state, e)
            of_l, ob_l, _ = hgrn2_mixer(proj, lat, hgrn_lb_logits, jnp.swapaxes(state_hgrn[:, e], -1, -2), e)
            hs.append(jnp.swapaxes(st_c, -1, -2))
            qn_c, kn_c = qk_prepare(proj, ctx, q_norm_w[e], k_norm_w[e], None)
            att_c = context_attention(qn_c, kn_c, proj, ctx, attn_sink[e])
            ks.append(kn_c.reshape(b_ctx, l_ctx, B_KV_HEADS, HEAD_DIM).transpose(0, 2, 1, 3))
            vs.append(proj[:ctx.tokens, V_COL:].reshape(b_ctx, l_ctx, B_KV_HEADS, HEAD_DIM).transpose(0, 2, 1, 3))
            qr_l, kr_l = qk_prepare(proj, lat, q_norm_w[e], k_norm_w[e], rope)
            att_l = latent_attention(qr_l, kr_l, proj, lat, cache_k[:, e], cache_v[:, e], attn_sink[e])
            x = even_out_proj((of_c, of_l), (ob_c, ob_l), (att_c, att_l), proj, x, mod, rowmap, groups,
                              hgrn_norm_w[e].reshape(1, A_DK), w_out_ab[e].astype(BF16))
        else:
            o = l // 2
            ops = s5_operators(s5_a_re[o], s5_a_im[o], s5_log_dt[o], s5_b_re[o], s5_b_im[o], s5_c_re[o], s5_c_im[o])
            zero_s5 = jnp.zeros((b_ctx, 2, 2, d // S5_GROUP, S5_STATE), F32)
            y_c, fin_c = s5_mixer(x, ctx, nw1, mod, rowmap, ops, zero_s5)
            y_l, _ = s5_mixer(x, lat, nw1, mod, rowmap, ops, state_s5[:, o])
            ss.append(fin_c)
            x = glu_residual((y_c, y_l), x, nw1, mod, rowmap, groups, s5_d[o].reshape(1, d),
                             glu_w_a[o].astype(BF16), glu_w_b[o].astype(BF16))
        x = moe_layer(x, nw2, mod, rowmap, groups, router_w[l], exp_w_gate, exp_w_up, exp_w_down, l,
                      split=(l == depth - 1))
    y_prompt = x[0].reshape(b_ctx, l_ctx, d)
    y_sample = x[1].reshape(b_lat, l_lat, d)
    return (y_prompt, y_sample, jnp.stack(ks, axis=1), jnp.stack(vs, axis=1),
            jnp.stack(hs, axis=1), jnp.stack(ss, axis=1))
```
